```python
import jax
import jax.numpy as jnp
from jax import lax
import numpy as np

D_MODEL = 1024
BATCH = 4
SEQ = 4096
DEPTH = 2

GRID_W = 64
CTX_LEN = 256
HEAD_DIM = 64
N_Q_HEADS = 8
N_KV_HEADS = 2
ATTN_WIDTH = N_Q_HEADS * HEAD_DIM
KV_WIDTH = 2 * N_KV_HEADS * HEAD_DIM
CONV_CH = D_MODEL - ATTN_WIDTH
CONF_K = 31
ROPE_AXIS_DIM = HEAD_DIM // 2
ROPE_THETA = 10000.0
Q_BLOCK = 128
EVEN_IN = ATTN_WIDTH + KV_WIDTH + 2 * CONV_CH
SC_K = 3
N_EXPERTS = 16
EC_CAPACITY = 2
EXPERT_FF = 1024
N_MOD = 6
EPS = 1e-6
N_EVEN = (DEPTH + 1) // 2
N_ODD = DEPTH // 2

kernel_name = "hybrid_attn_conformer_shortconv_ec_moe_dit"


def rms_norm(x, g):
    xf = x.astype(jnp.float32)
    y = xf * lax.rsqrt(jnp.mean(xf * xf, axis=-1, keepdims=True) + EPS)
    return (y * g.astype(jnp.float32)).astype(x.dtype)


def layer_norm(x, g, b):
    xf = x.astype(jnp.float32)
    mu = jnp.mean(xf, axis=-1, keepdims=True)
    var = jnp.mean(jnp.square(xf - mu), axis=-1, keepdims=True)
    y = (xf - mu) * lax.rsqrt(var + EPS)
    return (y * g.astype(jnp.float32) + b.astype(jnp.float32)).astype(x.dtype)


def ada_mod(cvec, w, b):
    m = jax.nn.silu(cvec) @ w + b
    return m.reshape(cvec.shape[:-1] + (N_MOD, cvec.shape[-1]))


def modulate(xn, shift, scale):
    return xn * (1 + scale) + shift


def axial_rope_tables(n):
    rows = n // GRID_W
    row = jnp.repeat(jnp.arange(rows, dtype=jnp.float32), GRID_W)
    col = jnp.tile(jnp.arange(GRID_W, dtype=jnp.float32), rows)
    inv = ROPE_THETA ** (-jnp.arange(0, ROPE_AXIS_DIM, 2, dtype=jnp.float32) / ROPE_AXIS_DIM)
    ang = jnp.concatenate([row[:, None] * inv, col[:, None] * inv], axis=-1)
    return jnp.cos(ang)[None, :, None, :], jnp.sin(ang)[None, :, None, :]


def apply_rope(x, cos, sin):
    half = x.shape[-1] // 2
    xf = x.astype(jnp.float32)
    x1, x2 = xf[..., :half], xf[..., half:]
    return jnp.concatenate([x1 * cos - x2 * sin, x2 * cos + x1 * sin], axis=-1).astype(x.dtype)


def depthwise_conv(x, w):
    k = w.shape[0]
    return lax.conv_general_dilated(
        x, w[:, None, :].astype(x.dtype), window_strides=(1,), padding=[(k // 2, k // 2)],
        dimension_numbers=("NWC", "WIO", "NWC"), feature_group_count=x.shape[-1])


def split_kv(kv, k_g):
    bsz, n, _ = kv.shape
    half = KV_WIDTH // 2
    k = kv[..., :half].reshape(bsz, n, N_KV_HEADS, HEAD_DIM)
    v = kv[..., half:].reshape(bsz, n, N_KV_HEADS, HEAD_DIM)
    return rms_norm(k, k_g), v


def block_attention(q, k, v):
    bsz, n, hq, hd = q.shape
    grp = hq // N_KV_HEADS
    nb = n // Q_BLOCK
    qb = q.reshape(bsz, nb, Q_BLOCK, N_KV_HEADS, grp, hd).transpose(1, 0, 2, 3, 4, 5)
    scale = HEAD_DIM ** -0.5

    def one_block(qblk):
        s = jnp.einsum("bqkgd,btkd->bkgqt", qblk, k).astype(jnp.float32) * scale
        p = jax.nn.softmax(s, axis=-1).astype(v.dtype)
        return jnp.einsum("bkgqt,btkd->bqkgd", p, v)

    o = lax.map(one_block, qb)
    return o.transpose(1, 0, 2, 3, 4, 5).reshape(bsz, n, hq * hd)


def context_kv(hcn, w_in, k_g):
    return split_kv(hcn @ w_in[:, ATTN_WIDTH:ATTN_WIDTH + KV_WIDTH], k_g)


def even_mixer(hm, w_in, q_g, k_g, conv_w, conv_b, ln_g, ln_b, w_out, rope, kv_prefix):
    bsz, n, _ = hm.shape
    proj = hm @ w_in
    q = rms_norm(proj[..., :ATTN_WIDTH].reshape(bsz, n, N_Q_HEADS, HEAD_DIM), q_g)
    k, v = split_kv(proj[..., ATTN_WIDTH:ATTN_WIDTH + KV_WIDTH], k_g)
    if rope is not None:
        q = apply_rope(q, *rope)
        k = apply_rope(k, *rope)
    own_kv = (k, v)
    if kv_prefix is not None:
        k = jnp.concatenate([kv_prefix[0], k], axis=1)
        v = jnp.concatenate([kv_prefix[1], v], axis=1)
    attn = block_attention(q, k, v)
    c0 = ATTN_WIDTH + KV_WIDTH
    u = proj[..., c0:c0 + CONV_CH]
    gate = proj[..., c0 + CONV_CH:]
    g = u * jax.nn.sigmoid(gate)
    g = depthwise_conv(g, conv_w) + conv_b
    g = jax.nn.silu(layer_norm(g, ln_g, ln_b))
    return jnp.concatenate([attn, g], axis=-1) @ w_out, own_kv


def short_conv_mixer(hm, w_in, conv_w, w_out):
    proj = hm @ w_in
    d = hm.shape[-1]
    b_gate, c_gate, xv = proj[..., :d], proj[..., d:2 * d], proj[..., 2 * d:]
    y = depthwise_conv(c_gate * xv, conv_w)
    return (b_gate * y) @ w_out


def ec_moe(h, w_r, w_gate, w_up, w_down):
    bsz, n, _ = h.shape
    cap = max(1, EC_CAPACITY * n // N_EXPERTS)
    aff = jax.nn.softmax(jnp.einsum("bnd,de->bne", h, w_r).astype(jnp.float32), axis=-1)
    gates, idx = lax.top_k(jnp.swapaxes(aff, 1, 2), cap)
    bidx = jnp.broadcast_to(jnp.arange(bsz)[:, None, None], idx.shape)
    xs = h[bidx, idx]
    hid = jax.nn.silu(jnp.einsum("becd,edf->becf", xs, w_gate)) * jnp.einsum("becd,edf->becf", xs, w_up)
    ys = jnp.einsum("becf,efd->becd", hid, w_down) * gates[..., None].astype(h.dtype)
    return jnp.zeros_like(h).at[bidx, idx].add(ys)


def setup_inputs(seed: int = 0) -> dict:
    key = jax.random.key(seed)
    ks = iter(jax.random.split(key, 32))

    def nrm(shape, s):
        return jax.random.normal(next(ks), shape, jnp.float32) * s

    def gain(shape):
        return 1.0 + nrm(shape, 0.1)

    L, D, E, F = DEPTH, D_MODEL, N_EXPERTS, EXPERT_FF
    return {
        "x": nrm((BATCH, SEQ, D), 1.0),
        "c": nrm((BATCH, D), 1.0),
        "ctx": nrm((BATCH, CTX_LEN, D), 1.0),
        "c_ctx": nrm((D,), 1.0),
        "ada_w": nrm((L, D, N_MOD * D), 0.5 * D ** -0.5),
        "ada_b": nrm((L, N_MOD * D), 0.1),
        "norm1_g": gain((L, D)),
        "norm2_g": gain((L, D)),
        "ev_w_in": nrm((N_EVEN, D, EVEN_IN), D ** -0.5),
        "ev_q_g": gain((N_EVEN, HEAD_DIM)),
        "ev_k_g": gain((N_EVEN, HEAD_DIM)),
        "ev_conv_w": nrm((N_EVEN, CONF_K, CONV_CH), CONF_K ** -0.5),
        "ev_conv_b": nrm((N_EVEN, CONV_CH), 0.02),
        "ev_ln_g": gain((N_EVEN, CONV_CH)),
        "ev_ln_b": nrm((N_EVEN, CONV_CH), 0.02),
        "ev_w_out": nrm((N_EVEN, ATTN_WIDTH + CONV_CH, D), (ATTN_WIDTH + CONV_CH) ** -0.5),
        "sc_w_in": nrm((N_ODD, D, 3 * D), D ** -0.5),
        "sc_conv_w": nrm((N_ODD, SC_K, D), SC_K ** -0.5),
        "sc_w_out": nrm((N_ODD, D, D), D ** -0.5),
        "moe_w_r": nrm((L, D, E), D ** -0.5),
        "moe_w_gate": nrm((L, E, D, F), D ** -0.5),
        "moe_w_up": nrm((L, E, D, F), D ** -0.5),
        "moe_w_down": nrm((L, E, F, D), F ** -0.5),
        "final_g": gain((D,)),
    }


def reference(x, c, ctx, c_ctx, ada_w, ada_b, norm1_g, norm2_g,
              ev_w_in, ev_q_g, ev_k_g, ev_conv_w, ev_conv_b, ev_ln_g, ev_ln_b, ev_w_out,
              sc_w_in, sc_conv_w, sc_w_out,
              moe_w_r, moe_w_gate, moe_w_up, moe_w_down, final_g):
    rope = axial_rope_tables(x.shape[1])
    h, hc = x, ctx
    for i in range(DEPTH):
        is_even = i % 2 == 0
        ctx_later = any(j % 2 == 0 for j in range(i + 1, DEPTH))
        m = ada_mod(c, ada_w[i], ada_b[i])[:, None]
        hn = modulate(rms_norm(h, norm1_g[i]), m[:, :, 0], m[:, :, 1])
        if is_even or ctx_later:
            mc = ada_mod(c_ctx, ada_w[i], ada_b[i])
            hcn = modulate(rms_norm(hc, norm1_g[i]), mc[0], mc[1])
        if is_even:
            e = i // 2
            prm = (ev_w_in[e], ev_q_g[e], ev_k_g[e], ev_conv_w[e], ev_conv_b[e],
                   ev_ln_g[e], ev_ln_b[e], ev_w_out[e])
            if ctx_later:
                ctx_out, kv_ctx = even_mixer(hcn, *prm, None, None)
                hc = hc + mc[2] * ctx_out
            else:
                kv_ctx = context_kv(hcn, ev_w_in[e], ev_k_g[e])
            lat_out, _ = even_mixer(hn, *prm, rope, kv_ctx)
            h = h + m[:, :, 2] * lat_out
        else:
            o = i // 2
            prm = (sc_w_in[o], sc_conv_w[o], sc_w_out[o])
            h = h + m[:, :, 2] * short_conv_mixer(hn, *prm)
            if ctx_later:
                hc = hc + mc[2] * short_conv_mixer(hcn, *prm)
        moe = (moe_w_r[i], moe_w_gate[i], moe_w_up[i], moe_w_down[i])
        h = h + m[:, :, 5] * ec_moe(modulate(rms_norm(h, norm2_g[i]), m[:, :, 3], m[:, :, 4]), *moe)
        if ctx_later:
            hc = hc + mc[5] * ec_moe(modulate(rms_norm(hc, norm2_g[i]), mc[3], mc[4]), *moe)
    return rms_norm(h, final_g)
```

```python
import functools

import jax
import jax.numpy as jnp
from jax import lax
from jax.experimental import pallas as pl
from jax.experimental.pallas import tpu as pltpu

F32 = jnp.float32
BF16 = jnp.bfloat16

HEAD_DIM = 64
N_Q_HEADS = 8
N_KV_HEADS = 2
GRID_W = 64
ROPE_THETA = 10000.0
CONF_K = 31
SC_K = 3
N_EXPERTS = 16
EC_CAPACITY = 2
N_MOD = 6
EPS = 1e-6

ATTN_W = N_Q_HEADS * HEAD_DIM
KV_W = N_KV_HEADS * HEAD_DIM
Q_PER_KV = N_Q_HEADS // N_KV_HEADS

LANES = 128
SUBLANES = 8
MXU_DIM = 256
VMEM_LIMIT_BYTES = 60000 * 1024

CONV_HALO = 16
ROUTER_CHUNK = MXU_DIM
REFINE_STEPS = 24


def _params(*sem):
    return pltpu.CompilerParams(dimension_semantics=sem, vmem_limit_bytes=VMEM_LIMIT_BYTES)


def _norm_mod(x, g, shift, scale):
    ms = jnp.mean(x * x, axis=-1, keepdims=True)
    y = x * lax.rsqrt(ms + EPS) * g
    return y * (1.0 + scale) + shift


def _head_sumsq(x, ones_blockdiag):
    return jnp.dot((x * x).astype(BF16), ones_blockdiag, preferred_element_type=F32)


def _swap_half(x):
    w = x.shape[-1]
    lane = lax.broadcasted_iota(jnp.int32, x.shape, 1)
    first = (lane % HEAD_DIM) < (HEAD_DIM // 2)
    return jnp.where(first, pltpu.roll(x, w - HEAD_DIM // 2, axis=1), pltpu.roll(x, HEAD_DIM // 2, axis=1))


def _silu(x):
    return x * jax.nn.sigmoid(x)


def _ada_body(c_ref, w_ref, b_ref, o_ref):
    s = _silu(c_ref[...]).astype(BF16)
    o_ref[0] = jnp.dot(s, w_ref[0].astype(BF16), preferred_element_type=F32) + b_ref[0]


def _ada_mod(cvecs, ada_w, ada_b):
    n_layers, d, n_out = ada_w.shape
    tn = n_out // 4
    return pl.pallas_call(
        _ada_body,
        grid=(n_layers, n_out // tn),
        in_specs=[
            pl.BlockSpec((SUBLANES, d), lambda l, j: (0, 0)),
            pl.BlockSpec((1, d, tn), lambda l, j: (l, 0, j)),
            pl.BlockSpec((1, 1, tn), lambda l, j: (l, 0, j)),
        ],
        out_specs=pl.BlockSpec((1, SUBLANES, tn), lambda l, j: (l, 0, j)),
        out_shape=jax.ShapeDtypeStruct((n_layers, SUBLANES, n_out), F32),
        compiler_params=_params("arbitrary", "arbitrary"),
        name="ada_mod",
    )(cvecs, ada_w, ada_b.reshape(n_layers, 1, n_out))


def _inproj0_body(h_ref, mod_ref, g1_ref, w_ref, qg_ref, kg_ref, cos_ref, sin_ref, ones_ref,
                  q_ref, kt_ref, v_ref, glu_ref):
    hn = _norm_mod(h_ref[0], g1_ref[...], mod_ref[0, 0:1, :], mod_ref[0, 1:2, :])
    proj = jnp.dot(hn.astype(BF16), w_ref[...], preferred_element_type=F32)
    ones = ones_ref[...]
    cos2 = cos_ref[...]
    sin2 = sin_ref[...]

    q = proj[:, :ATTN_W]
    ssq = jnp.concatenate([_head_sumsq(q[:, :MXU_DIM], ones), _head_sumsq(q[:, MXU_DIM:], ones)], axis=1)
    qn = q * lax.rsqrt(ssq * (1.0 / HEAD_DIM) + EPS) * qg_ref[...]
    cos = jnp.concatenate([cos2] * (ATTN_W // LANES), axis=1)
    sin = jnp.concatenate([sin2] * (ATTN_W // LANES), axis=1)
    qr = (qn * cos + _swap_half(qn) * sin) * (HEAD_DIM ** -0.5)
    for h in range(N_Q_HEADS):
        q_ref[0, h] = qr[:, h * HEAD_DIM:(h + 1) * HEAD_DIM].astype(BF16)

    k = proj[:, ATTN_W:ATTN_W + KV_W]
    kn = k * lax.rsqrt(_head_sumsq(k, ones[:KV_W, :KV_W]) * (1.0 / HEAD_DIM) + EPS) * kg_ref[...]
    kr = kn * cos2 + _swap_half(kn) * sin2
    kt_ref[0] = kr.T.astype(BF16)
    v_ref[0] = proj[:, ATTN_W + KV_W:ATTN_W + 2 * KV_W].astype(BF16)

    c0 = ATTN_W + 2 * KV_W
    cc = (proj.shape[1] - c0) // 2
    glu_ref[0] = proj[:, c0:c0 + cc] * jax.nn.sigmoid(proj[:, c0 + cc:])


def _inproj0(h, mods, g1, w_in, qg, kg, cos2, sin2, ones_bd, tm=512):
    b, n, d = h.shape
    n_in = w_in.shape[1]
    conv_ch = (n_in - ATTN_W - 2 * KV_W) // 2
    return pl.pallas_call(
        _inproj0_body,
        grid=(b, n // tm),
        in_specs=[
            pl.BlockSpec((1, tm, d), lambda bi, i: (bi, i, 0)),
            pl.BlockSpec((1, N_MOD, d), lambda bi, i: (bi, 0, 0)),
            pl.BlockSpec((1, d), lambda bi, i: (0, 0)),
            pl.BlockSpec((d, n_in), lambda bi, i: (0, 0)),
            pl.BlockSpec((1, ATTN_W), lambda bi, i: (0, 0)),
            pl.BlockSpec((1, KV_W), lambda bi, i: (0, 0)),
            pl.BlockSpec((tm, LANES), lambda bi, i: (i, 0)),
            pl.BlockSpec((tm, LANES), lambda bi, i: (i, 0)),
            pl.BlockSpec((MXU_DIM, MXU_DIM), lambda bi, i: (0, 0)),
        ],
        out_specs=[
            pl.BlockSpec((1, N_Q_HEADS, tm, HEAD_DIM), lambda bi, i: (bi, 0, i, 0)),
            pl.BlockSpec((1, KV_W, tm), lambda bi, i: (bi, 0, i)),
            pl.BlockSpec((1, tm, KV_W), lambda bi, i: (bi, i, 0)),
            pl.BlockSpec((1, tm, conv_ch), lambda bi, i: (bi, i, 0)),
        ],
        out_shape=[
            jax.ShapeDtypeStruct((b, N_Q_HEADS, n, HEAD_DIM), BF16),
            jax.ShapeDtypeStruct((b, KV_W, n), BF16),
            jax.ShapeDtypeStruct((b, n, KV_W), BF16),
            jax.ShapeDtypeStruct((b, n, conv_ch), F32),
        ],
        compiler_params=_params("arbitrary", "arbitrary"),
        name="inproj0",
    )(h, mods, g1, w_in, qg, kg, cos2, sin2, ones_bd)


def _ctxkv_body(x_ref, mod_ref, g1_ref, w_ref, kg_ref, ones_ref, kt_ref, v_ref):
    hn = _norm_mod(x_ref[0], g1_ref[...], mod_ref[0, 0:1, :], mod_ref[0, 1:2, :])
    proj = jnp.dot(hn.astype(BF16), w_ref[...], preferred_element_type=F32)
    k = proj[:, :KV_W]
    kn = k * lax.rsqrt(_head_sumsq(k, ones_ref[...][:KV_W, :KV_W]) * (1.0 / HEAD_DIM) + EPS) * kg_ref[...]
    kt_ref[0] = kn.T.astype(BF16)
    v_ref[0] = proj[:, KV_W:].astype(BF16)


def _ctxkv(ctx, mods, ctx_row, g1, w_kv, kg, ones_bd):
    b, t, d = ctx.shape
    return pl.pallas_call(
        _ctxkv_body,
        grid=(b,),
        in_specs=[
            pl.BlockSpec((1, t, d), lambda bi: (bi, 0, 0)),
            pl.BlockSpec((1, N_MOD, d), lambda bi: (ctx_row, 0, 0)),
            pl.BlockSpec((1, d), lambda bi: (0, 0)),
            pl.BlockSpec((d, 2 * KV_W), lambda bi: (0, 0)),
            pl.BlockSpec((1, KV_W), lambda bi: (0, 0)),
            pl.BlockSpec((MXU_DIM, MXU_DIM), lambda bi: (0, 0)),
        ],
        out_specs=[
            pl.BlockSpec((1, KV_W, t), lambda bi: (bi, 0, 0)),
            pl.BlockSpec((1, t, KV_W), lambda bi: (bi, 0, 0)),
        ],
        out_shape=[
            jax.ShapeDtypeStruct((b, KV_W, t), BF16),
            jax.ShapeDtypeStruct((b, t, KV_W), BF16),
        ],
        compiler_params=_params("arbitrary"),
        name="ctx_kv",
    )(ctx, mods, g1, w_kv, kg, ones_bd)


def _attn_body(q_ref, ktc_ref, kt_ref, vc_ref, v_ref, o_ref):
    def one_head(h, carry):
        j = h // Q_PER_KV
        q = q_ref[0, h]
        s_c = jnp.dot(q, ktc_ref[0, j], preferred_element_type=F32)
        s_o = jnp.dot(q, kt_ref[0, j], preferred_element_type=F32)
        m = jnp.maximum(jnp.max(s_c, axis=-1, keepdims=True), jnp.max(s_o, axis=-1, keepdims=True))
        p_c = jnp.exp(s_c - m)
        p_o = jnp.exp(s_o - m)
        l = jnp.sum(p_c, axis=-1, keepdims=True) + jnp.sum(p_o, axis=-1, keepdims=True)
        o2 = (jnp.dot(p_c.astype(BF16), vc_ref[0], preferred_element_type=F32)
              + jnp.dot(p_o.astype(BF16), v_ref[0], preferred_element_type=F32))
        o = jnp.where(j == 0, o2[:, :HEAD_DIM], o2[:, HEAD_DIM:])
        o_ref[0, h] = (o / l).astype(BF16)
        return carry

    lax.fori_loop(0, N_Q_HEADS, one_head, 0)


def _attention(q, kt_ctx, kt, v_ctx, v, tq=128):
    b, hq, n, hd = q.shape
    t_ctx = kt_ctx.shape[-1]
    return pl.pallas_call(
        _attn_body,
        grid=(b, n // tq),
        in_specs=[
            pl.BlockSpec((1, hq, tq, hd), lambda bi, i: (bi, 0, i, 0)),
            pl.BlockSpec((1, N_KV_HEADS, hd, t_ctx), lambda bi, i: (bi, 0, 0, 0)),
            pl.BlockSpec((1, N_KV_HEADS, hd, n), lambda bi, i: (bi, 0, 0, 0)),
            pl.BlockSpec((1, t_ctx, KV_W), lambda bi, i: (bi, 0, 0)),
            pl.BlockSpec((1, n, KV_W), lambda bi, i: (bi, 0, 0)),
        ],
        out_specs=pl.BlockSpec((1, hq, tq, hd), lambda bi, i: (bi, 0, i, 0)),
        out_shape=jax.ShapeDtypeStruct((b, hq, n, hd), BF16),
        compiler_params=_params("arbitrary", "arbitrary"),
        name="attention",
    )(q, kt_ctx.reshape(b, N_KV_HEADS, hd, t_ctx), kt.reshape(b, N_KV_HEADS, hd, n), v_ctx, v)


def _conf_body(g_ref, w_ref, cb_ref, lg_ref, lb_ref, o_ref, xpad, *, tn):
    n = g_ref.shape[1]
    ch = g_ref.shape[2]
    xpad[0:CONV_HALO, :] = jnp.zeros((CONV_HALO, ch), F32)
    xpad[n + CONV_HALO:n + 2 * CONV_HALO, :] = jnp.zeros((CONV_HALO, ch), F32)
    xpad[CONV_HALO:n + CONV_HALO, :] = g_ref[0]
    win = tn + 2 * CONV_HALO
    off = CONV_HALO - CONF_K // 2

    def chunk(i, carry):
        r0 = pl.multiple_of(i * tn, tn)
        x = xpad[pl.ds(r0, win), :]
        acc = jnp.zeros((tn, ch), F32)
        for sub in range(SUBLANES):
            xs = x if sub == 0 else pltpu.roll(x, win - sub, axis=0)
            for a in range(win // SUBLANES):
                k = a * SUBLANES + sub - off
                if 0 <= k < CONF_K and a * SUBLANES + tn <= win:
                    acc = acc + xs[a * SUBLANES:a * SUBLANES + tn, :] * w_ref[k:k + 1, :]
        y = acc + cb_ref[...]
        mu = jnp.mean(y, axis=-1, keepdims=True)
        yc = y - mu
        var = jnp.mean(yc * yc, axis=-1, keepdims=True)
        z = yc * lax.rsqrt(var + EPS) * lg_ref[...] + lb_ref[...]
        o_ref[0, pl.ds(r0, tn), :] = _silu(z).astype(BF16)
        return carry

    lax.fori_loop(0, n // tn, chunk, 0)


def _conformer(glu, conv_w, conv_b, ln_g, ln_b, tn=64):
    b, n, ch = glu.shape
    return pl.pallas_call(
        functools.partial(_conf_body, tn=tn),
        grid=(b,),
        in_specs=[
            pl.BlockSpec((1, n, ch), lambda bi: (bi, 0, 0)),
            pl.BlockSpec((CONF_K, ch), lambda bi: (0, 0)),
            pl.BlockSpec((1, ch), lambda bi: (0, 0)),
            pl.BlockSpec((1, ch), lambda bi: (0, 0)),
            pl.BlockSpec((1, ch), lambda bi: (0, 0)),
        ],
        out_specs=pl.BlockSpec((1, n, ch), lambda bi: (bi, 0, 0)),
        out_shape=jax.ShapeDtypeStruct((b, n, ch), BF16),
        scratch_shapes=[pltpu.VMEM((n + 2 * CONV_HALO, ch), F32)],
        compiler_params=_params("arbitrary"),
        name="conformer",
    )(glu, conv_w, conv_b, ln_g, ln_b)


def _residual_router(mix, h_ref, mod_ref, g2_ref, wr_ref, h_out, hm_out, lg_out):
    h1 = h_ref[0] + mod_ref[0, 2:3, :] * mix
    h_out[0] = h1
    hm = _norm_mod(h1, g2_ref[...], mod_ref[0, 3:4, :], mod_ref[0, 4:5, :])
    hm_out[0] = hm
    lg_out[0] = jnp.dot(hm.astype(BF16), wr_ref[...], preferred_element_type=F32)


def _outproj0_body(attn_ref, conf_ref, w_ref, h_ref, mod_ref, g2_ref, wr_ref, h_out, hm_out, lg_out):
    a = jnp.concatenate([attn_ref[0, h] for h in range(N_Q_HEADS)] + [conf_ref[0]], axis=1)
    mix = jnp.dot(a, w_ref[...], preferred_element_type=F32)
    _residual_router(mix, h_ref, mod_ref, g2_ref, wr_ref, h_out, hm_out, lg_out)


def _token_out_specs(b, n, d, tm):
    specs = [
        pl.BlockSpec((1, tm, d), lambda bi, i: (bi, i, 0)),
        pl.BlockSpec((1, tm, d), lambda bi, i: (bi, i, 0)),
        pl.BlockSpec((1, tm, LANES), lambda bi, i: (bi, i, 0)),
    ]
    shapes = [
        jax.ShapeDtypeStruct((b, n, d), F32),
        jax.ShapeDtypeStruct((b, n, d), F32),
        jax.ShapeDtypeStruct((b, n, LANES), F32),
    ]
    return specs, shapes


def _outproj0(attn, conf, w_out, h, mods, g2, w_r, tm=512):
    b, n, d = h.shape
    ch = conf.shape[-1]
    out_specs, out_shape = _token_out_specs(b, n, d, tm)
    return pl.pallas_call(
        _outproj0_body,
        grid=(b, n // tm),
        in_specs=[
            pl.BlockSpec((1, N_Q_HEADS, tm, HEAD_DIM), lambda bi, i: (bi, 0, i, 0)),
            pl.BlockSpec((1, tm, ch), lambda bi, i: (bi, i, 0)),
            pl.BlockSpec((d, d), lambda bi, i: (0, 0)),
            pl.BlockSpec((1, tm, d), lambda bi, i: (bi, i, 0)),
            pl.BlockSpec((1, N_MOD, d), lambda bi, i: (bi, 0, 0)),
            pl.BlockSpec((1, d), lambda bi, i: (0, 0)),
            pl.BlockSpec((d, LANES), lambda bi, i: (0, 0)),
        ],
        out_specs=out_specs,
        out_shape=out_shape,
        compiler_params=_params("arbitrary", "arbitrary"),
        name="outproj0",
    )(attn, conf, w_out, h, mods, g2, w_r)


def _inproj1_body(h_ref, moe_ref, modp_ref, mod_ref, g1_ref, w_ref, h_out, bg_out, z_out):
    d = h_ref.shape[2]
    h = h_ref[0] + modp_ref[0, 5:6, :] * moe_ref[0]
    h_out[0] = h
    hn = _norm_mod(h, g1_ref[...], mod_ref[0, 0:1, :], mod_ref[0, 1:2, :])
    proj = jnp.dot(hn.astype(BF16), w_ref[...], preferred_element_type=F32)
    bg_out[0] = proj[:, :d]
    z_out[0] = proj[:, d:2 * d] * proj[:, 2 * d:]


def _inproj1(h, moe, mods_prev, mods, g1, w_in, tm=512):
    b, n, d = h.shape
    tok = pl.BlockSpec((1, tm, d), lambda bi, i: (bi, i, 0))
    modspec = pl.BlockSpec((1, N_MOD, d), lambda bi, i: (bi, 0, 0))
    return pl.pallas_call(
        _inproj1_body,
        grid=(b, n // tm),
        in_specs=[tok, tok, modspec, modspec,
                  pl.BlockSpec((1, d), lambda bi, i: (0, 0)),
                  pl.BlockSpec((d, 3 * d), lambda bi, i: (0, 0))],
        out_specs=[tok, tok, tok],
        out_shape=[jax.ShapeDtypeStruct((b, n, d), F32)] * 3,
        compiler_params=_params("arbitrary", "arbitrary"),
        name="inproj1",
    )(h, moe, mods_prev, mods, g1, w_in)


def _outproj1_body(z_ref, zp_ref, zn_ref, bg_ref, cw_ref, w_ref, h_ref, mod_ref, g2_ref, wr_ref,
                   h_out, hm_out, lg_out):
    i = pl.program_id(1)
    last = pl.num_programs(1) - 1
    z = z_ref[0]
    tm = z.shape[0]
    row = lax.broadcasted_iota(jnp.int32, z.shape, 0)
    prev_row = jnp.where(i > 0, zp_ref[0, SUBLANES - 1:SUBLANES, :], 0.0)
    next_row = jnp.where(i < last, zn_ref[0, 0:1, :], 0.0)
    z_dn = jnp.where(row == 0, prev_row, pltpu.roll(z, 1, axis=0))
    z_up = jnp.where(row == tm - 1, next_row, pltpu.roll(z, tm - 1, axis=0))
    y = z_dn * cw_ref[0:1, :] + z * cw_ref[1:2, :] + z_up * cw_ref[2:3, :]
    mix = jnp.dot((bg_ref[0] * y).astype(BF16), w_ref[...], preferred_element_type=F32)
    _residual_router(mix, h_ref, mod_ref, g2_ref, wr_ref, h_out, hm_out, lg_out)


def _outproj1(z, bg, conv_w, w_out, h, mods, g2, w_r, tm=512):
    b, n, d = h.shape
    per = tm // SUBLANES
    nblk8 = n // SUBLANES
    tok = pl.BlockSpec((1, tm, d), lambda bi, i: (bi, i, 0))
    out_specs, out_shape = _token_out_specs(b, n, d, tm)
    return pl.pallas_call(
        _outproj1_body,
        grid=(b, n // tm),
        in_specs=[
            tok,
            pl.BlockSpec((1, SUBLANES, d), lambda bi, i: (bi, jnp.maximum(i * per - 1, 0), 0)),
            pl.BlockSpec((1, SUBLANES, d), lambda bi, i: (bi, jnp.minimum((i + 1) * per, nblk8 - 1), 0)),
            tok,
            pl.BlockSpec((SC_K, d), lambda bi, i: (0, 0)),
            pl.BlockSpec((d, d), lambda bi, i: (0, 0)),
            tok,
            pl.BlockSpec((1, N_MOD, d), lambda bi, i: (bi, 0, 0)),
            pl.BlockSpec((1, d), lambda bi, i: (0, 0)),
            pl.BlockSpec((d, LANES), lambda bi, i: (0, 0)),
        ],
        out_specs=out_specs,
        out_shape=out_shape,
        compiler_params=_params("arbitrary", "arbitrary"),
        name="outproj1",
    )(z, z, z, bg, conv_w, w_out, h, mods, g2, w_r)


def _stack_chunks(x):
    n = x.shape[1]
    return jnp.concatenate([x[:, c * ROUTER_CHUNK:(c + 1) * ROUTER_CHUNK] for c in range(n // ROUTER_CHUNK)], axis=0)


def _exclusive_rank(flags, utri, chunk_lt):
    incl = jnp.dot(flags.astype(BF16), utri, preferred_element_type=F32)
    tot = jnp.broadcast_to(incl[:, ROUTER_CHUNK - 1:ROUTER_CHUNK], incl.shape)
    base = jnp.dot(chunk_lt, tot.astype(BF16), preferred_element_type=F32)
    return incl - flags + base


def _router_body(lg_ref, utri_ref, lt_ref, o_ref, slot_scr, v_scr, *, cap):
    n = lg_ref.shape[1]
    n_chunks = n // ROUTER_CHUNK
    lg = lg_ref[0]
    lane = lax.broadcasted_iota(jnp.int32, lg.shape, 1)
    valid = lane < N_EXPERTS
    x = jnp.where(valid, lg, -jnp.inf)
    ex = jnp.where(valid, jnp.exp(x - jnp.max(x, axis=-1, keepdims=True)), 0.0)
    aff = ex / jnp.sum(ex, axis=-1, keepdims=True)

    hi = aff.astype(BF16).astype(F32)
    r1 = aff - hi
    mid = r1.astype(BF16).astype(F32)
    lo = (r1 - mid).astype(BF16).astype(F32)
    tok = lax.broadcasted_iota(jnp.int32, lg.shape, 0)
    vals = (pltpu.roll(hi, N_EXPERTS, axis=1) + pltpu.roll(mid, 2 * N_EXPERTS, axis=1)
            + pltpu.roll(lo, 3 * N_EXPERTS, axis=1))
    vals = jnp.where(lane == 0, (tok // 64).astype(F32), jnp.where(lane == 1, (tok % 64).astype(F32), vals))
    v_scr[...] = vals.astype(BF16)

    aff_t = aff.T[:N_EXPERTS, :]

    def count_ge(t):
        return jnp.sum(jnp.where(aff_t >= t, 1.0, 0.0), axis=-1, keepdims=True)

    def bit_step(i, bits):
        cand = bits | jnp.left_shift(jnp.int32(1), 30 - i)
        return jnp.where(count_ge(lax.bitcast_convert_type(cand, F32)) >= cap, cand, bits)

    bits = lax.fori_loop(0, 31, bit_step, jnp.zeros((N_EXPERTS, 1), jnp.int32))

    def refine(i, lo_hi):
        lo, hi = lo_hi
        mid = (lo + hi) * 0.5
        ok = count_ge(mid) >= cap
        return jnp.where(ok, mid, lo), jnp.where(ok, hi, mid)

    thr, _ = lax.fori_loop(0, REFINE_STEPS, refine,
                           (lax.bitcast_convert_type(bits, F32), lax.bitcast_convert_type(bits + 1, F32)))
    gt = jnp.where(aff_t > thr, 1.0, 0.0)
    eq = jnp.where(aff_t == thr, 1.0, 0.0)
    need = cap - jnp.sum(gt, axis=-1, keepdims=True)
    utri = utri_ref[...]
    chunk_lt = lt_ref[...]
    gt_s = _stack_chunks(gt)
    eq_s = _stack_chunks(eq)
    need_s = jnp.concatenate([need] * n_chunks, axis=0)
    sel = jnp.where((gt_s > 0) | ((eq_s > 0) & (_exclusive_rank(eq_s, utri, chunk_lt) < need_s)), 1.0, 0.0)
    slot = _exclusive_rank(sel, utri, chunk_lt)
    slot_scr[...] = jnp.where(sel > 0, slot, -1.0)

    srow = lax.broadcasted_iota(jnp.int32, (cap, ROUTER_CHUNK), 0).astype(F32)
    olane = lax.broadcasted_iota(jnp.int32, (cap, LANES), 1)

    def one_expert(e, carry):
        res = jnp.zeros((cap, LANES), F32)
        for c in range(n_chunks):
            onehot = jnp.where(srow == slot_scr[pl.ds(c * N_EXPERTS + e, 1), :], 1.0, 0.0).astype(BF16)
            res = res + jnp.dot(onehot, v_scr[c * ROUTER_CHUNK:(c + 1) * ROUTER_CHUNK, :],
                                preferred_element_type=F32)
        idx = res[:, 0:1] * 64.0 + res[:, 1:2]
        is_gate = (olane == N_EXPERTS + e) | (olane == 2 * N_EXPERTS + e) | (olane == 3 * N_EXPERTS + e)
        gate = jnp.sum(jnp.where(is_gate, res, 0.0), axis=-1, keepdims=True)
        o_ref[0, e] = jnp.where(olane == 0, idx, jnp.where(olane == 1, gate, 0.0))
        return carry

    lax.fori_loop(0, N_EXPERTS, one_expert, 0)


def _router(logits, utri, chunk_lt, cap):
    b, n, _ = logits.shape
    rows = (n // ROUTER_CHUNK) * N_EXPERTS
    return pl.pallas_call(
        functools.partial(_router_body, cap=cap),
        grid=(b,),
        in_specs=[
            pl.BlockSpec((1, n, LANES), lambda bi: (bi, 0, 0)),
            pl.BlockSpec((ROUTER_CHUNK, ROUTER_CHUNK), lambda bi: (0, 0)),
            pl.BlockSpec((rows, rows), lambda bi: (0, 0)),
        ],
        out_specs=pl.BlockSpec((1, N_EXPERTS, cap, LANES), lambda bi: (bi, 0, 0, 0)),
        out_shape=jax.ShapeDtypeStruct((b, N_EXPERTS, cap, LANES), F32),
        scratch_shapes=[pltpu.VMEM((rows, ROUTER_CHUNK), F32), pltpu.VMEM((n, LANES), BF16)],
        compiler_params=_params("arbitrary"),
        name="router",
    )(logits, utri, chunk_lt)


GATHER_UNROLL = 8


def _gather_body(idx_ref, hm_ref, o_ref, rows_scr):
    cap = rows_scr.shape[0]

    def group(g, carry):
        base = pl.multiple_of(g * GATHER_UNROLL, GATHER_UNROLL)
        for u in range(GATHER_UNROLL):
            rows_scr[pl.ds(base + u, 1), :] = hm_ref[0, pl.ds(idx_ref[0, 0, base + u], 1), :]
        return carry

    lax.fori_loop(0, cap // GATHER_UNROLL, group, 0)
    o_ref[0, 0] = rows_scr[...].astype(BF16)


def _gather(idx, hm):
    b, n, d = hm.shape
    cap = idx.shape[-1]
    return pl.pallas_call(
        _gather_body,
        grid=(b, N_EXPERTS),
        in_specs=[
            pl.BlockSpec((1, 1, cap), lambda bi, e: (bi * N_EXPERTS + e, 0, 0), memory_space=pltpu.SMEM),
            pl.BlockSpec((1, n, d), lambda bi, e: (bi, 0, 0)),
        ],
        out_specs=pl.BlockSpec((1, 1, cap, d), lambda bi, e: (bi, e, 0, 0)),
        out_shape=jax.ShapeDtypeStruct((b, N_EXPERTS, cap, d), BF16),
        scratch_shapes=[pltpu.VMEM((cap, d), F32)],
        compiler_params=_params("arbitrary", "arbitrary"),
        name="moe_gather",
    )(idx, hm)


def _ffn_body(x_ref, gate_ref, wg_ref, wu_ref, wd_ref, o_ref, wg_s, wu_s, wd_s):
    @pl.when(pl.program_id(1) == 0)
    def _():
        wg_s[...] = wg_ref[0, 0].astype(BF16)
        wu_s[...] = wu_ref[0, 0].astype(BF16)
        wd_s[...] = wd_ref[0, 0].astype(BF16)

    x = x_ref[0, 0]
    hid = _silu(jnp.dot(x, wg_s[...], preferred_element_type=F32)) * jnp.dot(x, wu_s[...], preferred_element_type=F32)
    o_ref[0, 0] = jnp.dot(hid.astype(BF16), wd_s[...], preferred_element_type=F32) * gate_ref[0, 0]


def _expert_ffn(xs, gates, w_gate, w_up, w_down, layer):
    b, n_e, cap, d = xs.shape
    f = w_gate.shape[-1]
    return pl.pallas_call(
        _ffn_body,
        grid=(n_e, b),
        in_specs=[
            pl.BlockSpec((1, 1, cap, d), lambda e, bi: (bi, e, 0, 0)),
            pl.BlockSpec((1, 1, cap, 1), lambda e, bi: (bi, e, 0, 0)),
            pl.BlockSpec((1, 1, d, f), lambda e, bi: (layer, e, 0, 0)),
            pl.BlockSpec((1, 1, d, f), lambda e, bi: (layer, e, 0, 0)),
            pl.BlockSpec((1, 1, f, d), lambda e, bi: (layer, e, 0, 0)),
        ],
        out_specs=pl.BlockSpec((1, 1, cap, d), lambda e, bi: (bi, e, 0, 0)),
        out_shape=jax.ShapeDtypeStruct((b, n_e, cap, d), F32),
        scratch_shapes=[pltpu.VMEM((d, f), BF16), pltpu.VMEM((d, f), BF16), pltpu.VMEM((f, d), BF16)],
        compiler_params=_params("arbitrary", "arbitrary"),
        name="moe_ffn",
    )(xs, gates, w_gate, w_up, w_down)


SCATTER_UNROLL = 4


def _scatter_body(idx_ref, ys_ref, o_ref):
    cap = ys_ref.shape[2]

    @pl.when(pl.program_id(1) == 0)
    def _():
        o_ref[...] = jnp.zeros(o_ref.shape, F32)

    def group(g, carry):
        base = pl.multiple_of(g * SCATTER_UNROLL, SCATTER_UNROLL)
        rows = [idx_ref[0, 0, base + u] for u in range(SCATTER_UNROLL)]
        sums = [o_ref[0, pl.ds(rows[u], 1), :] + ys_ref[0, 0, pl.ds(base + u, 1), :] for u in range(SCATTER_UNROLL)]
        for u in range(SCATTER_UNROLL):
            o_ref[0, pl.ds(rows[u], 1), :] = sums[u]
        return carry

    lax.fori_loop(0, cap // SCATTER_UNROLL, group, 0)


def _scatter(idx, ys, n):
    b, n_e, cap, d = ys.shape
    return pl.pallas_call(
        _scatter_body,
        grid=(b, n_e),
        in_specs=[
            pl.BlockSpec((1, 1, cap), lambda bi, e: (bi * N_EXPERTS + e, 0, 0), memory_space=pltpu.SMEM),
            pl.BlockSpec((1, 1, cap, d), lambda bi, e: (bi, e, 0, 0)),
        ],
        out_specs=pl.BlockSpec((1, n, d), lambda bi, e: (bi, 0, 0)),
        out_shape=jax.ShapeDtypeStruct((b, n, d), F32),
        compiler_params=_params("arbitrary", "arbitrary"),
        name="moe_scatter",
    )(idx, ys)


def _ec_moe(hm, logits, w_gate, w_up, w_down, layer, utri, chunk_lt):
    b, n, d = hm.shape
    cap = max(1, EC_CAPACITY * n // N_EXPERTS)
    routed = _router(logits, utri, chunk_lt, cap)
    idx = routed[..., 0].astype(jnp.int32).reshape(b * N_EXPERTS, 1, cap)
    gates = routed[..., 1:2]
    xs = _gather(idx, hm)
    ys = _expert_ffn(xs, gates, w_gate, w_up, w_down, layer)
    return _scatter(idx, ys, n)


def _final_body(h_ref, moe_ref, mod_ref, g_ref, o_ref):
    h = h_ref[0] + mod_ref[0, 5:6, :] * moe_ref[0]
    o_ref[0] = h * lax.rsqrt(jnp.mean(h * h, axis=-1, keepdims=True) + EPS) * g_ref[...]


def _final(h, moe, mods, g, tm=512):
    b, n, d = h.shape
    tok = pl.BlockSpec((1, tm, d), lambda bi, i: (bi, i, 0))
    return pl.pallas_call(
        _final_body,
        grid=(b, n // tm),
        in_specs=[tok, tok, pl.BlockSpec((1, N_MOD, d), lambda bi, i: (bi, 0, 0)),
                  pl.BlockSpec((1, d), lambda bi, i: (0, 0))],
        out_specs=tok,
        out_shape=jax.ShapeDtypeStruct((b, n, d), F32),
        compiler_params=_params("arbitrary", "arbitrary"),
        name="final_norm",
    )(h, moe, mods, g)


def _rope_tables(n):
    rows = n // GRID_W
    row = jnp.repeat(jnp.arange(rows, dtype=F32), GRID_W)
    col = jnp.tile(jnp.arange(GRID_W, dtype=F32), rows)
    axis_dim = HEAD_DIM // 2
    inv = ROPE_THETA ** (-jnp.arange(0, axis_dim, 2, dtype=F32) / axis_dim)
    ang = jnp.concatenate([row[:, None] * inv, col[:, None] * inv], axis=-1)
    cos = jnp.cos(ang)
    sin = jnp.sin(ang)
    cos64 = jnp.concatenate([cos, cos], axis=-1)
    sin64 = jnp.concatenate([-sin, sin], axis=-1)
    reps = LANES // HEAD_DIM
    return jnp.tile(cos64, (1, reps)), jnp.tile(sin64, (1, reps))


def _router_tables(n):
    r = jnp.arange(ROUTER_CHUNK)
    utri = (r[:, None] <= r[None, :]).astype(BF16)
    rows = jnp.arange((n // ROUTER_CHUNK) * N_EXPERTS)
    same_e = (rows[:, None] % N_EXPERTS) == (rows[None, :] % N_EXPERTS)
    earlier = (rows[None, :] // N_EXPERTS) < (rows[:, None] // N_EXPERTS)
    return utri, (same_e & earlier).astype(BF16)


def _head_ones():
    r = jnp.arange(MXU_DIM) // HEAD_DIM
    return (r[:, None] == r[None, :]).astype(BF16)


def kernel(x, c, ctx, c_ctx, ada_w, ada_b, norm1_g, norm2_g, ev_w_in, ev_q_g, ev_k_g, ev_conv_w, ev_conv_b,
           ev_ln_g, ev_ln_b, ev_w_out, sc_w_in, sc_conv_w, sc_w_out, moe_w_r, moe_w_gate, moe_w_up,
           moe_w_down, final_g):
    b, n, d = x.shape
    depth = ada_w.shape[0]
    assert depth == 2 and b < SUBLANES and n % ROUTER_CHUNK == 0

    cos2, sin2 = _rope_tables(n)
    utri, chunk_lt = _router_tables(n)
    ones_bd = _head_ones()

    cvecs = jnp.zeros((SUBLANES, d), F32).at[:b].set(c).at[b].set(c_ctx)
    mods = _ada_mod(cvecs, ada_w, ada_b).reshape(depth, SUBLANES, N_MOD, d)
    w_r = jnp.pad(moe_w_r, ((0, 0), (0, 0), (0, LANES - N_EXPERTS))).astype(BF16)

    w_in0 = ev_w_in[0].astype(BF16)
    qg = jnp.tile(ev_q_g[0], N_Q_HEADS)[None, :]
    kg = jnp.tile(ev_k_g[0], N_KV_HEADS)[None, :]
    q, kt, v, glu = _inproj0(x, mods[0], norm1_g[0:1], w_in0, qg, kg, cos2, sin2, ones_bd)
    kt_ctx, v_ctx = _ctxkv(ctx, mods[0], b, norm1_g[0:1], w_in0[:, ATTN_W:ATTN_W + 2 * KV_W], kg, ones_bd)
    attn = _attention(q, kt_ctx, kt, v_ctx, v)
    conf = _conformer(glu, ev_conv_w[0], ev_conv_b[0:1], ev_ln_g[0:1], ev_ln_b[0:1])
    h, hm, logits = _outproj0(attn, conf, ev_w_out[0].astype(BF16), x, mods[0], norm2_g[0:1], w_r[0])
    moe0 = _ec_moe(hm, logits, moe_w_gate, moe_w_up, moe_w_down, 0, utri, chunk_lt)

    h, bg, z = _inproj1(h, moe0, mods[0], mods[1], norm1_g[1:2], sc_w_in[0].astype(BF16))
    h, hm, logits = _outproj1(z, bg, sc_conv_w[0], sc_w_out[0].astype(BF16), h, mods[1], norm2_g[1:2], w_r[1])
    moe1 = _ec_moe(hm, logits, moe_w_gate, moe_w_up, moe_w_down, 1, utri, chunk_lt)

    return _final(h, moe1, mods[1], final_g[None, :])
```

```python
import functools

import jax
import jax.numpy as jnp
from jax import lax
from jax.experimental import pallas as pl
from jax.experimental.pallas import tpu as pltpu

F32 = jnp.float32
BF16 = jnp.bfloat16

HEAD_DIM = 64
N_Q_HEADS = 8
N_KV_HEADS = 2
GRID_W = 64
ROPE_THETA = 10000.0
CONF_K = 31
SC_K = 3
N_EXPERTS = 16
EC_CAPACITY = 2
N_MOD = 6
EPS = 1e-6

ATTN_W = N_Q_HEADS * HEAD_DIM
KV_W = N_KV_HEADS * HEAD_DIM
Q_PER_KV = N_Q_HEADS // N_KV_HEADS
QK_SCALE = HEAD_DIM ** -0.5 * 1.4426950408889634

LANES = 128
SUBLANES = 8
MXU_DIM = 256
VMEM_LIMIT_BYTES = 60000 * 1024

CONV_HALO = 16
ROUTER_CHUNK = MXU_DIM
REFINE_STEPS = 24


def _params(*sem):
    return pltpu.CompilerParams(dimension_semantics=sem, vmem_limit_bytes=VMEM_LIMIT_BYTES)


def _norm_mod(x, g, shift, scale):
    ms = jnp.mean(x * x, axis=-1, keepdims=True)
    y = x * lax.rsqrt(ms + EPS) * g
    return y * (1.0 + scale) + shift


def _head_sumsq(x, ones_blockdiag):
    return jnp.dot((x * x).astype(BF16), ones_blockdiag, preferred_element_type=F32)


def _swap_half(x):
    w = x.shape[-1]
    lane = lax.broadcasted_iota(jnp.int32, x.shape, 1)
    first = (lane % HEAD_DIM) < (HEAD_DIM // 2)
    return jnp.where(first, pltpu.roll(x, w - HEAD_DIM // 2, axis=1), pltpu.roll(x, HEAD_DIM // 2, axis=1))


def _silu(x):
    return x * jax.nn.sigmoid(x)


def _store_v_with_ones(v_ref, v):
    ones = jnp.ones((v.shape[0], HEAD_DIM), F32)
    for j in range(N_KV_HEADS):
        v_ref[0, j] = jnp.concatenate([v[:, j * HEAD_DIM:(j + 1) * HEAD_DIM], ones], axis=1).astype(BF16)


def _ada_body(c_ref, w_ref, b_ref, o_ref):
    s = _silu(c_ref[...]).astype(BF16)
    o_ref[0] = jnp.dot(s, w_ref[0].astype(BF16), preferred_element_type=F32) + b_ref[0]


def _ada_mod(cvecs, ada_w, ada_b):
    n_layers, d, n_out = ada_w.shape
    tn = n_out // 4
    return pl.pallas_call(
        _ada_body,
        grid=(n_layers, n_out // tn),
        in_specs=[
            pl.BlockSpec((SUBLANES, d), lambda l, j: (0, 0)),
            pl.BlockSpec((1, d, tn), lambda l, j: (l, 0, j)),
            pl.BlockSpec((1, 1, tn), lambda l, j: (l, 0, j)),
        ],
        out_specs=pl.BlockSpec((1, SUBLANES, tn), lambda l, j: (l, 0, j)),
        out_shape=jax.ShapeDtypeStruct((n_layers, SUBLANES, n_out), F32),
        compiler_params=_params("arbitrary", "arbitrary"),
        name="ada_mod",
    )(cvecs, ada_w, ada_b.reshape(n_layers, 1, n_out))


def _inproj0_body(h_ref, mod_ref, g1_ref, w_ref, qg_ref, kg_ref, cos_ref, sin_ref, ones_ref,
                  q_ref, kt_ref, v_ref, glu_ref):
    hn = _norm_mod(h_ref[0], g1_ref[...], mod_ref[0, 0:1, :], mod_ref[0, 1:2, :])
    proj = jnp.dot(hn.astype(BF16), w_ref[...], preferred_element_type=F32)
    ones = ones_ref[...]
    cos2 = cos_ref[...]
    sin2 = sin_ref[...]

    q = proj[:, :ATTN_W]
    ssq = jnp.concatenate([_head_sumsq(q[:, :MXU_DIM], ones), _head_sumsq(q[:, MXU_DIM:], ones)], axis=1)
    qn = q * lax.rsqrt(ssq * (1.0 / HEAD_DIM) + EPS) * qg_ref[...]
    cos = jnp.concatenate([cos2] * (ATTN_W // LANES), axis=1)
    sin = jnp.concatenate([sin2] * (ATTN_W // LANES), axis=1)
    qr = (qn * cos + _swap_half(qn) * sin) * QK_SCALE
    for h in range(N_Q_HEADS):
        q_ref[0, h] = qr[:, h * HEAD_DIM:(h + 1) * HEAD_DIM].astype(BF16)

    k = proj[:, ATTN_W:ATTN_W + KV_W]
    kn = k * lax.rsqrt(_head_sumsq(k, ones[:KV_W, :KV_W]) * (1.0 / HEAD_DIM) + EPS) * kg_ref[...]
    kr = kn * cos2 + _swap_half(kn) * sin2
    kt_ref[0] = kr.T.astype(BF16)
    _store_v_with_ones(v_ref, proj[:, ATTN_W + KV_W:ATTN_W + 2 * KV_W])

    c0 = ATTN_W + 2 * KV_W
    cc = (proj.shape[1] - c0) // 2
    glu_ref[0] = proj[:, c0:c0 + cc] * jax.nn.sigmoid(proj[:, c0 + cc:])


def _inproj0(h, mods, g1, w_in, qg, kg, cos2, sin2, ones_bd, tm=512):
    b, n, d = h.shape
    n_in = w_in.shape[1]
    conv_ch = (n_in - ATTN_W - 2 * KV_W) // 2
    return pl.pallas_call(
        _inproj0_body,
        grid=(b, n // tm),
        in_specs=[
            pl.BlockSpec((1, tm, d), lambda bi, i: (bi, i, 0)),
            pl.BlockSpec((1, N_MOD, d), lambda bi, i: (bi, 0, 0)),
            pl.BlockSpec((1, d), lambda bi, i: (0, 0)),
            pl.BlockSpec((d, n_in), lambda bi, i: (0, 0)),
            pl.BlockSpec((1, ATTN_W), lambda bi, i: (0, 0)),
            pl.BlockSpec((1, KV_W), lambda bi, i: (0, 0)),
            pl.BlockSpec((tm, LANES), lambda bi, i: (i, 0)),
            pl.BlockSpec((tm, LANES), lambda bi, i: (i, 0)),
            pl.BlockSpec((MXU_DIM, MXU_DIM), lambda bi, i: (0, 0)),
        ],
        out_specs=[
            pl.BlockSpec((1, N_Q_HEADS, tm, HEAD_DIM), lambda bi, i: (bi, 0, i, 0)),
            pl.BlockSpec((1, KV_W, tm), lambda bi, i: (bi, 0, i)),
            pl.BlockSpec((1, N_KV_HEADS, tm, 2 * HEAD_DIM), lambda bi, i: (bi, 0, i, 0)),
            pl.BlockSpec((1, tm, conv_ch), lambda bi, i: (bi, i, 0)),
        ],
        out_shape=[
            jax.ShapeDtypeStruct((b, N_Q_HEADS, n, HEAD_DIM), BF16),
            jax.ShapeDtypeStruct((b, KV_W, n), BF16),
            jax.ShapeDtypeStruct((b, N_KV_HEADS, n, 2 * HEAD_DIM), BF16),
            jax.ShapeDtypeStruct((b, n, conv_ch), F32),
        ],
        compiler_params=_params("arbitrary", "arbitrary"),
        name="inproj0",
    )(h, mods, g1, w_in, qg, kg, cos2, sin2, ones_bd)


def _ctxkv_body(x_ref, mod_ref, g1_ref, w_ref, kg_ref, ones_ref, kt_ref, v_ref):
    hn = _norm_mod(x_ref[0], g1_ref[...], mod_ref[0, 0:1, :], mod_ref[0, 1:2, :])
    proj = jnp.dot(hn.astype(BF16), w_ref[...], preferred_element_type=F32)
    k = proj[:, :KV_W]
    kn = k * lax.rsqrt(_head_sumsq(k, ones_ref[...][:KV_W, :KV_W]) * (1.0 / HEAD_DIM) + EPS) * kg_ref[...]
    kt_ref[0] = kn.T.astype(BF16)
    _store_v_with_ones(v_ref, proj[:, KV_W:])


def _ctxkv(ctx, mods, ctx_row, g1, w_kv, kg, ones_bd):
    b, t, d = ctx.shape
    return pl.pallas_call(
        _ctxkv_body,
        grid=(b,),
        in_specs=[
            pl.BlockSpec((1, t, d), lambda bi: (bi, 0, 0)),
            pl.BlockSpec((1, N_MOD, d), lambda bi: (ctx_row, 0, 0)),
            pl.BlockSpec((1, d), lambda bi: (0, 0)),
            pl.BlockSpec((d, 2 * KV_W), lambda bi: (0, 0)),
            pl.BlockSpec((1, KV_W), lambda bi: (0, 0)),
            pl.BlockSpec((MXU_DIM, MXU_DIM), lambda bi: (0, 0)),
        ],
        out_specs=[
            pl.BlockSpec((1, KV_W, t), lambda bi: (bi, 0, 0)),
            pl.BlockSpec((1, N_KV_HEADS, t, 2 * HEAD_DIM), lambda bi: (bi, 0, 0, 0)),
        ],
        out_shape=[
            jax.ShapeDtypeStruct((b, KV_W, t), BF16),
            jax.ShapeDtypeStruct((b, N_KV_HEADS, t, 2 * HEAD_DIM), BF16),
        ],
        compiler_params=_params("arbitrary"),
        name="ctx_kv",
    )(ctx, mods, g1, w_kv, kg, ones_bd)


ATTN_KEY_CHUNK = 512


def _attn_body(q_ref, ktc_ref, kt_ref, vc_ref, v_ref, o_ref, s0, s1, p0, p1):
    s_bufs = (s0, s1)
    p_bufs = (p0, p1)
    t_ctx = ktc_ref.shape[-1]
    n = kt_ref.shape[-1]
    tq = q_ref.shape[2]

    def scores(h):
        j = h // Q_PER_KV
        q = q_ref[0, h]
        s = s_bufs[h % 2]
        sc = jnp.dot(q, ktc_ref[0, j], preferred_element_type=F32)
        s[:, :t_ctx] = sc
        run = sc[:, :LANES]
        for i in range(1, t_ctx // LANES):
            run = jnp.maximum(run, sc[:, i * LANES:(i + 1) * LANES])
        for c in range(n // ATTN_KEY_CHUNK):
            lo = c * ATTN_KEY_CHUNK
            sc = jnp.dot(q, kt_ref[0, j, :, lo:lo + ATTN_KEY_CHUNK], preferred_element_type=F32)
            s[:, t_ctx + lo:t_ctx + lo + ATTN_KEY_CHUNK] = sc
            for i in range(ATTN_KEY_CHUNK // LANES):
                run = jnp.maximum(run, sc[:, i * LANES:(i + 1) * LANES])
        return jnp.max(run, axis=-1, keepdims=True)

    def exponentials(h, m):
        s = s_bufs[h % 2]
        p = p_bufs[h % 2]
        p[:, :t_ctx] = jnp.exp2(s[:, :t_ctx] - m).astype(BF16)
        for c in range(n // ATTN_KEY_CHUNK):
            lo = t_ctx + c * ATTN_KEY_CHUNK
            p[:, lo:lo + ATTN_KEY_CHUNK] = jnp.exp2(s[:, lo:lo + ATTN_KEY_CHUNK] - m).astype(BF16)

    def weighted_values(h):
        j = h // Q_PER_KV
        p = p_bufs[h % 2]
        ol = (jnp.dot(p[:, :t_ctx], vc_ref[0, j], preferred_element_type=F32)
              + jnp.dot(p[:, t_ctx:], v_ref[0, j], preferred_element_type=F32))
        o = ol / pltpu.roll(ol, HEAD_DIM, axis=1)
        o_ref[0, h] = o[:, :HEAD_DIM].astype(BF16)

    assert t_ctx % LANES == 0 and n % ATTN_KEY_CHUNK == 0
    m = scores(0)
    for h in range(N_Q_HEADS):
        m_next = scores(h + 1) if h + 1 < N_Q_HEADS else None
        if h > 0:
            weighted_values(h - 1)
        exponentials(h, m)
        m = m_next
    weighted_values(N_Q_HEADS - 1)


def _attention(q, kt_ctx, kt, v_ctx, v, tq=256):
    b, hq, n, hd = q.shape
    t_ctx = kt_ctx.shape[-1]
    t_all = t_ctx + n
    return pl.pallas_call(
        _attn_body,
        grid=(b, n // tq),
        scratch_shapes=[pltpu.VMEM((tq, t_all), F32), pltpu.VMEM((tq, t_all), F32),
                        pltpu.VMEM((tq, t_all), BF16), pltpu.VMEM((tq, t_all), BF16)],
        in_specs=[
            pl.BlockSpec((1, hq, tq, hd), lambda bi, i: (bi, 0, i, 0)),
            pl.BlockSpec((1, N_KV_HEADS, hd, t_ctx), lambda bi, i: (bi, 0, 0, 0)),
            pl.BlockSpec((1, N_KV_HEADS, hd, n), lambda bi, i: (bi, 0, 0, 0)),
            pl.BlockSpec((1, N_KV_HEADS, t_ctx, 2 * hd), lambda bi, i: (bi, 0, 0, 0)),
            pl.BlockSpec((1, N_KV_HEADS, n, 2 * hd), lambda bi, i: (bi, 0, 0, 0)),
        ],
        out_specs=pl.BlockSpec((1, hq, tq, hd), lambda bi, i: (bi, 0, i, 0)),
        out_shape=jax.ShapeDtypeStruct((b, hq, n, hd), BF16),
        compiler_params=_params("arbitrary", "arbitrary"),
        name="attention",
    )(q, kt_ctx.reshape(b, N_KV_HEADS, hd, t_ctx), kt.reshape(b, N_KV_HEADS, hd, n), v_ctx, v)


def _conf_body(g_ref, w_ref, cb_ref, lg_ref, lb_ref, o_ref, xpad, *, tn):
    n = g_ref.shape[1]
    ch = g_ref.shape[2]
    xpad[0:CONV_HALO, :] = jnp.zeros((CONV_HALO, ch), F32)
    xpad[n + CONV_HALO:n + 2 * CONV_HALO, :] = jnp.zeros((CONV_HALO, ch), F32)
    xpad[CONV_HALO:n + CONV_HALO, :] = g_ref[0]
    win = tn + 2 * CONV_HALO
    off = CONV_HALO - CONF_K // 2

    def chunk(i, carry):
        r0 = pl.multiple_of(i * tn, tn)
        x = xpad[pl.ds(r0, win), :]
        acc = jnp.zeros((tn, ch), F32)
        for sub in range(SUBLANES):
            xs = x if sub == 0 else pltpu.roll(x, win - sub, axis=0)
            for a in range(win // SUBLANES):
                k = a * SUBLANES + sub - off
                if 0 <= k < CONF_K and a * SUBLANES + tn <= win:
                    acc = acc + xs[a * SUBLANES:a * SUBLANES + tn, :] * w_ref[k:k + 1, :]
        y = acc + cb_ref[...]
        mu = jnp.mean(y, axis=-1, keepdims=True)
        yc = y - mu
        var = jnp.mean(yc * yc, axis=-1, keepdims=True)
        z = yc * lax.rsqrt(var + EPS) * lg_ref[...] + lb_ref[...]
        o_ref[0, pl.ds(r0, tn), :] = _silu(z).astype(BF16)
        return carry

    lax.fori_loop(0, n // tn, chunk, 0)


def _conformer(glu, conv_w, conv_b, ln_g, ln_b, tn=64):
    b, n, ch = glu.shape
    return pl.pallas_call(
        functools.partial(_conf_body, tn=tn),
        grid=(b,),
        in_specs=[
            pl.BlockSpec((1, n, ch), lambda bi: (bi, 0, 0)),
            pl.BlockSpec((CONF_K, ch), lambda bi: (0, 0)),
            pl.BlockSpec((1, ch), lambda bi: (0, 0)),
            pl.BlockSpec((1, ch), lambda bi: (0, 0)),
            pl.BlockSpec((1, ch), lambda bi: (0, 0)),
        ],
        out_specs=pl.BlockSpec((1, n, ch), lambda bi: (bi, 0, 0)),
        out_shape=jax.ShapeDtypeStruct((b, n, ch), BF16),
        scratch_shapes=[pltpu.VMEM((n + 2 * CONV_HALO, ch), F32)],
        compiler_params=_params("arbitrary"),
        name="conformer",
    )(glu, conv_w, conv_b, ln_g, ln_b)


def _residual_router(mix, h_ref, mod_ref, g2_ref, wr_ref, h_out, hm_out, lg_out):
    h1 = h_ref[0] + mod_ref[0, 2:3, :] * mix
    h_out[0] = h1
    hm = _norm_mod(h1, g2_ref[...], mod_ref[0, 3:4, :], mod_ref[0, 4:5, :])
    hm_out[0] = hm
    lg_out[0] = jnp.dot(hm.astype(BF16), wr_ref[...], preferred_element_type=F32)


def _outproj0_body(attn_ref, conf_ref, w_ref, h_ref, mod_ref, g2_ref, wr_ref, h_out, hm_out, lg_out):
    a = jnp.concatenate([attn_ref[0, h] for h in range(N_Q_HEADS)] + [conf_ref[0]], axis=1)
    mix = jnp.dot(a, w_ref[...], preferred_element_type=F32)
    _residual_router(mix, h_ref, mod_ref, g2_ref, wr_ref, h_out, hm_out, lg_out)


def _token_out_specs(b, n, d, tm):
    specs = [
        pl.BlockSpec((1, tm, d), lambda bi, i: (bi, i, 0)),
        pl.BlockSpec((1, tm, d), lambda bi, i: (bi, i, 0)),
        pl.BlockSpec((1, tm, LANES), lambda bi, i: (bi, i, 0)),
    ]
    shapes = [
        jax.ShapeDtypeStruct((b, n, d), F32),
        jax.ShapeDtypeStruct((b, n, d), F32),
        jax.ShapeDtypeStruct((b, n, LANES), F32),
    ]
    return specs, shapes


def _outproj0(attn, conf, w_out, h, mods, g2, w_r, tm=512):
    b, n, d = h.shape
    ch = conf.shape[-1]
    out_specs, out_shape = _token_out_specs(b, n, d, tm)
    return pl.pallas_call(
        _outproj0_body,
        grid=(b, n // tm),
        in_specs=[
            pl.BlockSpec((1, N_Q_HEADS, tm, HEAD_DIM), lambda bi, i: (bi, 0, i, 0)),
            pl.BlockSpec((1, tm, ch), lambda bi, i: (bi, i, 0)),
            pl.BlockSpec((d, d), lambda bi, i: (0, 0)),
            pl.BlockSpec((1, tm, d), lambda bi, i: (bi, i, 0)),
            pl.BlockSpec((1, N_MOD, d), lambda bi, i: (bi, 0, 0)),
            pl.BlockSpec((1, d), lambda bi, i: (0, 0)),
            pl.BlockSpec((d, LANES), lambda bi, i: (0, 0)),
        ],
        out_specs=out_specs,
        out_shape=out_shape,
        compiler_params=_params("arbitrary", "arbitrary"),
        name="outproj0",
    )(attn, conf, w_out, h, mods, g2, w_r)


def _inproj1_body(h_ref, moe_ref, modp_ref, mod_ref, g1_ref, w_ref, h_out, bg_out, z_out):
    d = h_ref.shape[2]
    h = h_ref[0] + modp_ref[0, 5:6, :] * moe_ref[0]
    h_out[0] = h
    hn = _norm_mod(h, g1_ref[...], mod_ref[0, 0:1, :], mod_ref[0, 1:2, :])
    proj = jnp.dot(hn.astype(BF16), w_ref[...], preferred_element_type=F32)
    bg_out[0] = proj[:, :d]
    z_out[0] = proj[:, d:2 * d] * proj[:, 2 * d:]


def _inproj1(h, moe, mods_prev, mods, g1, w_in, tm=512):
    b, n, d = h.shape
    tok = pl.BlockSpec((1, tm, d), lambda bi, i: (bi, i, 0))
    modspec = pl.BlockSpec((1, N_MOD, d), lambda bi, i: (bi, 0, 0))
    return pl.pallas_call(
        _inproj1_body,
        grid=(b, n // tm),
        in_specs=[tok, tok, modspec, modspec,
                  pl.BlockSpec((1, d), lambda bi, i: (0, 0)),
                  pl.BlockSpec((d, 3 * d), lambda bi, i: (0, 0))],
        out_specs=[tok, tok, tok],
        out_shape=[jax.ShapeDtypeStruct((b, n, d), F32)] * 3,
        compiler_params=_params("arbitrary", "arbitrary"),
        name="inproj1",
    )(h, moe, mods_prev, mods, g1, w_in)


def _outproj1_body(z_ref, zp_ref, zn_ref, bg_ref, cw_ref, w_ref, h_ref, mod_ref, g2_ref, wr_ref,
                   h_out, hm_out, lg_out):
    i = pl.program_id(1)
    last = pl.num_programs(1) - 1
    z = z_ref[0]
    tm = z.shape[0]
    row = lax.broadcasted_iota(jnp.int32, z.shape, 0)
    prev_row = jnp.where(i > 0, zp_ref[0, SUBLANES - 1:SUBLANES, :], 0.0)
    next_row = jnp.where(i < last, zn_ref[0, 0:1, :], 0.0)
    z_dn = jnp.where(row == 0, prev_row, pltpu.roll(z, 1, axis=0))
    z_up = jnp.where(row == tm - 1, next_row, pltpu.roll(z, tm - 1, axis=0))
    y = z_dn * cw_ref[0:1, :] + z * cw_ref[1:2, :] + z_up * cw_ref[2:3, :]
    mix = jnp.dot((bg_ref[0] * y).astype(BF16), w_ref[...], preferred_element_type=F32)
    _residual_router(mix, h_ref, mod_ref, g2_ref, wr_ref, h_out, hm_out, lg_out)


def _outproj1(z, bg, conv_w, w_out, h, mods, g2, w_r, tm=512):
    b, n, d = h.shape
    per = tm // SUBLANES
    nblk8 = n // SUBLANES
    tok = pl.BlockSpec((1, tm, d), lambda bi, i: (bi, i, 0))
    out_specs, out_shape = _token_out_specs(b, n, d, tm)
    return pl.pallas_call(
        _outproj1_body,
        grid=(b, n // tm),
        in_specs=[
            tok,
            pl.BlockSpec((1, SUBLANES, d), lambda bi, i: (bi, jnp.maximum(i * per - 1, 0), 0)),
            pl.BlockSpec((1, SUBLANES, d), lambda bi, i: (bi, jnp.minimum((i + 1) * per, nblk8 - 1), 0)),
            tok,
            pl.BlockSpec((SC_K, d), lambda bi, i: (0, 0)),
            pl.BlockSpec((d, d), lambda bi, i: (0, 0)),
            tok,
            pl.BlockSpec((1, N_MOD, d), lambda bi, i: (bi, 0, 0)),
            pl.BlockSpec((1, d), lambda bi, i: (0, 0)),
            pl.BlockSpec((d, LANES), lambda bi, i: (0, 0)),
        ],
        out_specs=out_specs,
        out_shape=out_shape,
        compiler_params=_params("arbitrary", "arbitrary"),
        name="outproj1",
    )(z, z, z, bg, conv_w, w_out, h, mods, g2, w_r)


def _stack_chunks(x):
    n = x.shape[1]
    return jnp.concatenate([x[:, c * ROUTER_CHUNK:(c + 1) * ROUTER_CHUNK] for c in range(n // ROUTER_CHUNK)], axis=0)


def _exclusive_rank(flags, utri, chunk_lt):
    incl = jnp.dot(flags.astype(BF16), utri, preferred_element_type=F32)
    tot = jnp.broadcast_to(incl[:, ROUTER_CHUNK - 1:ROUTER_CHUNK], incl.shape)
    base = jnp.dot(chunk_lt, tot.astype(BF16), preferred_element_type=F32)
    return incl - flags + base


def _router_body(lg_ref, utri_ref, lt_ref, o_ref, slot_scr, v_scr, *, cap):
    n = lg_ref.shape[1]
    n_chunks = n // ROUTER_CHUNK
    lg = lg_ref[0]
    lane = lax.broadcasted_iota(jnp.int32, lg.shape, 1)
    valid = lane < N_EXPERTS
    x = jnp.where(valid, lg, -jnp.inf)
    ex = jnp.where(valid, jnp.exp(x - jnp.max(x, axis=-1, keepdims=True)), 0.0)
    aff = ex / jnp.sum(ex, axis=-1, keepdims=True)

    hi = aff.astype(BF16).astype(F32)
    r1 = aff - hi
    mid = r1.astype(BF16).astype(F32)
    lo = (r1 - mid).astype(BF16).astype(F32)
    tok = lax.broadcasted_iota(jnp.int32, lg.shape, 0)
    vals = (pltpu.roll(hi, N_EXPERTS, axis=1) + pltpu.roll(mid, 2 * N_EXPERTS, axis=1)
            + pltpu.roll(lo, 3 * N_EXPERTS, axis=1))
    vals = jnp.where(lane == 0, (tok // 64).astype(F32), jnp.where(lane == 1, (tok % 64).astype(F32), vals))
    v_scr[...] = vals.astype(BF16)

    aff_t = aff.T[:N_EXPERTS, :]

    def count_ge(t):
        return jnp.sum(jnp.where(aff_t >= t, 1.0, 0.0), axis=-1, keepdims=True)

    def bit_step(i, bits):
        cand = bits | jnp.left_shift(jnp.int32(1), 30 - i)
        return jnp.where(count_ge(lax.bitcast_convert_type(cand, F32)) >= cap, cand, bits)

    bits = lax.fori_loop(0, 31, bit_step, jnp.zeros((N_EXPERTS, 1), jnp.int32))

    def refine(i, lo_hi):
        lo, hi = lo_hi
        mid = (lo + hi) * 0.5
        ok = count_ge(mid) >= cap
        return jnp.where(ok, mid, lo), jnp.where(ok, hi, mid)

    thr, _ = lax.fori_loop(0, REFINE_STEPS, refine,
                           (lax.bitcast_convert_type(bits, F32), lax.bitcast_convert_type(bits + 1, F32)))
    gt = jnp.where(aff_t > thr, 1.0, 0.0)
    eq = jnp.where(aff_t == thr, 1.0, 0.0)
    need = cap - jnp.sum(gt, axis=-1, keepdims=True)
    utri = utri_ref[...]
    chunk_lt = lt_ref[...]
    gt_s = _stack_chunks(gt)
    eq_s = _stack_chunks(eq)
    need_s = jnp.concatenate([need] * n_chunks, axis=0)
    sel = jnp.where((gt_s > 0) | ((eq_s > 0) & (_exclusive_rank(eq_s, utri, chunk_lt) < need_s)), 1.0, 0.0)
    slot = _exclusive_rank(sel, utri, chunk_lt)
    slot_scr[...] = jnp.where(sel > 0, slot, -1.0)

    srow = lax.broadcasted_iota(jnp.int32, (cap, ROUTER_CHUNK), 0).astype(F32)
    olane = lax.broadcasted_iota(jnp.int32, (cap, LANES), 1)

    def one_expert(e, carry):
        res = jnp.zeros((cap, LANES), F32)
        for c in range(n_chunks):
            onehot = jnp.where(srow == slot_scr[pl.ds(c * N_EXPERTS + e, 1), :], 1.0, 0.0).astype(BF16)
            res = res + jnp.dot(onehot, v_scr[c * ROUTER_CHUNK:(c + 1) * ROUTER_CHUNK, :],
                                preferred_element_type=F32)
        idx = res[:, 0:1] * 64.0 + res[:, 1:2]
        is_gate = (olane == N_EXPERTS + e) | (olane == 2 * N_EXPERTS + e) | (olane == 3 * N_EXPERTS + e)
        gate = jnp.sum(jnp.where(is_gate, res, 0.0), axis=-1, keepdims=True)
        o_ref[0, e] = jnp.where(olane == 0, idx, jnp.where(olane == 1, gate, 0.0))
        return carry

    lax.fori_loop(0, N_EXPERTS, one_expert, 0)


def _router(logits, utri, chunk_lt, cap):
    b, n, _ = logits.shape
    rows = (n // ROUTER_CHUNK) * N_EXPERTS
    return pl.pallas_call(
        functools.partial(_router_body, cap=cap),
        grid=(b,),
        in_specs=[
            pl.BlockSpec((1, n, LANES), lambda bi: (bi, 0, 0)),
            pl.BlockSpec((ROUTER_CHUNK, ROUTER_CHUNK), lambda bi: (0, 0)),
            pl.BlockSpec((rows, rows), lambda bi: (0, 0)),
        ],
        out_specs=pl.BlockSpec((1, N_EXPERTS, cap, LANES), lambda bi: (bi, 0, 0, 0)),
        out_shape=jax.ShapeDtypeStruct((b, N_EXPERTS, cap, LANES), F32),
        scratch_shapes=[pltpu.VMEM((rows, ROUTER_CHUNK), F32), pltpu.VMEM((n, LANES), BF16)],
        compiler_params=_params("arbitrary"),
        name="router",
    )(logits, utri, chunk_lt)


GATHER_UNROLL = 8


def _gather_body(idx_ref, hm_ref, o_ref, rows_scr):
    cap = rows_scr.shape[0]

    def group(g, carry):
        base = pl.multiple_of(g * GATHER_UNROLL, GATHER_UNROLL)
        for u in range(GATHER_UNROLL):
            rows_scr[pl.ds(base + u, 1), :] = hm_ref[0, pl.ds(idx_ref[0, 0, base + u], 1), :]
        return carry

    lax.fori_loop(0, cap // GATHER_UNROLL, group, 0)
    o_ref[0, 0] = rows_scr[...].astype(BF16)


def _gather(idx, hm):
    b, n, d = hm.shape
    cap = idx.shape[-1]
    return pl.pallas_call(
        _gather_body,
        grid=(b, N_EXPERTS),
        in_specs=[
            pl.BlockSpec((1, 1, cap), lambda bi, e: (bi * N_EXPERTS + e, 0, 0), memory_space=pltpu.SMEM),
            pl.BlockSpec((1, n, d), lambda bi, e: (bi, 0, 0)),
        ],
        out_specs=pl.BlockSpec((1, 1, cap, d), lambda bi, e: (bi, e, 0, 0)),
        out_shape=jax.ShapeDtypeStruct((b, N_EXPERTS, cap, d), BF16),
        scratch_shapes=[pltpu.VMEM((cap, d), F32)],
        compiler_params=_params("arbitrary", "arbitrary"),
        name="moe_gather",
    )(idx, hm)


def _ffn_body(x_ref, gate_ref, wg_ref, wu_ref, wd_ref, o_ref, wg_s, wu_s, wd_s):
    @pl.when(pl.program_id(1) == 0)
    def _():
        wg_s[...] = wg_ref[0, 0].astype(BF16)
        wu_s[...] = wu_ref[0, 0].astype(BF16)
        wd_s[...] = wd_ref[0, 0].astype(BF16)

    x = x_ref[0, 0]
    hid = _silu(jnp.dot(x, wg_s[...], preferred_element_type=F32)) * jnp.dot(x, wu_s[...], preferred_element_type=F32)
    o_ref[0, 0] = jnp.dot(hid.astype(BF16), wd_s[...], preferred_element_type=F32) * gate_ref[0, 0]


def _expert_ffn(xs, gates, w_gate, w_up, w_down, layer):
    b, n_e, cap, d = xs.shape
    f = w_gate.shape[-1]
    return pl.pallas_call(
        _ffn_body,
        grid=(n_e, b),
        in_specs=[
            pl.BlockSpec((1, 1, cap, d), lambda e, bi: (bi, e, 0, 0)),
            pl.BlockSpec((1, 1, cap, 1), lambda e, bi: (bi, e, 0, 0)),
            pl.BlockSpec((1, 1, d, f), lambda e, bi: (layer, e, 0, 0)),
            pl.BlockSpec((1, 1, d, f), lambda e, bi: (layer, e, 0, 0)),
            pl.BlockSpec((1, 1, f, d), lambda e, bi: (layer, e, 0, 0)),
        ],
        out_specs=pl.BlockSpec((1, 1, cap, d), lambda e, bi: (bi, e, 0, 0)),
        out_shape=jax.ShapeDtypeStruct((b, n_e, cap, d), F32),
        scratch_shapes=[pltpu.VMEM((d, f), BF16), pltpu.VMEM((d, f), BF16), pltpu.VMEM((f, d), BF16)],
        compiler_params=_params("arbitrary", "arbitrary"),
        name="moe_ffn",
    )(xs, gates, w_gate, w_up, w_down)


SCATTER_UNROLL = 4


def _scatter_body(idx_ref, ys_ref, o_ref):
    cap = ys_ref.shape[2]

    @pl.when(pl.program_id(1) == 0)
    def _():
        o_ref[...] = jnp.zeros(o_ref.shape, F32)

    def group(g, carry):
        base = pl.multiple_of(g * SCATTER_UNROLL, SCATTER_UNROLL)
        rows = [idx_ref[0, 0, base + u] for u in range(SCATTER_UNROLL)]
        sums = [o_ref[0, pl.ds(rows[u], 1), :] + ys_ref[0, 0, pl.ds(base + u, 1), :] for u in range(SCATTER_UNROLL)]
        for u in range(SCATTER_UNROLL):
            o_ref[0, pl.ds(rows[u], 1), :] = sums[u]
        return carry

    lax.fori_loop(0, cap // SCATTER_UNROLL, group, 0)


def _scatter(idx, ys, n):
    b, n_e, cap, d = ys.shape
    return pl.pallas_call(
        _scatter_body,
        grid=(b, n_e),
        in_specs=[
            pl.BlockSpec((1, 1, cap), lambda bi, e: (bi * N_EXPERTS + e, 0, 0), memory_space=pltpu.SMEM),
            pl.BlockSpec((1, 1, cap, d), lambda bi, e: (bi, e, 0, 0)),
        ],
        out_specs=pl.BlockSpec((1, n, d), lambda bi, e: (bi, 0, 0)),
        out_shape=jax.ShapeDtypeStruct((b, n, d), F32),
        compiler_params=_params("arbitrary", "arbitrary"),
        name="moe_scatter",
    )(idx, ys)


def _ec_moe(hm, logits, w_gate, w_up, w_down, layer, utri, chunk_lt):
    b, n, d = hm.shape
    cap = max(1, EC_CAPACITY * n // N_EXPERTS)
    routed = _router(logits, utri, chunk_lt, cap)
    idx = routed[..., 0].astype(jnp.int32).reshape(b * N_EXPERTS, 1, cap)
    gates = routed[..., 1:2]
    xs = _gather(idx, hm)
    ys = _expert_ffn(xs, gates, w_gate, w_up, w_down, layer)
    return _scatter(idx, ys, n)


def _final_body(h_ref, moe_ref, mod_ref, g_ref, o_ref):
    h = h_ref[0] + mod_ref[0, 5:6, :] * moe_ref[0]
    o_ref[0] = h * lax.rsqrt(jnp.mean(h * h, axis=-1, keepdims=True) + EPS) * g_ref[...]


def _final(h, moe, mods, g, tm=512):
    b, n, d = h.shape
    tok = pl.BlockSpec((1, tm, d), lambda bi, i: (bi, i, 0))
    return pl.pallas_call(
        _final_body,
        grid=(b, n // tm),
        in_specs=[tok, tok, pl.BlockSpec((1, N_MOD, d), lambda bi, i: (bi, 0, 0)),
                  pl.BlockSpec((1, d), lambda bi, i: (0, 0))],
        out_specs=tok,
        out_shape=jax.ShapeDtypeStruct((b, n, d), F32),
        compiler_params=_params("arbitrary", "arbitrary"),
        name="final_norm",
    )(h, moe, mods, g)


def _rope_tables(n):
    rows = n // GRID_W
    row = jnp.repeat(jnp.arange(rows, dtype=F32), GRID_W)
    col = jnp.tile(jnp.arange(GRID_W, dtype=F32), rows)
    axis_dim = HEAD_DIM // 2
    inv = ROPE_THETA ** (-jnp.arange(0, axis_dim, 2, dtype=F32) / axis_dim)
    ang = jnp.concatenate([row[:, None] * inv, col[:, None] * inv], axis=-1)
    cos = jnp.cos(ang)
    sin = jnp.sin(ang)
    cos64 = jnp.concatenate([cos, cos], axis=-1)
    sin64 = jnp.concatenate([-sin, sin], axis=-1)
    reps = LANES // HEAD_DIM
    return jnp.tile(cos64, (1, reps)), jnp.tile(sin64, (1, reps))


def _router_tables(n):
    r = jnp.arange(ROUTER_CHUNK)
    utri = (r[:, None] <= r[None, :]).astype(BF16)
    rows = jnp.arange((n // ROUTER_CHUNK) * N_EXPERTS)
    same_e = (rows[:, None] % N_EXPERTS) == (rows[None, :] % N_EXPERTS)
    earlier = (rows[None, :] // N_EXPERTS) < (rows[:, None] // N_EXPERTS)
    return utri, (same_e & earlier).astype(BF16)


def _head_ones():
    r = jnp.arange(MXU_DIM) // HEAD_DIM
    return (r[:, None] == r[None, :]).astype(BF16)


def kernel(x, c, ctx, c_ctx, ada_w, ada_b, norm1_g, norm2_g, ev_w_in, ev_q_g, ev_k_g, ev_conv_w, ev_conv_b,
           ev_ln_g, ev_ln_b, ev_w_out, sc_w_in, sc_conv_w, sc_w_out, moe_w_r, moe_w_gate, moe_w_up,
           moe_w_down, final_g):
    b, n, d = x.shape
    depth = ada_w.shape[0]
    assert depth == 2 and b < SUBLANES and n % ROUTER_CHUNK == 0

    cos2, sin2 = _rope_tables(n)
    utri, chunk_lt = _router_tables(n)
    ones_bd = _head_ones()

    cvecs = jnp.zeros((SUBLANES, d), F32).at[:b].set(c).at[b].set(c_ctx)
    mods = _ada_mod(cvecs, ada_w, ada_b).reshape(depth, SUBLANES, N_MOD, d)
    w_r = jnp.pad(moe_w_r, ((0, 0), (0, 0), (0, LANES - N_EXPERTS))).astype(BF16)

    w_in0 = ev_w_in[0].astype(BF16)
    qg = jnp.tile(ev_q_g[0], N_Q_HEADS)[None, :]
    kg = jnp.tile(ev_k_g[0], N_KV_HEADS)[None, :]
    q, kt, v, glu = _inproj0(x, mods[0], norm1_g[0:1], w_in0, qg, kg, cos2, sin2, ones_bd)
    kt_ctx, v_ctx = _ctxkv(ctx, mods[0], b, norm1_g[0:1], w_in0[:, ATTN_W:ATTN_W + 2 * KV_W], kg, ones_bd)
    attn = _attention(q, kt_ctx, kt, v_ctx, v)
    conf = _conformer(glu, ev_conv_w[0], ev_conv_b[0:1], ev_ln_g[0:1], ev_ln_b[0:1])
    h, hm, logits = _outproj0(attn, conf, ev_w_out[0].astype(BF16), x, mods[0], norm2_g[0:1], w_r[0])
    moe0 = _ec_moe(hm, logits, moe_w_gate, moe_w_up, moe_w_down, 0, utri, chunk_lt)

    h, bg, z = _inproj1(h, moe0, mods[0], mods[1], norm1_g[1:2], sc_w_in[0].astype(BF16))
    h, hm, logits = _outproj1(z, bg, sc_conv_w[0], sc_w_out[0].astype(BF16), h, mods[1], norm2_g[1:2], w_r[1])
    moe1 = _ec_moe(hm, logits, moe_w_gate, moe_w_up, moe_w_down, 1, utri, chunk_lt)

    return _final(h, moe1, mods[1], final_g[None, :])
```

```python
import functools

import jax
import jax.numpy as jnp
from jax import lax
from jax.experimental import pallas as pl
from jax.experimental.pallas import tpu as pltpu

F32 = jnp.float32
BF16 = jnp.bfloat16

HEAD_DIM = 64
N_Q_HEADS = 8
N_KV_HEADS = 2
GRID_W = 64
ROPE_THETA = 10000.0
CONF_K = 31
SC_K = 3
N_EXPERTS = 16
EC_CAPACITY = 2
N_MOD = 6
EPS = 1e-6

ATTN_W = N_Q_HEADS * HEAD_DIM
KV_W = N_KV_HEADS * HEAD_DIM
Q_PER_KV = N_Q_HEADS // N_KV_HEADS
QK_SCALE = HEAD_DIM ** -0.5 * 1.4426950408889634

LANES = 128
SUBLANES = 8
MXU_DIM = 256
VMEM_LIMIT_BYTES = 60000 * 1024

CONV_HALO = 16
ROUTER_CHUNK = LANES
REFINE_STEPS = 24


def _params(*sem):
    return pltpu.CompilerParams(dimension_semantics=sem, vmem_limit_bytes=VMEM_LIMIT_BYTES)


def _norm_mod(x, g, shift, scale):
    ms = jnp.mean(x * x, axis=-1, keepdims=True)
    y = x * lax.rsqrt(ms + EPS) * g
    return y * (1.0 + scale) + shift


def _head_sumsq(x, ones_blockdiag):
    return jnp.dot((x * x).astype(BF16), ones_blockdiag, preferred_element_type=F32)


def _swap_half(x):
    w = x.shape[-1]
    lane = lax.broadcasted_iota(jnp.int32, x.shape, 1)
    first = (lane % HEAD_DIM) < (HEAD_DIM // 2)
    return jnp.where(first, pltpu.roll(x, w - HEAD_DIM // 2, axis=1), pltpu.roll(x, HEAD_DIM // 2, axis=1))


def _silu(x):
    return x * jax.nn.sigmoid(x)


def _tile_transpose8(vs):
    sub = lax.broadcasted_iota(jnp.int32, vs[0].shape, 1)
    for d in (4, 2, 1):
        keep = (sub & d) == 0
        out = list(vs)
        for i in range(SUBLANES):
            if i & d == 0:
                a, b = vs[i], vs[i + d]
                out[i] = jnp.where(keep, a, pltpu.roll(b, d, axis=1))
                out[i + d] = jnp.where(keep, pltpu.roll(a, SUBLANES - d, axis=1), b)
        vs = out
    return vs


def _rows_to_tiles(x):
    r = x.shape[0]
    vs = [x[:, a * LANES:(a + 1) * LANES].reshape(r // SUBLANES, SUBLANES, LANES) for a in range(SUBLANES)]
    return jnp.stack(_tile_transpose8(vs), axis=1).reshape(r, SUBLANES, LANES)


def _tiles_to_rows(x3):
    r = x3.shape[0]
    x4 = x3.reshape(r // SUBLANES, SUBLANES, SUBLANES, LANES)
    vs = _tile_transpose8([x4[:, j] for j in range(SUBLANES)])
    return jnp.concatenate([v.reshape(r, LANES) for v in vs], axis=1)


def _store_v_with_ones(v_ref, v):
    ones = jnp.ones((v.shape[0], HEAD_DIM), F32)
    for j in range(N_KV_HEADS):
        v_ref[0, j] = jnp.concatenate([v[:, j * HEAD_DIM:(j + 1) * HEAD_DIM], ones], axis=1).astype(BF16)


def _ada_body(c_ref, w_ref, b_ref, o_ref):
    s = _silu(c_ref[...]).astype(BF16)
    o_ref[0] = jnp.dot(s, w_ref[0].astype(BF16), preferred_element_type=F32) + b_ref[0]


def _ada_mod(cvecs, ada_w, ada_b):
    n_layers, d, n_out = ada_w.shape
    tn = n_out // 4
    return pl.pallas_call(
        _ada_body,
        grid=(n_layers, n_out // tn),
        in_specs=[
            pl.BlockSpec((SUBLANES, d), lambda l, j: (0, 0)),
            pl.BlockSpec((1, d, tn), lambda l, j: (l, 0, j)),
            pl.BlockSpec((1, 1, tn), lambda l, j: (l, 0, j)),
        ],
        out_specs=pl.BlockSpec((1, SUBLANES, tn), lambda l, j: (l, 0, j)),
        out_shape=jax.ShapeDtypeStruct((n_layers, SUBLANES, n_out), F32),
        compiler_params=_params("arbitrary", "arbitrary"),
        name="ada_mod",
    )(cvecs, ada_w, ada_b.reshape(n_layers, 1, n_out))


def _inproj0_body(h_ref, mod_ref, g1_ref, w_ref, qg_ref, kg_ref, cos_ref, sin_ref, ones_ref,
                  q_ref, kt_ref, v_ref, glu_ref):
    hn = _norm_mod(h_ref[0], g1_ref[...], mod_ref[0, 0:1, :], mod_ref[0, 1:2, :])
    proj = jnp.dot(hn.astype(BF16), w_ref[...], preferred_element_type=F32)
    ones = ones_ref[...]
    cos2 = cos_ref[...]
    sin2 = sin_ref[...]

    q = proj[:, :ATTN_W]
    ssq = jnp.concatenate([_head_sumsq(q[:, :MXU_DIM], ones), _head_sumsq(q[:, MXU_DIM:], ones)], axis=1)
    qn = q * lax.rsqrt(ssq * (1.0 / HEAD_DIM) + EPS) * qg_ref[...]
    cos = jnp.concatenate([cos2] * (ATTN_W // LANES), axis=1)
    sin = jnp.concatenate([sin2] * (ATTN_W // LANES), axis=1)
    qr = (qn * cos + _swap_half(qn) * sin) * QK_SCALE
    for h in range(N_Q_HEADS):
        q_ref[0, h] = qr[:, h * HEAD_DIM:(h + 1) * HEAD_DIM].astype(BF16)

    k = proj[:, ATTN_W:ATTN_W + KV_W]
    kn = k * lax.rsqrt(_head_sumsq(k, ones[:KV_W, :KV_W]) * (1.0 / HEAD_DIM) + EPS) * kg_ref[...]
    kr = kn * cos2 + _swap_half(kn) * sin2
    kt_ref[0] = kr.T.astype(BF16)
    _store_v_with_ones(v_ref, proj[:, ATTN_W + KV_W:ATTN_W + 2 * KV_W])

    c0 = ATTN_W + 2 * KV_W
    cc = (proj.shape[1] - c0) // 2
    glu_ref[0] = proj[:, c0:c0 + cc] * jax.nn.sigmoid(proj[:, c0 + cc:])


def _inproj0(h, mods, g1, w_in, qg, kg, cos2, sin2, ones_bd, tm=512):
    b, n, d = h.shape
    n_in = w_in.shape[1]
    conv_ch = (n_in - ATTN_W - 2 * KV_W) // 2
    return pl.pallas_call(
        _inproj0_body,
        grid=(b, n // tm),
        in_specs=[
            pl.BlockSpec((1, tm, d), lambda bi, i: (bi, i, 0)),
            pl.BlockSpec((1, N_MOD, d), lambda bi, i: (bi, 0, 0)),
            pl.BlockSpec((1, d), lambda bi, i: (0, 0)),
            pl.BlockSpec((d, n_in), lambda bi, i: (0, 0)),
            pl.BlockSpec((1, ATTN_W), lambda bi, i: (0, 0)),
            pl.BlockSpec((1, KV_W), lambda bi, i: (0, 0)),
            pl.BlockSpec((tm, LANES), lambda bi, i: (i, 0)),
            pl.BlockSpec((tm, LANES), lambda bi, i: (i, 0)),
            pl.BlockSpec((MXU_DIM, MXU_DIM), lambda bi, i: (0, 0)),
        ],
        out_specs=[
            pl.BlockSpec((1, N_Q_HEADS, tm, HEAD_DIM), lambda bi, i: (bi, 0, i, 0)),
            pl.BlockSpec((1, KV_W, tm), lambda bi, i: (bi, 0, i)),
            pl.BlockSpec((1, N_KV_HEADS, tm, 2 * HEAD_DIM), lambda bi, i: (bi, 0, i, 0)),
            pl.BlockSpec((1, tm, conv_ch), lambda bi, i: (bi, i, 0)),
        ],
        out_shape=[
            jax.ShapeDtypeStruct((b, N_Q_HEADS, n, HEAD_DIM), BF16),
            jax.ShapeDtypeStruct((b, KV_W, n), BF16),
            jax.ShapeDtypeStruct((b, N_KV_HEADS, n, 2 * HEAD_DIM), BF16),
            jax.ShapeDtypeStruct((b, n, conv_ch), F32),
        ],
        compiler_params=_params("arbitrary", "arbitrary"),
        name="inproj0",
    )(h, mods, g1, w_in, qg, kg, cos2, sin2, ones_bd)


def _ctxkv_body(x_ref, mod_ref, g1_ref, w_ref, kg_ref, ones_ref, kt_ref, v_ref):
    hn = _norm_mod(x_ref[0], g1_ref[...], mod_ref[0, 0:1, :], mod_ref[0, 1:2, :])
    proj = jnp.dot(hn.astype(BF16), w_ref[...], preferred_element_type=F32)
    k = proj[:, :KV_W]
    kn = k * lax.rsqrt(_head_sumsq(k, ones_ref[...][:KV_W, :KV_W]) * (1.0 / HEAD_DIM) + EPS) * kg_ref[...]
    kt_ref[0] = kn.T.astype(BF16)
    _store_v_with_ones(v_ref, proj[:, KV_W:])


def _ctxkv(ctx, mods, ctx_row, g1, w_kv, kg, ones_bd):
    b, t, d = ctx.shape
    return pl.pallas_call(
        _ctxkv_body,
        grid=(b,),
        in_specs=[
            pl.BlockSpec((1, t, d), lambda bi: (bi, 0, 0)),
            pl.BlockSpec((1, N_MOD, d), lambda bi: (ctx_row, 0, 0)),
            pl.BlockSpec((1, d), lambda bi: (0, 0)),
            pl.BlockSpec((d, 2 * KV_W), lambda bi: (0, 0)),
            pl.BlockSpec((1, KV_W), lambda bi: (0, 0)),
            pl.BlockSpec((MXU_DIM, MXU_DIM), lambda bi: (0, 0)),
        ],
        out_specs=[
            pl.BlockSpec((1, KV_W, t), lambda bi: (bi, 0, 0)),
            pl.BlockSpec((1, N_KV_HEADS, t, 2 * HEAD_DIM), lambda bi: (bi, 0, 0, 0)),
        ],
        out_shape=[
            jax.ShapeDtypeStruct((b, KV_W, t), BF16),
            jax.ShapeDtypeStruct((b, N_KV_HEADS, t, 2 * HEAD_DIM), BF16),
        ],
        compiler_params=_params("arbitrary"),
        name="ctx_kv",
    )(ctx, mods, g1, w_kv, kg, ones_bd)


ATTN_KEY_CHUNK = 512


def _attn_body(q_ref, ktc_ref, kt_ref, vc_ref, v_ref, o_ref, s0, s1, p0, p1):
    s_bufs = (s0, s1)
    p_bufs = (p0, p1)
    t_ctx = ktc_ref.shape[-1]
    n = kt_ref.shape[-1]
    tq = q_ref.shape[2]

    def scores(h):
        j = h // Q_PER_KV
        q = q_ref[0, h]
        s = s_bufs[h % 2]
        sc = jnp.dot(q, ktc_ref[0, j], preferred_element_type=F32)
        s[:, :t_ctx] = sc
        run = sc[:, :LANES]
        for i in range(1, t_ctx // LANES):
            run = jnp.maximum(run, sc[:, i * LANES:(i + 1) * LANES])
        for c in range(n // ATTN_KEY_CHUNK):
            lo = c * ATTN_KEY_CHUNK
            sc = jnp.dot(q, kt_ref[0, j, :, lo:lo + ATTN_KEY_CHUNK], preferred_element_type=F32)
            s[:, t_ctx + lo:t_ctx + lo + ATTN_KEY_CHUNK] = sc
            for i in range(ATTN_KEY_CHUNK // LANES):
                run = jnp.maximum(run, sc[:, i * LANES:(i + 1) * LANES])
        return jnp.max(run, axis=-1, keepdims=True)

    def exponentials(h, m):
        s = s_bufs[h % 2]
        p = p_bufs[h % 2]
        p[:, :t_ctx] = jnp.exp2(s[:, :t_ctx] - m).astype(BF16)
        for c in range(n // ATTN_KEY_CHUNK):
            lo = t_ctx + c * ATTN_KEY_CHUNK
            p[:, lo:lo + ATTN_KEY_CHUNK] = jnp.exp2(s[:, lo:lo + ATTN_KEY_CHUNK] - m).astype(BF16)

    def weighted_values(h):
        j = h // Q_PER_KV
        p = p_bufs[h % 2]
        ol = (jnp.dot(p[:, :t_ctx], vc_ref[0, j], preferred_element_type=F32)
              + jnp.dot(p[:, t_ctx:], v_ref[0, j], preferred_element_type=F32))
        o = ol / pltpu.roll(ol, HEAD_DIM, axis=1)
        o_ref[0, h] = o[:, :HEAD_DIM].astype(BF16)

    assert t_ctx % LANES == 0 and n % ATTN_KEY_CHUNK == 0
    m = scores(0)
    for h in range(N_Q_HEADS):
        m_next = scores(h + 1) if h + 1 < N_Q_HEADS else None
        if h > 0:
            weighted_values(h - 1)
        exponentials(h, m)
        m = m_next
    weighted_values(N_Q_HEADS - 1)


def _attention(q, kt_ctx, kt, v_ctx, v, tq=256):
    b, hq, n, hd = q.shape
    t_ctx = kt_ctx.shape[-1]
    t_all = t_ctx + n
    return pl.pallas_call(
        _attn_body,
        grid=(b, n // tq),
        scratch_shapes=[pltpu.VMEM((tq, t_all), F32), pltpu.VMEM((tq, t_all), F32),
                        pltpu.VMEM((tq, t_all), BF16), pltpu.VMEM((tq, t_all), BF16)],
        in_specs=[
            pl.BlockSpec((1, hq, tq, hd), lambda bi, i: (bi, 0, i, 0)),
            pl.BlockSpec((1, N_KV_HEADS, hd, t_ctx), lambda bi, i: (bi, 0, 0, 0)),
            pl.BlockSpec((1, N_KV_HEADS, hd, n), lambda bi, i: (bi, 0, 0, 0)),
            pl.BlockSpec((1, N_KV_HEADS, t_ctx, 2 * hd), lambda bi, i: (bi, 0, 0, 0)),
            pl.BlockSpec((1, N_KV_HEADS, n, 2 * hd), lambda bi, i: (bi, 0, 0, 0)),
        ],
        out_specs=pl.BlockSpec((1, hq, tq, hd), lambda bi, i: (bi, 0, i, 0)),
        out_shape=jax.ShapeDtypeStruct((b, hq, n, hd), BF16),
        compiler_params=_params("arbitrary", "arbitrary"),
        name="attention",
    )(q, kt_ctx.reshape(b, N_KV_HEADS, hd, t_ctx), kt.reshape(b, N_KV_HEADS, hd, n), v_ctx, v)


def _conf_body(g_ref, w_ref, cb_ref, lg_ref, lb_ref, o_ref, xpad, *, tn):
    n = g_ref.shape[1]
    ch = g_ref.shape[2]
    xpad[0:CONV_HALO, :] = jnp.zeros((CONV_HALO, ch), F32)
    xpad[n + CONV_HALO:n + 2 * CONV_HALO, :] = jnp.zeros((CONV_HALO, ch), F32)
    xpad[CONV_HALO:n + CONV_HALO, :] = g_ref[0]
    win = tn + 2 * CONV_HALO
    off = CONV_HALO - CONF_K // 2

    def chunk(i, carry):
        r0 = pl.multiple_of(i * tn, tn)
        x = xpad[pl.ds(r0, win), :]
        acc = jnp.zeros((tn, ch), F32)
        for sub in range(SUBLANES):
            xs = x if sub == 0 else pltpu.roll(x, win - sub, axis=0)
            for a in range(win // SUBLANES):
                k = a * SUBLANES + sub - off
                if 0 <= k < CONF_K and a * SUBLANES + tn <= win:
                    acc = acc + xs[a * SUBLANES:a * SUBLANES + tn, :] * w_ref[k:k + 1, :]
        y = acc + cb_ref[...]
        mu = jnp.mean(y, axis=-1, keepdims=True)
        yc = y - mu
        var = jnp.mean(yc * yc, axis=-1, keepdims=True)
        z = yc * lax.rsqrt(var + EPS) * lg_ref[...] + lb_ref[...]
        o_ref[0, pl.ds(r0, tn), :] = _silu(z).astype(BF16)
        return carry

    lax.fori_loop(0, n // tn, chunk, 0)


def _conformer(glu, conv_w, conv_b, ln_g, ln_b, tn=64):
    b, n, ch = glu.shape
    return pl.pallas_call(
        functools.partial(_conf_body, tn=tn),
        grid=(b,),
        in_specs=[
            pl.BlockSpec((1, n, ch), lambda bi: (bi, 0, 0)),
            pl.BlockSpec((CONF_K, ch), lambda bi: (0, 0)),
            pl.BlockSpec((1, ch), lambda bi: (0, 0)),
            pl.BlockSpec((1, ch), lambda bi: (0, 0)),
            pl.BlockSpec((1, ch), lambda bi: (0, 0)),
        ],
        out_specs=pl.BlockSpec((1, n, ch), lambda bi: (bi, 0, 0)),
        out_shape=jax.ShapeDtypeStruct((b, n, ch), BF16),
        scratch_shapes=[pltpu.VMEM((n + 2 * CONV_HALO, ch), F32)],
        compiler_params=_params("arbitrary"),
        name="conformer",
    )(glu, conv_w, conv_b, ln_g, ln_b)


def _residual_router(mix, h_ref, mod_ref, g2_ref, wr_ref, h_out, hm_out, lg_out):
    h1 = h_ref[0] + mod_ref[0, 2:3, :] * mix
    h_out[0] = h1
    hm = _norm_mod(h1, g2_ref[...], mod_ref[0, 3:4, :], mod_ref[0, 4:5, :])
    hm_out[0] = _rows_to_tiles(hm)
    lg_out[0] = jnp.dot(hm.astype(BF16), wr_ref[...], preferred_element_type=F32)


def _outproj0_body(attn_ref, conf_ref, w_ref, h_ref, mod_ref, g2_ref, wr_ref, h_out, hm_out, lg_out):
    a = jnp.concatenate([attn_ref[0, h] for h in range(N_Q_HEADS)] + [conf_ref[0]], axis=1)
    mix = jnp.dot(a, w_ref[...], preferred_element_type=F32)
    _residual_router(mix, h_ref, mod_ref, g2_ref, wr_ref, h_out, hm_out, lg_out)


def _tile_spec(tm, d):
    assert d == SUBLANES * LANES
    return pl.BlockSpec((1, tm, SUBLANES, LANES), lambda bi, i: (bi, i, 0, 0))


def _token_out_specs(b, n, d, tm):
    specs = [
        pl.BlockSpec((1, tm, d), lambda bi, i: (bi, i, 0)),
        _tile_spec(tm, d),
        pl.BlockSpec((1, tm, LANES), lambda bi, i: (bi, i, 0)),
    ]
    shapes = [
        jax.ShapeDtypeStruct((b, n, d), F32),
        jax.ShapeDtypeStruct((b, n, SUBLANES, d // SUBLANES), F32),
        jax.ShapeDtypeStruct((b, n, LANES), F32),
    ]
    return specs, shapes


def _outproj0(attn, conf, w_out, h, mods, g2, w_r, tm=512):
    b, n, d = h.shape
    ch = conf.shape[-1]
    out_specs, out_shape = _token_out_specs(b, n, d, tm)
    return pl.pallas_call(
        _outproj0_body,
        grid=(b, n // tm),
        in_specs=[
            pl.BlockSpec((1, N_Q_HEADS, tm, HEAD_DIM), lambda bi, i: (bi, 0, i, 0)),
            pl.BlockSpec((1, tm, ch), lambda bi, i: (bi, i, 0)),
            pl.BlockSpec((d, d), lambda bi, i: (0, 0)),
            pl.BlockSpec((1, tm, d), lambda bi, i: (bi, i, 0)),
            pl.BlockSpec((1, N_MOD, d), lambda bi, i: (bi, 0, 0)),
            pl.BlockSpec((1, d), lambda bi, i: (0, 0)),
            pl.BlockSpec((d, LANES), lambda bi, i: (0, 0)),
        ],
        out_specs=out_specs,
        out_shape=out_shape,
        compiler_params=_params("arbitrary", "arbitrary"),
        name="outproj0",
    )(attn, conf, w_out, h, mods, g2, w_r)


def _inproj1_body(h_ref, moe_ref, modp_ref, mod_ref, g1_ref, w_ref, h_out, bg_out, z_out):
    d = h_ref.shape[2]
    h = h_ref[0] + modp_ref[0, 5:6, :] * _tiles_to_rows(moe_ref[0])
    h_out[0] = h
    hn = _norm_mod(h, g1_ref[...], mod_ref[0, 0:1, :], mod_ref[0, 1:2, :])
    proj = jnp.dot(hn.astype(BF16), w_ref[...], preferred_element_type=F32)
    bg_out[0] = proj[:, :d]
    z_out[0] = proj[:, d:2 * d] * proj[:, 2 * d:]


def _inproj1(h, moe, mods_prev, mods, g1, w_in, tm=512):
    b, n, d = h.shape
    tok = pl.BlockSpec((1, tm, d), lambda bi, i: (bi, i, 0))
    modspec = pl.BlockSpec((1, N_MOD, d), lambda bi, i: (bi, 0, 0))
    return pl.pallas_call(
        _inproj1_body,
        grid=(b, n // tm),
        in_specs=[tok, _tile_spec(tm, d), modspec, modspec,
                  pl.BlockSpec((1, d), lambda bi, i: (0, 0)),
                  pl.BlockSpec((d, 3 * d), lambda bi, i: (0, 0))],
        out_specs=[tok, tok, tok],
        out_shape=[jax.ShapeDtypeStruct((b, n, d), F32)] * 3,
        compiler_params=_params("arbitrary", "arbitrary"),
        name="inproj1",
    )(h, moe, mods_prev, mods, g1, w_in)


def _outproj1_body(z_ref, zp_ref, zn_ref, bg_ref, cw_ref, w_ref, h_ref, mod_ref, g2_ref, wr_ref,
                   h_out, hm_out, lg_out):
    i = pl.program_id(1)
    last = pl.num_programs(1) - 1
    z = z_ref[0]
    tm = z.shape[0]
    row = lax.broadcasted_iota(jnp.int32, z.shape, 0)
    prev_row = jnp.where(i > 0, zp_ref[0, SUBLANES - 1:SUBLANES, :], 0.0)
    next_row = jnp.where(i < last, zn_ref[0, 0:1, :], 0.0)
    z_dn = jnp.where(row == 0, prev_row, pltpu.roll(z, 1, axis=0))
    z_up = jnp.where(row == tm - 1, next_row, pltpu.roll(z, tm - 1, axis=0))
    y = z_dn * cw_ref[0:1, :] + z * cw_ref[1:2, :] + z_up * cw_ref[2:3, :]
    mix = jnp.dot((bg_ref[0] * y).astype(BF16), w_ref[...], preferred_element_type=F32)
    _residual_router(mix, h_ref, mod_ref, g2_ref, wr_ref, h_out, hm_out, lg_out)


def _outproj1(z, bg, conv_w, w_out, h, mods, g2, w_r, tm=512):
    b, n, d = h.shape
    per = tm // SUBLANES
    nblk8 = n // SUBLANES
    tok = pl.BlockSpec((1, tm, d), lambda bi, i: (bi, i, 0))
    out_specs, out_shape = _token_out_specs(b, n, d, tm)
    return pl.pallas_call(
        _outproj1_body,
        grid=(b, n // tm),
        in_specs=[
            tok,
            pl.BlockSpec((1, SUBLANES, d), lambda bi, i: (bi, jnp.maximum(i * per - 1, 0), 0)),
            pl.BlockSpec((1, SUBLANES, d), lambda bi, i: (bi, jnp.minimum((i + 1) * per, nblk8 - 1), 0)),
            tok,
            pl.BlockSpec((SC_K, d), lambda bi, i: (0, 0)),
            pl.BlockSpec((d, d), lambda bi, i: (0, 0)),
            tok,
            pl.BlockSpec((1, N_MOD, d), lambda bi, i: (bi, 0, 0)),
            pl.BlockSpec((1, d), lambda bi, i: (0, 0)),
            pl.BlockSpec((d, LANES), lambda bi, i: (0, 0)),
        ],
        out_specs=out_specs,
        out_shape=out_shape,
        compiler_params=_params("arbitrary", "arbitrary"),
        name="outproj1",
    )(z, z, z, bg, conv_w, w_out, h, mods, g2, w_r)


def _stack_chunks(x):
    n = x.shape[1]
    return jnp.concatenate([x[:, c * ROUTER_CHUNK:(c + 1) * ROUTER_CHUNK] for c in range(n // ROUTER_CHUNK)], axis=0)


def _exclusive_rank(flags, utri, chunk_lt):
    incl = jnp.dot(flags.astype(BF16), utri, preferred_element_type=F32)
    tot = jnp.broadcast_to(incl[:, ROUTER_CHUNK - 1:ROUTER_CHUNK], incl.shape)
    base = jnp.dot(chunk_lt, tot.astype(BF16), preferred_element_type=F32)
    return incl - flags + base


def _router_body(lg_ref, utri_ref, lt_ref, blk_ref, idx_ref, aff_ref, loc_scr, tot_scr, end_scr, *, cap):
    n = lg_ref.shape[1]
    n_chunks = n // ROUTER_CHUNK
    lg = lg_ref[0]
    lane = lax.broadcasted_iota(jnp.int32, lg.shape, 1)
    valid = lane < N_EXPERTS
    x = jnp.where(valid, lg, -jnp.inf)
    ex = jnp.where(valid, jnp.exp(x - jnp.max(x, axis=-1, keepdims=True)), 0.0)
    aff = ex / jnp.sum(ex, axis=-1, keepdims=True)

    aff_t = aff.T[:N_EXPERTS, :]
    aff_ref[0] = aff_t

    def count_ge(t):
        return jnp.sum(jnp.where(aff_t >= t, 1.0, 0.0), axis=-1, keepdims=True)

    def bit_step(i, bits):
        cand = bits | jnp.left_shift(jnp.int32(1), 30 - i)
        return jnp.where(count_ge(lax.bitcast_convert_type(cand, F32)) >= cap, cand, bits)

    bits = lax.fori_loop(0, 31, bit_step, jnp.zeros((N_EXPERTS, 1), jnp.int32))

    def refine(i, lo_hi):
        lo, hi = lo_hi
        mid = (lo + hi) * 0.5
        ok = count_ge(mid) >= cap
        return jnp.where(ok, mid, lo), jnp.where(ok, hi, mid)

    thr, _ = lax.fori_loop(0, REFINE_STEPS, refine,
                           (lax.bitcast_convert_type(bits, F32), lax.bitcast_convert_type(bits + 1, F32)))
    gt = jnp.where(aff_t > thr, 1.0, 0.0)
    eq = jnp.where(aff_t == thr, 1.0, 0.0)
    need = cap - jnp.sum(gt, axis=-1, keepdims=True)
    utri = utri_ref[...]
    chunk_lt = lt_ref[...]
    gt_s = _stack_chunks(gt)
    eq_s = _stack_chunks(eq)
    need_s = jnp.concatenate([need] * n_chunks, axis=0)
    sel = jnp.where((gt_s > 0) | ((eq_s > 0) & (_exclusive_rank(eq_s, utri, chunk_lt) < need_s)), 1.0, 0.0)

    loc_scr[...] = jnp.dot(sel.astype(BF16), utri, preferred_element_type=F32).astype(BF16)
    sel_t = jnp.concatenate([sel[c * N_EXPERTS:(c + 1) * N_EXPERTS, :] for c in range(n_chunks)], axis=1)
    tot = jnp.dot(sel_t.astype(BF16), blk_ref[...], preferred_element_type=F32)
    tot_scr[...] = tot
    end_scr[...] = jnp.dot(tot.astype(BF16), utri, preferred_element_type=F32)

    slot = lax.broadcasted_iota(jnp.int32, (cap, 1), 0).astype(F32)
    row_id = lax.broadcasted_iota(jnp.int32, (cap, n_chunks * N_EXPERTS), 1)

    def one_expert(e, carry):
        ends = end_scr[pl.ds(e, 1), :]
        before = ends <= slot
        chunk = jnp.sum(jnp.where(before, 1.0, 0.0), axis=-1, keepdims=True)
        base = jnp.sum(jnp.where(before, tot_scr[pl.ds(e, 1), :], 0.0), axis=-1, keepdims=True)
        pick = jnp.where(row_id == chunk.astype(jnp.int32) * N_EXPERTS + e, 1.0, 0.0).astype(BF16)
        counts = jnp.dot(pick, loc_scr[...], preferred_element_type=F32)
        inside = jnp.sum(jnp.where(counts <= slot - base, 1.0, 0.0), axis=-1, keepdims=True)
        idx_ref[0, e] = (chunk * ROUTER_CHUNK + inside).astype(jnp.int32)
        return carry

    lax.fori_loop(0, N_EXPERTS, one_expert, 0)


def _router(logits, utri, chunk_lt, chunk_of_token, cap):
    b, n, _ = logits.shape
    rows = (n // ROUTER_CHUNK) * N_EXPERTS
    assert n // ROUTER_CHUNK <= LANES
    return pl.pallas_call(
        functools.partial(_router_body, cap=cap),
        grid=(b,),
        in_specs=[
            pl.BlockSpec((1, n, LANES), lambda bi: (bi, 0, 0)),
            pl.BlockSpec((ROUTER_CHUNK, ROUTER_CHUNK), lambda bi: (0, 0)),
            pl.BlockSpec((rows, rows), lambda bi: (0, 0)),
            pl.BlockSpec((n, LANES), lambda bi: (0, 0)),
        ],
        out_specs=[
            pl.BlockSpec((1, N_EXPERTS, cap, 1), lambda bi: (bi, 0, 0, 0)),
            pl.BlockSpec((1, N_EXPERTS, n), lambda bi: (bi, 0, 0)),
        ],
        out_shape=[
            jax.ShapeDtypeStruct((b, N_EXPERTS, cap, 1), jnp.int32),
            jax.ShapeDtypeStruct((b, N_EXPERTS, n), F32),
        ],
        scratch_shapes=[pltpu.VMEM((rows, ROUTER_CHUNK), BF16), pltpu.VMEM((N_EXPERTS, LANES), F32),
                        pltpu.VMEM((N_EXPERTS, LANES), F32)],
        compiler_params=_params("arbitrary"),
        name="router",
    )(logits, utri, chunk_lt, chunk_of_token)


GATHER_UNROLL = 8


def _gather_body(idx_ref, hm_ref, o_ref):
    cap = o_ref.shape[2]

    def group(g, carry):
        base = pl.multiple_of(g * GATHER_UNROLL, GATHER_UNROLL)
        for u in range(GATHER_UNROLL):
            o_ref[0, 0, base + u] = hm_ref[0, idx_ref[0, 0, base + u]]
        return carry

    lax.fori_loop(0, cap // GATHER_UNROLL, group, 0)


def _gather(idx, hm):
    b, n = hm.shape[:2]
    cap = idx.shape[-1]
    return pl.pallas_call(
        _gather_body,
        grid=(b, N_EXPERTS),
        in_specs=[
            pl.BlockSpec((1, 1, cap), lambda bi, e: (bi * N_EXPERTS + e, 0, 0), memory_space=pltpu.SMEM),
            pl.BlockSpec((1, n, SUBLANES, LANES), lambda bi, e: (bi, 0, 0, 0)),
        ],
        out_specs=pl.BlockSpec((1, 1, cap, SUBLANES, LANES), lambda bi, e: (bi, e, 0, 0, 0)),
        out_shape=jax.ShapeDtypeStruct((b, N_EXPERTS, cap, SUBLANES, LANES), F32),
        compiler_params=_params("arbitrary", "arbitrary"),
        name="moe_gather",
    )(idx, hm)


def _ffn_body(x_ref, wg_ref, wu_ref, wd_ref, o_ref, wg_s, wu_s, wd_s):
    @pl.when(pl.program_id(1) == 0)
    def _():
        wg_s[...] = wg_ref[0, 0].astype(BF16)
        wu_s[...] = wu_ref[0, 0].astype(BF16)
        wd_s[...] = wd_ref[0, 0].astype(BF16)

    x = _tiles_to_rows(x_ref[0, 0]).astype(BF16)
    hid = _silu(jnp.dot(x, wg_s[...], preferred_element_type=F32)) * jnp.dot(x, wu_s[...], preferred_element_type=F32)
    o_ref[0, 0] = _rows_to_tiles(jnp.dot(hid.astype(BF16), wd_s[...], preferred_element_type=F32))


def _expert_ffn(xs, w_gate, w_up, w_down, layer):
    b, n_e, cap = xs.shape[:3]
    d, f = w_gate.shape[-2:]
    tiles = pl.BlockSpec((1, 1, cap, SUBLANES, LANES), lambda e, bi: (bi, e, 0, 0, 0))
    return pl.pallas_call(
        _ffn_body,
        grid=(n_e, b),
        in_specs=[
            tiles,
            pl.BlockSpec((1, 1, d, f), lambda e, bi: (layer, e, 0, 0)),
            pl.BlockSpec((1, 1, d, f), lambda e, bi: (layer, e, 0, 0)),
            pl.BlockSpec((1, 1, f, d), lambda e, bi: (layer, e, 0, 0)),
        ],
        out_specs=tiles,
        out_shape=jax.ShapeDtypeStruct(xs.shape, F32),
        scratch_shapes=[pltpu.VMEM((d, f), BF16), pltpu.VMEM((d, f), BF16), pltpu.VMEM((f, d), BF16)],
        compiler_params=_params("arbitrary", "arbitrary"),
        name="moe_ffn",
    )(xs, w_gate, w_up, w_down)


SCATTER_UNROLL = 8


def _scatter_body(idx_ref, aff_ref, ys_ref, o_ref):
    cap = ys_ref.shape[2]

    @pl.when(pl.program_id(1) == 0)
    def _():
        o_ref[...] = jnp.zeros(o_ref.shape, F32)

    def group(g, carry):
        base = pl.multiple_of(g * SCATTER_UNROLL, SCATTER_UNROLL)
        rows = [idx_ref[0, 0, base + u] for u in range(SCATTER_UNROLL)]
        sums = [o_ref[0, rows[u]] + ys_ref[0, 0, base + u] * aff_ref[0, 0, rows[u]] for u in range(SCATTER_UNROLL)]
        for u in range(SCATTER_UNROLL):
            o_ref[0, rows[u]] = sums[u]
        return carry

    lax.fori_loop(0, cap // SCATTER_UNROLL, group, 0)


def _scatter(idx, aff, ys, n):
    b, n_e, cap = ys.shape[:3]
    return pl.pallas_call(
        _scatter_body,
        grid=(b, n_e),
        in_specs=[
            pl.BlockSpec((1, 1, cap), lambda bi, e: (bi * N_EXPERTS + e, 0, 0), memory_space=pltpu.SMEM),
            pl.BlockSpec((1, 1, n), lambda bi, e: (bi * N_EXPERTS + e, 0, 0), memory_space=pltpu.SMEM),
            pl.BlockSpec((1, 1, cap, SUBLANES, LANES), lambda bi, e: (bi, e, 0, 0, 0)),
        ],
        out_specs=pl.BlockSpec((1, n, SUBLANES, LANES), lambda bi, e: (bi, 0, 0, 0)),
        out_shape=jax.ShapeDtypeStruct((b, n, SUBLANES, LANES), F32),
        compiler_params=_params("arbitrary", "arbitrary"),
        name="moe_scatter",
    )(idx, aff, ys)


def _ec_moe(hm, logits, w_gate, w_up, w_down, layer, tables):
    b, n = hm.shape[:2]
    cap = max(1, EC_CAPACITY * n // N_EXPERTS)
    idx, aff = _router(logits, *tables, cap)
    idx = idx.reshape(b * N_EXPERTS, 1, cap)
    xs = _gather(idx, hm)
    ys = _expert_ffn(xs, w_gate, w_up, w_down, layer)
    return _scatter(idx, aff.reshape(b * N_EXPERTS, 1, n), ys, n)


def _final_body(h_ref, moe_ref, mod_ref, g_ref, o_ref):
    h = h_ref[0] + mod_ref[0, 5:6, :] * _tiles_to_rows(moe_ref[0])
    o_ref[0] = h * lax.rsqrt(jnp.mean(h * h, axis=-1, keepdims=True) + EPS) * g_ref[...]


def _final(h, moe, mods, g, tm=512):
    b, n, d = h.shape
    tok = pl.BlockSpec((1, tm, d), lambda bi, i: (bi, i, 0))
    return pl.pallas_call(
        _final_body,
        grid=(b, n // tm),
        in_specs=[tok, _tile_spec(tm, d), pl.BlockSpec((1, N_MOD, d), lambda bi, i: (bi, 0, 0)),
                  pl.BlockSpec((1, d), lambda bi, i: (0, 0))],
        out_specs=tok,
        out_shape=jax.ShapeDtypeStruct((b, n, d), F32),
        compiler_params=_params("arbitrary", "arbitrary"),
        name="final_norm",
    )(h, moe, mods, g)


def _rope_tables(n):
    rows = n // GRID_W
    row = jnp.repeat(jnp.arange(rows, dtype=F32), GRID_W)
    col = jnp.tile(jnp.arange(GRID_W, dtype=F32), rows)
    axis_dim = HEAD_DIM // 2
    inv = ROPE_THETA ** (-jnp.arange(0, axis_dim, 2, dtype=F32) / axis_dim)
    ang = jnp.concatenate([row[:, None] * inv, col[:, None] * inv], axis=-1)
    cos = jnp.cos(ang)
    sin = jnp.sin(ang)
    cos64 = jnp.concatenate([cos, cos], axis=-1)
    sin64 = jnp.concatenate([-sin, sin], axis=-1)
    reps = LANES // HEAD_DIM
    return jnp.tile(cos64, (1, reps)), jnp.tile(sin64, (1, reps))


def _router_tables(n):
    r = jnp.arange(ROUTER_CHUNK)
    utri = (r[:, None] <= r[None, :]).astype(BF16)
    rows = jnp.arange((n // ROUTER_CHUNK) * N_EXPERTS)
    same_e = (rows[:, None] % N_EXPERTS) == (rows[None, :] % N_EXPERTS)
    earlier = (rows[None, :] // N_EXPERTS) < (rows[:, None] // N_EXPERTS)
    chunk_of_token = (jnp.arange(n)[:, None] // ROUTER_CHUNK) == jnp.arange(LANES)[None, :]
    return utri, (same_e & earlier).astype(BF16), chunk_of_token.astype(BF16)


def _head_ones():
    r = jnp.arange(MXU_DIM) // HEAD_DIM
    return (r[:, None] == r[None, :]).astype(BF16)


def kernel(x, c, ctx, c_ctx, ada_w, ada_b, norm1_g, norm2_g, ev_w_in, ev_q_g, ev_k_g, ev_conv_w, ev_conv_b,
           ev_ln_g, ev_ln_b, ev_w_out, sc_w_in, sc_conv_w, sc_w_out, moe_w_r, moe_w_gate, moe_w_up,
           moe_w_down, final_g):
    b, n, d = x.shape
    depth = ada_w.shape[0]
    assert depth == 2 and b < SUBLANES and n % ROUTER_CHUNK == 0

    cos2, sin2 = _rope_tables(n)
    tables = _router_tables(n)
    ones_bd = _head_ones()

    cvecs = jnp.zeros((SUBLANES, d), F32).at[:b].set(c).at[b].set(c_ctx)
    mods = _ada_mod(cvecs, ada_w, ada_b).reshape(depth, SUBLANES, N_MOD, d)
    w_r = jnp.pad(moe_w_r, ((0, 0), (0, 0), (0, LANES - N_EXPERTS))).astype(BF16)

    w_in0 = ev_w_in[0].astype(BF16)
    qg = jnp.tile(ev_q_g[0], N_Q_HEADS)[None, :]
    kg = jnp.tile(ev_k_g[0], N_KV_HEADS)[None, :]
    q, kt, v, glu = _inproj0(x, mods[0], norm1_g[0:1], w_in0, qg, kg, cos2, sin2, ones_bd)
    kt_ctx, v_ctx = _ctxkv(ctx, mods[0], b, norm1_g[0:1], w_in0[:, ATTN_W:ATTN_W + 2 * KV_W], kg, ones_bd)
    attn = _attention(q, kt_ctx, kt, v_ctx, v)
    conf = _conformer(glu, ev_conv_w[0], ev_conv_b[0:1], ev_ln_g[0:1], ev_ln_b[0:1])
    h, hm, logits = _outproj0(attn, conf, ev_w_out[0].astype(BF16), x, mods[0], norm2_g[0:1], w_r[0])
    moe0 = _ec_moe(hm, logits, moe_w_gate, moe_w_up, moe_w_down, 0, tables)

    h, bg, z = _inproj1(h, moe0, mods[0], mods[1], norm1_g[1:2], sc_w_in[0].astype(BF16))
    h, hm, logits = _outproj1(z, bg, sc_conv_w[0], sc_w_out[0].astype(BF16), h, mods[1], norm2_g[1:2], w_r[1])
    moe1 = _ec_moe(hm, logits, moe_w_gate, moe_w_up, moe_w_down, 1, tables)

    return _final(h, moe1, mods[1], final_g[None, :])
```

```python
import functools

import jax
import jax.numpy as jnp
from jax import lax
from jax.experimental import pallas as pl
from jax.experimental.pallas import tpu as pltpu

F32 = jnp.float32
BF16 = jnp.bfloat16

HEAD_DIM = 64
N_Q_HEADS = 8
N_KV_HEADS = 2
GRID_W = 64
ROPE_THETA = 10000.0
CONF_K = 31
SC_K = 3
N_EXPERTS = 16
EC_CAPACITY = 2
N_MOD = 6
EPS = 1e-6

ATTN_W = N_Q_HEADS * HEAD_DIM
KV_W = N_KV_HEADS * HEAD_DIM
Q_PER_KV = N_Q_HEADS // N_KV_HEADS
QK_SCALE = HEAD_DIM ** -0.5 * 1.4426950408889634

LANES = 128
SUBLANES = 8
MXU_DIM = 256
VMEM_LIMIT_BYTES = 60000 * 1024

CONV_HALO = 16
ROUTER_CHUNK = LANES
REFINE_STEPS = 24


def _params(*sem):
    return pltpu.CompilerParams(dimension_semantics=sem, vmem_limit_bytes=VMEM_LIMIT_BYTES)


def _norm_mod(x, g, shift, scale):
    ms = jnp.mean(x * x, axis=-1, keepdims=True)
    y = x * lax.rsqrt(ms + EPS) * g
    return y * (1.0 + scale) + shift


def _head_sumsq(x, ones_blockdiag):
    return jnp.dot((x * x).astype(BF16), ones_blockdiag, preferred_element_type=F32)


def _swap_half(x):
    w = x.shape[-1]
    lane = lax.broadcasted_iota(jnp.int32, x.shape, 1)
    first = (lane % HEAD_DIM) < (HEAD_DIM // 2)
    return jnp.where(first, pltpu.roll(x, w - HEAD_DIM // 2, axis=1), pltpu.roll(x, HEAD_DIM // 2, axis=1))


def _silu(x):
    return x * jax.nn.sigmoid(x)


def _tile_transpose8(vs):
    sub = lax.broadcasted_iota(jnp.int32, vs[0].shape, 1)
    for d in (4, 2, 1):
        keep = (sub & d) == 0
        out = list(vs)
        for i in range(SUBLANES):
            if i & d == 0:
                a, b = vs[i], vs[i + d]
                out[i] = jnp.where(keep, a, pltpu.roll(b, d, axis=1))
                out[i + d] = jnp.where(keep, pltpu.roll(a, SUBLANES - d, axis=1), b)
        vs = out
    return vs


def _rows_to_tiles(x):
    r = x.shape[0]
    vs = [x[:, a * LANES:(a + 1) * LANES].reshape(r // SUBLANES, SUBLANES, LANES) for a in range(SUBLANES)]
    return jnp.stack(_tile_transpose8(vs), axis=1).reshape(r, SUBLANES, LANES)


def _tiles_to_rows(x3):
    r = x3.shape[0]
    x4 = x3.reshape(r // SUBLANES, SUBLANES, SUBLANES, LANES)
    vs = _tile_transpose8([x4[:, j] for j in range(SUBLANES)])
    return jnp.concatenate([v.reshape(r, LANES) for v in vs], axis=1)


def _store_v_with_ones(v_ref, v):
    ones = jnp.ones((v.shape[0], HEAD_DIM), F32)
    for j in range(N_KV_HEADS):
        v_ref[0, j] = jnp.concatenate([v[:, j * HEAD_DIM:(j + 1) * HEAD_DIM], ones], axis=1).astype(BF16)


def _ada_body(c_ref, w_ref, b_ref, o_ref):
    s = _silu(c_ref[...]).astype(BF16)
    o_ref[0] = jnp.dot(s, w_ref[0].astype(BF16), preferred_element_type=F32) + b_ref[0]


def _ada_mod(cvecs, ada_w, ada_b):
    n_layers, d, n_out = ada_w.shape
    tn = n_out // 4
    return pl.pallas_call(
        _ada_body,
        grid=(n_layers, n_out // tn),
        in_specs=[
            pl.BlockSpec((SUBLANES, d), lambda l, j: (0, 0)),
            pl.BlockSpec((1, d, tn), lambda l, j: (l, 0, j)),
            pl.BlockSpec((1, 1, tn), lambda l, j: (l, 0, j)),
        ],
        out_specs=pl.BlockSpec((1, SUBLANES, tn), lambda l, j: (l, 0, j)),
        out_shape=jax.ShapeDtypeStruct((n_layers, SUBLANES, n_out), F32),
        compiler_params=_params("arbitrary", "arbitrary"),
        name="ada_mod",
    )(cvecs, ada_w, ada_b.reshape(n_layers, 1, n_out))


def _inproj0_body(h_ref, mod_ref, g1_ref, w_ref, qg_ref, kg_ref, cos_ref, sin_ref, ones_ref,
                  q_ref, kt_ref, v_ref, glu_ref):
    hn = _norm_mod(h_ref[0], g1_ref[...], mod_ref[0, 0:1, :], mod_ref[0, 1:2, :])
    proj = jnp.dot(hn.astype(BF16), w_ref[...], preferred_element_type=F32)
    ones = ones_ref[...]
    cos2 = cos_ref[...]
    sin2 = sin_ref[...]

    q = proj[:, :ATTN_W]
    ssq = jnp.concatenate([_head_sumsq(q[:, :MXU_DIM], ones), _head_sumsq(q[:, MXU_DIM:], ones)], axis=1)
    qn = q * lax.rsqrt(ssq * (1.0 / HEAD_DIM) + EPS) * qg_ref[...]
    cos = jnp.concatenate([cos2] * (ATTN_W // LANES), axis=1)
    sin = jnp.concatenate([sin2] * (ATTN_W // LANES), axis=1)
    qr = (qn * cos + _swap_half(qn) * sin) * QK_SCALE
    for h in range(N_Q_HEADS):
        q_ref[0, h] = qr[:, h * HEAD_DIM:(h + 1) * HEAD_DIM].astype(BF16)

    k = proj[:, ATTN_W:ATTN_W + KV_W]
    kn = k * lax.rsqrt(_head_sumsq(k, ones[:KV_W, :KV_W]) * (1.0 / HEAD_DIM) + EPS) * kg_ref[...]
    kr = kn * cos2 + _swap_half(kn) * sin2
    kt_ref[0] = kr.T.astype(BF16)
    _store_v_with_ones(v_ref, proj[:, ATTN_W + KV_W:ATTN_W + 2 * KV_W])

    c0 = ATTN_W + 2 * KV_W
    cc = (proj.shape[1] - c0) // 2
    glu_ref[0] = proj[:, c0:c0 + cc] * jax.nn.sigmoid(proj[:, c0 + cc:])


def _inproj0(h, mods, g1, w_in, qg, kg, cos2, sin2, ones_bd, tm=512):
    b, n, d = h.shape
    n_in = w_in.shape[1]
    conv_ch = (n_in - ATTN_W - 2 * KV_W) // 2
    return pl.pallas_call(
        _inproj0_body,
        grid=(b, n // tm),
        in_specs=[
            pl.BlockSpec((1, tm, d), lambda bi, i: (bi, i, 0)),
            pl.BlockSpec((1, N_MOD, d), lambda bi, i: (bi, 0, 0)),
            pl.BlockSpec((1, d), lambda bi, i: (0, 0)),
            pl.BlockSpec((d, n_in), lambda bi, i: (0, 0)),
            pl.BlockSpec((1, ATTN_W), lambda bi, i: (0, 0)),
            pl.BlockSpec((1, KV_W), lambda bi, i: (0, 0)),
            pl.BlockSpec((tm, LANES), lambda bi, i: (i, 0)),
            pl.BlockSpec((tm, LANES), lambda bi, i: (i, 0)),
            pl.BlockSpec((MXU_DIM, MXU_DIM), lambda bi, i: (0, 0)),
        ],
        out_specs=[
            pl.BlockSpec((1, N_Q_HEADS, tm, HEAD_DIM), lambda bi, i: (bi, 0, i, 0)),
            pl.BlockSpec((1, KV_W, tm), lambda bi, i: (bi, 0, i)),
            pl.BlockSpec((1, N_KV_HEADS, tm, 2 * HEAD_DIM), lambda bi, i: (bi, 0, i, 0)),
            pl.BlockSpec((1, tm, conv_ch), lambda bi, i: (bi, i, 0)),
        ],
        out_shape=[
            jax.ShapeDtypeStruct((b, N_Q_HEADS, n, HEAD_DIM), BF16),
            jax.ShapeDtypeStruct((b, KV_W, n), BF16),
            jax.ShapeDtypeStruct((b, N_KV_HEADS, n, 2 * HEAD_DIM), BF16),
            jax.ShapeDtypeStruct((b, n, conv_ch), F32),
        ],
        compiler_params=_params("arbitrary", "arbitrary"),
        name="inproj0",
    )(h, mods, g1, w_in, qg, kg, cos2, sin2, ones_bd)


def _ctxkv_body(x_ref, mod_ref, g1_ref, w_ref, kg_ref, ones_ref, kt_ref, v_ref):
    hn = _norm_mod(x_ref[0], g1_ref[...], mod_ref[0, 0:1, :], mod_ref[0, 1:2, :])
    proj = jnp.dot(hn.astype(BF16), w_ref[...], preferred_element_type=F32)
    k = proj[:, :KV_W]
    kn = k * lax.rsqrt(_head_sumsq(k, ones_ref[...][:KV_W, :KV_W]) * (1.0 / HEAD_DIM) + EPS) * kg_ref[...]
    kt_ref[0] = kn.T.astype(BF16)
    _store_v_with_ones(v_ref, proj[:, KV_W:])


def _ctxkv(ctx, mods, ctx_row, g1, w_kv, kg, ones_bd):
    b, t, d = ctx.shape
    return pl.pallas_call(
        _ctxkv_body,
        grid=(b,),
        in_specs=[
            pl.BlockSpec((1, t, d), lambda bi: (bi, 0, 0)),
            pl.BlockSpec((1, N_MOD, d), lambda bi: (ctx_row, 0, 0)),
            pl.BlockSpec((1, d), lambda bi: (0, 0)),
            pl.BlockSpec((d, 2 * KV_W), lambda bi: (0, 0)),
            pl.BlockSpec((1, KV_W), lambda bi: (0, 0)),
            pl.BlockSpec((MXU_DIM, MXU_DIM), lambda bi: (0, 0)),
        ],
        out_specs=[
            pl.BlockSpec((1, KV_W, t), lambda bi: (bi, 0, 0)),
            pl.BlockSpec((1, N_KV_HEADS, t, 2 * HEAD_DIM), lambda bi: (bi, 0, 0, 0)),
        ],
        out_shape=[
            jax.ShapeDtypeStruct((b, KV_W, t), BF16),
            jax.ShapeDtypeStruct((b, N_KV_HEADS, t, 2 * HEAD_DIM), BF16),
        ],
        compiler_params=_params("arbitrary"),
        name="ctx_kv",
    )(ctx, mods, g1, w_kv, kg, ones_bd)


ATTN_KEY_CHUNK = 512


def _attn_body(q_ref, ktc_ref, kt_ref, vc_ref, v_ref, o_ref, s0, s1, p0, p1):
    s_bufs = (s0, s1)
    p_bufs = (p0, p1)
    t_ctx = ktc_ref.shape[-1]
    n = kt_ref.shape[-1]
    tq = q_ref.shape[2]

    def scores(h):
        j = h // Q_PER_KV
        q = q_ref[0, h]
        s = s_bufs[h % 2]
        sc = jnp.dot(q, ktc_ref[0, j], preferred_element_type=F32)
        s[:, :t_ctx] = sc
        run = sc[:, :LANES]
        for i in range(1, t_ctx // LANES):
            run = jnp.maximum(run, sc[:, i * LANES:(i + 1) * LANES])
        for c in range(n // ATTN_KEY_CHUNK):
            lo = c * ATTN_KEY_CHUNK
            sc = jnp.dot(q, kt_ref[0, j, :, lo:lo + ATTN_KEY_CHUNK], preferred_element_type=F32)
            s[:, t_ctx + lo:t_ctx + lo + ATTN_KEY_CHUNK] = sc
            for i in range(ATTN_KEY_CHUNK // LANES):
                run = jnp.maximum(run, sc[:, i * LANES:(i + 1) * LANES])
        return jnp.max(run, axis=-1, keepdims=True)

    def exponentials(h, m):
        s = s_bufs[h % 2]
        p = p_bufs[h % 2]
        p[:, :t_ctx] = jnp.exp2(s[:, :t_ctx] - m).astype(BF16)
        for c in range(n // ATTN_KEY_CHUNK):
            lo = t_ctx + c * ATTN_KEY_CHUNK
            p[:, lo:lo + ATTN_KEY_CHUNK] = jnp.exp2(s[:, lo:lo + ATTN_KEY_CHUNK] - m).astype(BF16)

    def weighted_values(h):
        j = h // Q_PER_KV
        p = p_bufs[h % 2]
        ol = (jnp.dot(p[:, :t_ctx], vc_ref[0, j], preferred_element_type=F32)
              + jnp.dot(p[:, t_ctx:], v_ref[0, j], preferred_element_type=F32))
        o = ol / pltpu.roll(ol, HEAD_DIM, axis=1)
        o_ref[0, h] = o[:, :HEAD_DIM].astype(BF16)

    assert t_ctx % LANES == 0 and n % ATTN_KEY_CHUNK == 0
    m = scores(0)
    for h in range(N_Q_HEADS):
        m_next = scores(h + 1) if h + 1 < N_Q_HEADS else None
        if h > 0:
            weighted_values(h - 1)
        exponentials(h, m)
        m = m_next
    weighted_values(N_Q_HEADS - 1)


def _attention(q, kt_ctx, kt, v_ctx, v, tq=256):
    b, hq, n, hd = q.shape
    t_ctx = kt_ctx.shape[-1]
    t_all = t_ctx + n
    return pl.pallas_call(
        _attn_body,
        grid=(b, n // tq),
        scratch_shapes=[pltpu.VMEM((tq, t_all), F32), pltpu.VMEM((tq, t_all), F32),
                        pltpu.VMEM((tq, t_all), BF16), pltpu.VMEM((tq, t_all), BF16)],
        in_specs=[
            pl.BlockSpec((1, hq, tq, hd), lambda bi, i: (bi, 0, i, 0)),
            pl.BlockSpec((1, N_KV_HEADS, hd, t_ctx), lambda bi, i: (bi, 0, 0, 0)),
            pl.BlockSpec((1, N_KV_HEADS, hd, n), lambda bi, i: (bi, 0, 0, 0)),
            pl.BlockSpec((1, N_KV_HEADS, t_ctx, 2 * hd), lambda bi, i: (bi, 0, 0, 0)),
            pl.BlockSpec((1, N_KV_HEADS, n, 2 * hd), lambda bi, i: (bi, 0, 0, 0)),
        ],
        out_specs=pl.BlockSpec((1, hq, tq, hd), lambda bi, i: (bi, 0, i, 0)),
        out_shape=jax.ShapeDtypeStruct((b, hq, n, hd), BF16),
        compiler_params=_params("arbitrary", "arbitrary"),
        name="attention",
    )(q, kt_ctx.reshape(b, N_KV_HEADS, hd, t_ctx), kt.reshape(b, N_KV_HEADS, hd, n), v_ctx, v)


def _conf_body(g_ref, w_ref, cb_ref, lg_ref, lb_ref, o_ref, xpad, *, tn):
    n = g_ref.shape[1]
    ch = g_ref.shape[2]
    xpad[0:CONV_HALO, :] = jnp.zeros((CONV_HALO, ch), F32)
    xpad[n + CONV_HALO:n + 2 * CONV_HALO, :] = jnp.zeros((CONV_HALO, ch), F32)
    xpad[CONV_HALO:n + CONV_HALO, :] = g_ref[0]
    win = tn + 2 * CONV_HALO
    off = CONV_HALO - CONF_K // 2

    def chunk(i, carry):
        r0 = pl.multiple_of(i * tn, tn)
        x = xpad[pl.ds(r0, win), :]
        acc = jnp.zeros((tn, ch), F32)
        for sub in range(SUBLANES):
            xs = x if sub == 0 else pltpu.roll(x, win - sub, axis=0)
            for a in range(win // SUBLANES):
                k = a * SUBLANES + sub - off
                if 0 <= k < CONF_K and a * SUBLANES + tn <= win:
                    acc = acc + xs[a * SUBLANES:a * SUBLANES + tn, :] * w_ref[k:k + 1, :]
        y = acc + cb_ref[...]
        mu = jnp.mean(y, axis=-1, keepdims=True)
        yc = y - mu
        var = jnp.mean(yc * yc, axis=-1, keepdims=True)
        z = yc * lax.rsqrt(var + EPS) * lg_ref[...] + lb_ref[...]
        o_ref[0, pl.ds(r0, tn), :] = _silu(z).astype(BF16)
        return carry

    lax.fori_loop(0, n // tn, chunk, 0)


def _conformer(glu, conv_w, conv_b, ln_g, ln_b, tn=64):
    b, n, ch = glu.shape
    return pl.pallas_call(
        functools.partial(_conf_body, tn=tn),
        grid=(b,),
        in_specs=[
            pl.BlockSpec((1, n, ch), lambda bi: (bi, 0, 0)),
            pl.BlockSpec((CONF_K, ch), lambda bi: (0, 0)),
            pl.BlockSpec((1, ch), lambda bi: (0, 0)),
            pl.BlockSpec((1, ch), lambda bi: (0, 0)),
            pl.BlockSpec((1, ch), lambda bi: (0, 0)),
        ],
        out_specs=pl.BlockSpec((1, n, ch), lambda bi: (bi, 0, 0)),
        out_shape=jax.ShapeDtypeStruct((b, n, ch), BF16),
        scratch_shapes=[pltpu.VMEM((n + 2 * CONV_HALO, ch), F32)],
        compiler_params=_params("arbitrary"),
        name="conformer",
    )(glu, conv_w, conv_b, ln_g, ln_b)


def _residual_router(mix, h_ref, mod_ref, g2_ref, wr_ref, h_out, hm_out, lg_out):
    h1 = h_ref[0] + mod_ref[0, 2:3, :] * mix
    h_out[0] = h1
    hm = _norm_mod(h1, g2_ref[...], mod_ref[0, 3:4, :], mod_ref[0, 4:5, :])
    hm_out[0] = _rows_to_tiles(hm)
    lg_out[0] = jnp.dot(hm.astype(BF16), wr_ref[...], preferred_element_type=F32)


def _outproj0_body(attn_ref, conf_ref, w_ref, h_ref, mod_ref, g2_ref, wr_ref, h_out, hm_out, lg_out):
    a = jnp.concatenate([attn_ref[0, h] for h in range(N_Q_HEADS)] + [conf_ref[0]], axis=1)
    mix = jnp.dot(a, w_ref[...], preferred_element_type=F32)
    _residual_router(mix, h_ref, mod_ref, g2_ref, wr_ref, h_out, hm_out, lg_out)


def _tile_spec(tm, d):
    assert d == SUBLANES * LANES
    return pl.BlockSpec((1, tm, SUBLANES, LANES), lambda bi, i: (bi, i, 0, 0))


def _token_out_specs(b, n, d, tm):
    specs = [
        pl.BlockSpec((1, tm, d), lambda bi, i: (bi, i, 0)),
        _tile_spec(tm, d),
        pl.BlockSpec((1, tm, LANES), lambda bi, i: (bi, i, 0)),
    ]
    shapes = [
        jax.ShapeDtypeStruct((b, n, d), F32),
        jax.ShapeDtypeStruct((b, n, SUBLANES, d // SUBLANES), F32),
        jax.ShapeDtypeStruct((b, n, LANES), F32),
    ]
    return specs, shapes


def _outproj0(attn, conf, w_out, h, mods, g2, w_r, tm=512):
    b, n, d = h.shape
    ch = conf.shape[-1]
    out_specs, out_shape = _token_out_specs(b, n, d, tm)
    return pl.pallas_call(
        _outproj0_body,
        grid=(b, n // tm),
        in_specs=[
            pl.BlockSpec((1, N_Q_HEADS, tm, HEAD_DIM), lambda bi, i: (bi, 0, i, 0)),
            pl.BlockSpec((1, tm, ch), lambda bi, i: (bi, i, 0)),
            pl.BlockSpec((d, d), lambda bi, i: (0, 0)),
            pl.BlockSpec((1, tm, d), lambda bi, i: (bi, i, 0)),
            pl.BlockSpec((1, N_MOD, d), lambda bi, i: (bi, 0, 0)),
            pl.BlockSpec((1, d), lambda bi, i: (0, 0)),
            pl.BlockSpec((d, LANES), lambda bi, i: (0, 0)),
        ],
        out_specs=out_specs,
        out_shape=out_shape,
        compiler_params=_params("arbitrary", "arbitrary"),
        name="outproj0",
    )(attn, conf, w_out, h, mods, g2, w_r)


def _inproj1_body(h_ref, moe_ref, modp_ref, mod_ref, g1_ref, w_ref, h_out, bg_out, z_out):
    d = h_ref.shape[2]
    h = h_ref[0] + modp_ref[0, 5:6, :] * _tiles_to_rows(moe_ref[0])
    h_out[0] = h
    hn = _norm_mod(h, g1_ref[...], mod_ref[0, 0:1, :], mod_ref[0, 1:2, :])
    proj = jnp.dot(hn.astype(BF16), w_ref[...], preferred_element_type=F32)
    bg_out[0] = proj[:, :d]
    z_out[0] = proj[:, d:2 * d] * proj[:, 2 * d:]


def _inproj1(h, moe, mods_prev, mods, g1, w_in, tm=512):
    b, n, d = h.shape
    tok = pl.BlockSpec((1, tm, d), lambda bi, i: (bi, i, 0))
    modspec = pl.BlockSpec((1, N_MOD, d), lambda bi, i: (bi, 0, 0))
    return pl.pallas_call(
        _inproj1_body,
        grid=(b, n // tm),
        in_specs=[tok, _tile_spec(tm, d), modspec, modspec,
                  pl.BlockSpec((1, d), lambda bi, i: (0, 0)),
                  pl.BlockSpec((d, 3 * d), lambda bi, i: (0, 0))],
        out_specs=[tok, tok, tok],
        out_shape=[jax.ShapeDtypeStruct((b, n, d), F32)] * 3,
        compiler_params=_params("arbitrary", "arbitrary"),
        name="inproj1",
    )(h, moe, mods_prev, mods, g1, w_in)


def _outproj1_body(z_ref, zp_ref, zn_ref, bg_ref, cw_ref, w_ref, h_ref, mod_ref, g2_ref, wr_ref,
                   h_out, hm_out, lg_out):
    i = pl.program_id(1)
    last = pl.num_programs(1) - 1
    z = z_ref[0]
    tm = z.shape[0]
    row = lax.broadcasted_iota(jnp.int32, z.shape, 0)
    prev_row = jnp.where(i > 0, zp_ref[0, SUBLANES - 1:SUBLANES, :], 0.0)
    next_row = jnp.where(i < last, zn_ref[0, 0:1, :], 0.0)
    z_dn = jnp.where(row == 0, prev_row, pltpu.roll(z, 1, axis=0))
    z_up = jnp.where(row == tm - 1, next_row, pltpu.roll(z, tm - 1, axis=0))
    y = z_dn * cw_ref[0:1, :] + z * cw_ref[1:2, :] + z_up * cw_ref[2:3, :]
    mix = jnp.dot((bg_ref[0] * y).astype(BF16), w_ref[...], preferred_element_type=F32)
    _residual_router(mix, h_ref, mod_ref, g2_ref, wr_ref, h_out, hm_out, lg_out)


def _outproj1(z, bg, conv_w, w_out, h, mods, g2, w_r, tm=512):
    b, n, d = h.shape
    per = tm // SUBLANES
    nblk8 = n // SUBLANES
    tok = pl.BlockSpec((1, tm, d), lambda bi, i: (bi, i, 0))
    out_specs, out_shape = _token_out_specs(b, n, d, tm)
    return pl.pallas_call(
        _outproj1_body,
        grid=(b, n // tm),
        in_specs=[
            tok,
            pl.BlockSpec((1, SUBLANES, d), lambda bi, i: (bi, jnp.maximum(i * per - 1, 0), 0)),
            pl.BlockSpec((1, SUBLANES, d), lambda bi, i: (bi, jnp.minimum((i + 1) * per, nblk8 - 1), 0)),
            tok,
            pl.BlockSpec((SC_K, d), lambda bi, i: (0, 0)),
            pl.BlockSpec((d, d), lambda bi, i: (0, 0)),
            tok,
            pl.BlockSpec((1, N_MOD, d), lambda bi, i: (bi, 0, 0)),
            pl.BlockSpec((1, d), lambda bi, i: (0, 0)),
            pl.BlockSpec((d, LANES), lambda bi, i: (0, 0)),
        ],
        out_specs=out_specs,
        out_shape=out_shape,
        compiler_params=_params("arbitrary", "arbitrary"),
        name="outproj1",
    )(z, z, z, bg, conv_w, w_out, h, mods, g2, w_r)


def _stack_chunks(x):
    n = x.shape[1]
    return jnp.concatenate([x[:, c * ROUTER_CHUNK:(c + 1) * ROUTER_CHUNK] for c in range(n // ROUTER_CHUNK)], axis=0)


def _exclusive_rank(flags, utri, chunk_lt):
    incl = jnp.dot(flags.astype(BF16), utri, preferred_element_type=F32)
    tot = jnp.broadcast_to(incl[:, ROUTER_CHUNK - 1:ROUTER_CHUNK], incl.shape)
    base = jnp.dot(chunk_lt, tot.astype(BF16), preferred_element_type=F32)
    return incl - flags + base


def _router_body(lg_ref, utri_ref, lt_ref, blk_ref, idx_ref, aff_ref, loc_scr, tot_scr, end_scr, *, cap):
    n = lg_ref.shape[1]
    n_chunks = n // ROUTER_CHUNK
    lg = lg_ref[0]
    lane = lax.broadcasted_iota(jnp.int32, lg.shape, 1)
    valid = lane < N_EXPERTS
    x = jnp.where(valid, lg, -jnp.inf)
    ex = jnp.where(valid, jnp.exp(x - jnp.max(x, axis=-1, keepdims=True)), 0.0)
    aff = ex / jnp.sum(ex, axis=-1, keepdims=True)

    aff_t = aff.T[:N_EXPERTS, :]
    aff_ref[0] = aff_t

    def count_ge(t):
        return jnp.sum(jnp.where(aff_t >= t, 1.0, 0.0), axis=-1, keepdims=True)

    def bit_step(i, bits):
        cand = bits | jnp.left_shift(jnp.int32(1), 30 - i)
        return jnp.where(count_ge(lax.bitcast_convert_type(cand, F32)) >= cap, cand, bits)

    bits = lax.fori_loop(0, 31, bit_step, jnp.zeros((N_EXPERTS, 1), jnp.int32))

    def refine(i, lo_hi):
        lo, hi = lo_hi
        mid = (lo + hi) * 0.5
        ok = count_ge(mid) >= cap
        return jnp.where(ok, mid, lo), jnp.where(ok, hi, mid)

    thr, _ = lax.fori_loop(0, REFINE_STEPS, refine,
                           (lax.bitcast_convert_type(bits, F32), lax.bitcast_convert_type(bits + 1, F32)))
    gt = jnp.where(aff_t > thr, 1.0, 0.0)
    eq = jnp.where(aff_t == thr, 1.0, 0.0)
    need = cap - jnp.sum(gt, axis=-1, keepdims=True)
    utri = utri_ref[...]
    chunk_lt = lt_ref[...]
    gt_s = _stack_chunks(gt)
    eq_s = _stack_chunks(eq)
    need_s = jnp.concatenate([need] * n_chunks, axis=0)
    sel = jnp.where((gt_s > 0) | ((eq_s > 0) & (_exclusive_rank(eq_s, utri, chunk_lt) < need_s)), 1.0, 0.0)

    loc_scr[...] = jnp.dot(sel.astype(BF16), utri, preferred_element_type=F32).astype(BF16)
    sel_t = jnp.concatenate([sel[c * N_EXPERTS:(c + 1) * N_EXPERTS, :] for c in range(n_chunks)], axis=1)
    tot = jnp.dot(sel_t.astype(BF16), blk_ref[...], preferred_element_type=F32)
    tot_scr[...] = tot
    end_scr[...] = jnp.dot(tot.astype(BF16), utri, preferred_element_type=F32)

    slot = lax.broadcasted_iota(jnp.int32, (cap, 1), 0).astype(F32)
    row_id = lax.broadcasted_iota(jnp.int32, (cap, n_chunks * N_EXPERTS), 1)

    def one_expert(e, carry):
        ends = end_scr[pl.ds(e, 1), :]
        before = ends <= slot
        chunk = jnp.sum(jnp.where(before, 1.0, 0.0), axis=-1, keepdims=True)
        base = jnp.sum(jnp.where(before, tot_scr[pl.ds(e, 1), :], 0.0), axis=-1, keepdims=True)
        pick = jnp.where(row_id == chunk.astype(jnp.int32) * N_EXPERTS + e, 1.0, 0.0).astype(BF16)
        counts = jnp.dot(pick, loc_scr[...], preferred_element_type=F32)
        inside = jnp.sum(jnp.where(counts <= slot - base, 1.0, 0.0), axis=-1, keepdims=True)
        idx_ref[0, e] = (chunk * ROUTER_CHUNK + inside).astype(jnp.int32)
        return carry

    lax.fori_loop(0, N_EXPERTS, one_expert, 0)


def _router(logits, utri, chunk_lt, chunk_of_token, cap):
    b, n, _ = logits.shape
    rows = (n // ROUTER_CHUNK) * N_EXPERTS
    assert n // ROUTER_CHUNK <= LANES
    return pl.pallas_call(
        functools.partial(_router_body, cap=cap),
        grid=(b,),
        in_specs=[
            pl.BlockSpec((1, n, LANES), lambda bi: (bi, 0, 0)),
            pl.BlockSpec((ROUTER_CHUNK, ROUTER_CHUNK), lambda bi: (0, 0)),
            pl.BlockSpec((rows, rows), lambda bi: (0, 0)),
            pl.BlockSpec((n, LANES), lambda bi: (0, 0)),
        ],
        out_specs=[
            pl.BlockSpec((1, N_EXPERTS, cap, 1), lambda bi: (bi, 0, 0, 0)),
            pl.BlockSpec((1, N_EXPERTS, n), lambda bi: (bi, 0, 0)),
        ],
        out_shape=[
            jax.ShapeDtypeStruct((b, N_EXPERTS, cap, 1), jnp.int32),
            jax.ShapeDtypeStruct((b, N_EXPERTS, n), F32),
        ],
        scratch_shapes=[pltpu.VMEM((rows, ROUTER_CHUNK), BF16), pltpu.VMEM((N_EXPERTS, LANES), F32),
                        pltpu.VMEM((N_EXPERTS, LANES), F32)],
        compiler_params=_params("arbitrary"),
        name="router",
    )(logits, utri, chunk_lt, chunk_of_token)


GATHER_UNROLL = 16


def _gather_body(idx_ref, hm_ref, o_ref):
    cap = o_ref.shape[2]

    def group(g, carry):
        base = pl.multiple_of(g * GATHER_UNROLL, GATHER_UNROLL)
        halves = []
        for half in range(GATHER_UNROLL // SUBLANES):
            tiles = [hm_ref[0, idx_ref[0, 0, base + half * SUBLANES + u]][None] for u in range(SUBLANES)]
            halves.append(_tile_transpose8(tiles))
        rows = jnp.concatenate([jnp.concatenate([h[a][0] for h in halves], axis=0) for a in range(SUBLANES)], axis=1)
        o_ref[0, 0, pl.ds(base, GATHER_UNROLL), :] = rows.astype(BF16)
        return carry

    lax.fori_loop(0, cap // GATHER_UNROLL, group, 0)


def _gather(idx, hm):
    b, n = hm.shape[:2]
    cap = idx.shape[-1]
    d = SUBLANES * LANES
    return pl.pallas_call(
        _gather_body,
        grid=(b, N_EXPERTS),
        in_specs=[
            pl.BlockSpec((1, 1, cap), lambda bi, e: (bi * N_EXPERTS + e, 0, 0), memory_space=pltpu.SMEM),
            pl.BlockSpec((1, n, SUBLANES, LANES), lambda bi, e: (bi, 0, 0, 0)),
        ],
        out_specs=pl.BlockSpec((1, 1, cap, d), lambda bi, e: (bi, e, 0, 0)),
        out_shape=jax.ShapeDtypeStruct((b, N_EXPERTS, cap, d), BF16),
        compiler_params=_params("arbitrary", "arbitrary"),
        name="moe_gather",
    )(idx, hm)


def _ffn_body(x_ref, wg_ref, wu_ref, wd_ref, o_ref):
    x = x_ref[0, 0]
    hg = jnp.dot(x, wg_ref[0, 0].astype(BF16), preferred_element_type=F32)
    hu = jnp.dot(x, wu_ref[0, 0].astype(BF16), preferred_element_type=F32)
    o_ref[0, 0] = jnp.dot((_silu(hg) * hu).astype(BF16), wd_ref[0, 0].astype(BF16), preferred_element_type=F32)


def _expert_ffn(xs, w_gate, w_up, w_down, layer):
    b, n_e, cap, d = xs.shape
    f = w_gate.shape[-1]
    rows = pl.BlockSpec((1, 1, cap, d), lambda e, bi: (bi, e, 0, 0))
    return pl.pallas_call(
        _ffn_body,
        grid=(n_e, b),
        in_specs=[
            rows,
            pl.BlockSpec((1, 1, d, f), lambda e, bi: (layer, e, 0, 0)),
            pl.BlockSpec((1, 1, d, f), lambda e, bi: (layer, e, 0, 0)),
            pl.BlockSpec((1, 1, f, d), lambda e, bi: (layer, e, 0, 0)),
        ],
        out_specs=rows,
        out_shape=jax.ShapeDtypeStruct(xs.shape, F32),
        compiler_params=_params("arbitrary", "arbitrary"),
        name="moe_ffn",
    )(xs, w_gate, w_up, w_down)


SCATTER_UNROLL = 8


def _scatter_body(idx_ref, aff_ref, ys_ref, o_ref):
    cap = ys_ref.shape[2]

    @pl.when(pl.program_id(1) == 0)
    def _():
        o_ref[...] = jnp.zeros(o_ref.shape, F32)

    def group(g, carry):
        base = pl.multiple_of(g * SCATTER_UNROLL, SCATTER_UNROLL)
        y = ys_ref[0, 0, pl.ds(base, SCATTER_UNROLL), :]
        tiles = _tile_transpose8([y[:, a * LANES:(a + 1) * LANES][None] for a in range(SUBLANES)])
        rows = [idx_ref[0, 0, base + u] for u in range(SCATTER_UNROLL)]
        sums = [o_ref[0, rows[u]] + tiles[u][0] * aff_ref[0, 0, rows[u]] for u in range(SCATTER_UNROLL)]
        for u in range(SCATTER_UNROLL):
            o_ref[0, rows[u]] = sums[u]
        return carry

    lax.fori_loop(0, cap // SCATTER_UNROLL, group, 0)


def _scatter(idx, aff, ys, n):
    b, n_e, cap, d = ys.shape
    assert SCATTER_UNROLL == SUBLANES and d == SUBLANES * LANES
    return pl.pallas_call(
        _scatter_body,
        grid=(b, n_e),
        in_specs=[
            pl.BlockSpec((1, 1, cap), lambda bi, e: (bi * N_EXPERTS + e, 0, 0), memory_space=pltpu.SMEM),
            pl.BlockSpec((1, 1, n), lambda bi, e: (bi * N_EXPERTS + e, 0, 0), memory_space=pltpu.SMEM),
            pl.BlockSpec((1, 1, cap, d), lambda bi, e: (bi, e, 0, 0)),
        ],
        out_specs=pl.BlockSpec((1, n, SUBLANES, LANES), lambda bi, e: (bi, 0, 0, 0)),
        out_shape=jax.ShapeDtypeStruct((b, n, SUBLANES, LANES), F32),
        compiler_params=_params("arbitrary", "arbitrary"),
        name="moe_scatter",
    )(idx, aff, ys)


def _ec_moe(hm, logits, w_gate, w_up, w_down, layer, tables):
    b, n = hm.shape[:2]
    cap = max(1, EC_CAPACITY * n // N_EXPERTS)
    idx, aff = _router(logits, *tables, cap)
    idx = idx.reshape(b * N_EXPERTS, 1, cap)
    xs = _gather(idx, hm)
    ys = _expert_ffn(xs, w_gate, w_up, w_down, layer)
    return _scatter(idx, aff.reshape(b * N_EXPERTS, 1, n), ys, n)


def _final_body(h_ref, moe_ref, mod_ref, g_ref, o_ref):
    h = h_ref[0] + mod_ref[0, 5:6, :] * _tiles_to_rows(moe_ref[0])
    o_ref[0] = h * lax.rsqrt(jnp.mean(h * h, axis=-1, keepdims=True) + EPS) * g_ref[...]


def _final(h, moe, mods, g, tm=512):
    b, n, d = h.shape
    tok = pl.BlockSpec((1, tm, d), lambda bi, i: (bi, i, 0))
    return pl.pallas_call(
        _final_body,
        grid=(b, n // tm),
        in_specs=[tok, _tile_spec(tm, d), pl.BlockSpec((1, N_MOD, d), lambda bi, i: (bi, 0, 0)),
                  pl.BlockSpec((1, d), lambda bi, i: (0, 0))],
        out_specs=tok,
        out_shape=jax.ShapeDtypeStruct((b, n, d), F32),
        compiler_params=_params("arbitrary", "arbitrary"),
        name="final_norm",
    )(h, moe, mods, g)


def _rope_tables(n):
    rows = n // GRID_W
    row = jnp.repeat(jnp.arange(rows, dtype=F32), GRID_W)
    col = jnp.tile(jnp.arange(GRID_W, dtype=F32), rows)
    axis_dim = HEAD_DIM // 2
    inv = ROPE_THETA ** (-jnp.arange(0, axis_dim, 2, dtype=F32) / axis_dim)
    ang = jnp.concatenate([row[:, None] * inv, col[:, None] * inv], axis=-1)
    cos = jnp.cos(ang)
    sin = jnp.sin(ang)
    cos64 = jnp.concatenate([cos, cos], axis=-1)
    sin64 = jnp.concatenate([-sin, sin], axis=-1)
    reps = LANES // HEAD_DIM
    return jnp.tile(cos64, (1, reps)), jnp.tile(sin64, (1, reps))


def _router_tables(n):
    r = jnp.arange(ROUTER_CHUNK)
    utri = (r[:, None] <= r[None, :]).astype(BF16)
    rows = jnp.arange((n // ROUTER_CHUNK) * N_EXPERTS)
    same_e = (rows[:, None] % N_EXPERTS) == (rows[None, :] % N_EXPERTS)
    earlier = (rows[None, :] // N_EXPERTS) < (rows[:, None] // N_EXPERTS)
    chunk_of_token = (jnp.arange(n)[:, None] // ROUTER_CHUNK) == jnp.arange(LANES)[None, :]
    return utri, (same_e & earlier).astype(BF16), chunk_of_token.astype(BF16)


def _head_ones():
    r = jnp.arange(MXU_DIM) // HEAD_DIM
    return (r[:, None] == r[None, :]).astype(BF16)


def kernel(x, c, ctx, c_ctx, ada_w, ada_b, norm1_g, norm2_g, ev_w_in, ev_q_g, ev_k_g, ev_conv_w, ev_conv_b,
           ev_ln_g, ev_ln_b, ev_w_out, sc_w_in, sc_conv_w, sc_w_out, moe_w_r, moe_w_gate, moe_w_up,
           moe_w_down, final_g):
    b, n, d = x.shape
    depth = ada_w.shape[0]
    assert depth == 2 and b < SUBLANES and n % ROUTER_CHUNK == 0

    cos2, sin2 = _rope_tables(n)
    tables = _router_tables(n)
    ones_bd = _head_ones()

    cvecs = jnp.zeros((SUBLANES, d), F32).at[:b].set(c).at[b].set(c_ctx)
    mods = _ada_mod(cvecs, ada_w, ada_b).reshape(depth, SUBLANES, N_MOD, d)
    w_r = jnp.pad(moe_w_r, ((0, 0), (0, 0), (0, LANES - N_EXPERTS))).astype(BF16)

    w_in0 = ev_w_in[0].astype(BF16)
    qg = jnp.tile(ev_q_g[0], N_Q_HEADS)[None, :]
    kg = jnp.tile(ev_k_g[0], N_KV_HEADS)[None, :]
    q, kt, v, glu = _inproj0(x, mods[0], norm1_g[0:1], w_in0, qg, kg, cos2, sin2, ones_bd)
    kt_ctx, v_ctx = _ctxkv(ctx, mods[0], b, norm1_g[0:1], w_in0[:, ATTN_W:ATTN_W + 2 * KV_W], kg, ones_bd)
    attn = _attention(q, kt_ctx, kt, v_ctx, v)
    conf = _conformer(glu, ev_conv_w[0], ev_conv_b[0:1], ev_ln_g[0:1], ev_ln_b[0:1])
    h, hm, logits = _outproj0(attn, conf, ev_w_out[0].astype(BF16), x, mods[0], norm2_g[0:1], w_r[0])
    moe0 = _ec_moe(hm, logits, moe_w_gate, moe_w_up, moe_w_down, 0, tables)

    h, bg, z = _inproj1(h, moe0, mods[0], mods[1], norm1_g[1:2], sc_w_in[0].astype(BF16))
    h, hm, logits = _outproj1(z, bg, sc_conv_w[0], sc_w_out[0].astype(BF16), h, mods[1], norm2_g[1:2], w_r[1])
    moe1 = _ec_moe(hm, logits, moe_w_gate, moe_w_up, moe_w_down, 1, tables)

    return _final(h, moe1, mods[1], final_g[None, :])
```

```python
import functools

import jax
import jax.numpy as jnp
import numpy as np
from jax import lax
from jax.experimental import pallas as pl
from jax.experimental.pallas import tpu as pltpu

F32 = jnp.float32
BF16 = jnp.bfloat16

HEAD_DIM = 64
N_Q_HEADS = 8
N_KV_HEADS = 2
GRID_W = 64
ROPE_THETA = 10000.0
CONF_K = 31
SC_K = 3
N_EXPERTS = 16
EC_CAPACITY = 2
N_MOD = 6
EPS = 1e-6

ATTN_W = N_Q_HEADS * HEAD_DIM
KV_W = N_KV_HEADS * HEAD_DIM
Q_PER_KV = N_Q_HEADS // N_KV_HEADS
QK_SCALE = HEAD_DIM ** -0.5 * 1.4426950408889634

LANES = 128
SUBLANES = 8
MXU_DIM = 256
VMEM_LIMIT_BYTES = 60000 * 1024

CONV_HALO = 16
ROUTER_CHUNK = LANES
REFINE_STEPS = 24


def _params(*sem):
    return pltpu.CompilerParams(dimension_semantics=sem, vmem_limit_bytes=VMEM_LIMIT_BYTES)


def _norm_mod(x, g, shift, scale):
    ms = jnp.mean(x * x, axis=-1, keepdims=True)
    y = x * lax.rsqrt(ms + EPS) * g
    return y * (1.0 + scale) + shift


def _head_sumsq(x, ones_blockdiag):
    return jnp.dot((x * x).astype(BF16), ones_blockdiag, preferred_element_type=F32)


def _swap_half(x):
    w = x.shape[-1]
    lane = lax.broadcasted_iota(jnp.int32, x.shape, 1)
    first = (lane % HEAD_DIM) < (HEAD_DIM // 2)
    return jnp.where(first, pltpu.roll(x, w - HEAD_DIM // 2, axis=1), pltpu.roll(x, HEAD_DIM // 2, axis=1))


def _silu(x):
    return x * jax.nn.sigmoid(x)


def _tile_transpose8(vs):
    sub = lax.broadcasted_iota(jnp.int32, vs[0].shape, 1)
    for d in (4, 2, 1):
        keep = (sub & d) == 0
        out = list(vs)
        for i in range(SUBLANES):
            if i & d == 0:
                a, b = vs[i], vs[i + d]
                out[i] = jnp.where(keep, a, pltpu.roll(b, d, axis=1))
                out[i + d] = jnp.where(keep, pltpu.roll(a, SUBLANES - d, axis=1), b)
        vs = out
    return vs


def _rows_to_tiles(x):
    r = x.shape[0]
    vs = [x[:, a * LANES:(a + 1) * LANES].reshape(r // SUBLANES, SUBLANES, LANES) for a in range(SUBLANES)]
    return jnp.stack(_tile_transpose8(vs), axis=1).reshape(r, SUBLANES, LANES)


def _tiles_to_rows(x3):
    r = x3.shape[0]
    x4 = x3.reshape(r // SUBLANES, SUBLANES, SUBLANES, LANES)
    vs = _tile_transpose8([x4[:, j] for j in range(SUBLANES)])
    return jnp.concatenate([v.reshape(r, LANES) for v in vs], axis=1)


def _store_v_with_ones(v_ref, v):
    ones = jnp.ones((v.shape[0], HEAD_DIM), F32)
    for j in range(N_KV_HEADS):
        v_ref[0, j] = jnp.concatenate([v[:, j * HEAD_DIM:(j + 1) * HEAD_DIM], ones], axis=1).astype(BF16)


def _ada_body(c_ref, w_ref, b_ref, o_ref):
    s = _silu(c_ref[...]).astype(BF16)
    o_ref[0] = jnp.dot(s, w_ref[0].astype(BF16), preferred_element_type=F32) + b_ref[0]


def _ada_mod(cvecs, ada_w, ada_b):
    n_layers, d, n_out = ada_w.shape
    tn = n_out // 4
    return pl.pallas_call(
        _ada_body,
        grid=(n_layers, n_out // tn),
        in_specs=[
            pl.BlockSpec((SUBLANES, d), lambda l, j: (0, 0)),
            pl.BlockSpec((1, d, tn), lambda l, j: (l, 0, j)),
            pl.BlockSpec((1, 1, tn), lambda l, j: (l, 0, j)),
        ],
        out_specs=pl.BlockSpec((1, SUBLANES, tn), lambda l, j: (l, 0, j)),
        out_shape=jax.ShapeDtypeStruct((n_layers, SUBLANES, n_out), F32),
        compiler_params=_params("arbitrary", "arbitrary"),
        name="ada_mod",
    )(cvecs, ada_w, ada_b.reshape(n_layers, 1, n_out))


def _inproj0_body(h_ref, mod_ref, g1_ref, w_ref, qg_ref, kg_ref, cos_ref, sin_ref, ones_ref,
                  q_ref, kt_ref, v_ref, glu_ref):
    hn = _norm_mod(h_ref[0], g1_ref[...], mod_ref[0, 0:1, :], mod_ref[0, 1:2, :])
    proj = jnp.dot(hn.astype(BF16), w_ref[...], preferred_element_type=F32)
    ones = ones_ref[...]
    cos2 = cos_ref[...]
    sin2 = sin_ref[...]

    q = proj[:, :ATTN_W]
    ssq = jnp.concatenate([_head_sumsq(q[:, :MXU_DIM], ones), _head_sumsq(q[:, MXU_DIM:], ones)], axis=1)
    qn = q * lax.rsqrt(ssq * (1.0 / HEAD_DIM) + EPS) * qg_ref[...]
    cos = jnp.concatenate([cos2] * (ATTN_W // LANES), axis=1)
    sin = jnp.concatenate([sin2] * (ATTN_W // LANES), axis=1)
    qr = (qn * cos + _swap_half(qn) * sin) * QK_SCALE
    for h in range(N_Q_HEADS):
        q_ref[0, h] = qr[:, h * HEAD_DIM:(h + 1) * HEAD_DIM].astype(BF16)

    k = proj[:, ATTN_W:ATTN_W + KV_W]
    kn = k * lax.rsqrt(_head_sumsq(k, ones[:KV_W, :KV_W]) * (1.0 / HEAD_DIM) + EPS) * kg_ref[...]
    kr = kn * cos2 + _swap_half(kn) * sin2
    kt_ref[0] = kr.T.astype(BF16)
    _store_v_with_ones(v_ref, proj[:, ATTN_W + KV_W:ATTN_W + 2 * KV_W])

    c0 = ATTN_W + 2 * KV_W
    cc = (proj.shape[1] - c0) // 2
    glu_ref[0] = proj[:, c0:c0 + cc] * jax.nn.sigmoid(proj[:, c0 + cc:])


def _inproj0(h, mods, g1, w_in, qg, kg, cos2, sin2, ones_bd, tm=512):
    b, n, d = h.shape
    n_in = w_in.shape[1]
    conv_ch = (n_in - ATTN_W - 2 * KV_W) // 2
    return pl.pallas_call(
        _inproj0_body,
        grid=(b, n // tm),
        in_specs=[
            pl.BlockSpec((1, tm, d), lambda bi, i: (bi, i, 0)),
            pl.BlockSpec((1, N_MOD, d), lambda bi, i: (bi, 0, 0)),
            pl.BlockSpec((1, d), lambda bi, i: (0, 0)),
            pl.BlockSpec((d, n_in), lambda bi, i: (0, 0)),
            pl.BlockSpec((1, ATTN_W), lambda bi, i: (0, 0)),
            pl.BlockSpec((1, KV_W), lambda bi, i: (0, 0)),
            pl.BlockSpec((tm, LANES), lambda bi, i: (i, 0)),
            pl.BlockSpec((tm, LANES), lambda bi, i: (i, 0)),
            pl.BlockSpec((MXU_DIM, MXU_DIM), lambda bi, i: (0, 0)),
        ],
        out_specs=[
            pl.BlockSpec((1, N_Q_HEADS, tm, HEAD_DIM), lambda bi, i: (bi, 0, i, 0)),
            pl.BlockSpec((1, KV_W, tm), lambda bi, i: (bi, 0, i)),
            pl.BlockSpec((1, N_KV_HEADS, tm, 2 * HEAD_DIM), lambda bi, i: (bi, 0, i, 0)),
            pl.BlockSpec((1, tm, conv_ch), lambda bi, i: (bi, i, 0)),
        ],
        out_shape=[
            jax.ShapeDtypeStruct((b, N_Q_HEADS, n, HEAD_DIM), BF16),
            jax.ShapeDtypeStruct((b, KV_W, n), BF16),
            jax.ShapeDtypeStruct((b, N_KV_HEADS, n, 2 * HEAD_DIM), BF16),
            jax.ShapeDtypeStruct((b, n, conv_ch), F32),
        ],
        compiler_params=_params("arbitrary", "arbitrary"),
        name="inproj0",
    )(h, mods, g1, w_in, qg, kg, cos2, sin2, ones_bd)


def _ctxkv_body(x_ref, mod_ref, g1_ref, w_ref, kg_ref, ones_ref, kt_ref, v_ref):
    hn = _norm_mod(x_ref[0], g1_ref[...], mod_ref[0, 0:1, :], mod_ref[0, 1:2, :])
    proj = jnp.dot(hn.astype(BF16), w_ref[...], preferred_element_type=F32)
    k = proj[:, :KV_W]
    kn = k * lax.rsqrt(_head_sumsq(k, ones_ref[...][:KV_W, :KV_W]) * (1.0 / HEAD_DIM) + EPS) * kg_ref[...]
    kt_ref[0] = kn.T.astype(BF16)
    _store_v_with_ones(v_ref, proj[:, KV_W:])


def _ctxkv(ctx, mods, ctx_row, g1, w_kv, kg, ones_bd):
    b, t, d = ctx.shape
    return pl.pallas_call(
        _ctxkv_body,
        grid=(b,),
        in_specs=[
            pl.BlockSpec((1, t, d), lambda bi: (bi, 0, 0)),
            pl.BlockSpec((1, N_MOD, d), lambda bi: (ctx_row, 0, 0)),
            pl.BlockSpec((1, d), lambda bi: (0, 0)),
            pl.BlockSpec((d, 2 * KV_W), lambda bi: (0, 0)),
            pl.BlockSpec((1, KV_W), lambda bi: (0, 0)),
            pl.BlockSpec((MXU_DIM, MXU_DIM), lambda bi: (0, 0)),
        ],
        out_specs=[
            pl.BlockSpec((1, KV_W, t), lambda bi: (bi, 0, 0)),
            pl.BlockSpec((1, N_KV_HEADS, t, 2 * HEAD_DIM), lambda bi: (bi, 0, 0, 0)),
        ],
        out_shape=[
            jax.ShapeDtypeStruct((b, KV_W, t), BF16),
            jax.ShapeDtypeStruct((b, N_KV_HEADS, t, 2 * HEAD_DIM), BF16),
        ],
        compiler_params=_params("arbitrary"),
        name="ctx_kv",
    )(ctx, mods, g1, w_kv, kg, ones_bd)


ATTN_KEY_CHUNK = 512


def _attn_body(q_ref, ktc_ref, kt_ref, vc_ref, v_ref, o_ref, s0, s1, p0, p1):
    s_bufs = (s0, s1)
    p_bufs = (p0, p1)
    t_ctx = ktc_ref.shape[-1]
    n = kt_ref.shape[-1]
    tq = q_ref.shape[2]

    def scores(h):
        j = h // Q_PER_KV
        q = q_ref[0, h]
        s = s_bufs[h % 2]
        sc = jnp.dot(q, ktc_ref[0, j], preferred_element_type=F32)
        s[:, :t_ctx] = sc
        run = sc[:, :LANES]
        for i in range(1, t_ctx // LANES):
            run = jnp.maximum(run, sc[:, i * LANES:(i + 1) * LANES])
        for c in range(n // ATTN_KEY_CHUNK):
            lo = c * ATTN_KEY_CHUNK
            sc = jnp.dot(q, kt_ref[0, j, :, lo:lo + ATTN_KEY_CHUNK], preferred_element_type=F32)
            s[:, t_ctx + lo:t_ctx + lo + ATTN_KEY_CHUNK] = sc
            for i in range(ATTN_KEY_CHUNK // LANES):
                run = jnp.maximum(run, sc[:, i * LANES:(i + 1) * LANES])
        return jnp.max(run, axis=-1, keepdims=True)

    def exponentials(h, m):
        s = s_bufs[h % 2]
        p = p_bufs[h % 2]
        p[:, :t_ctx] = jnp.exp2(s[:, :t_ctx] - m).astype(BF16)
        for c in range(n // ATTN_KEY_CHUNK):
            lo = t_ctx + c * ATTN_KEY_CHUNK
            p[:, lo:lo + ATTN_KEY_CHUNK] = jnp.exp2(s[:, lo:lo + ATTN_KEY_CHUNK] - m).astype(BF16)

    def weighted_values(h):
        j = h // Q_PER_KV
        p = p_bufs[h % 2]
        ol = (jnp.dot(p[:, :t_ctx], vc_ref[0, j], preferred_element_type=F32)
              + jnp.dot(p[:, t_ctx:], v_ref[0, j], preferred_element_type=F32))
        o = ol / pltpu.roll(ol, HEAD_DIM, axis=1)
        o_ref[0, h] = o[:, :HEAD_DIM].astype(BF16)

    assert t_ctx % LANES == 0 and n % ATTN_KEY_CHUNK == 0
    m = scores(0)
    for h in range(N_Q_HEADS):
        m_next = scores(h + 1) if h + 1 < N_Q_HEADS else None
        if h > 0:
            weighted_values(h - 1)
        exponentials(h, m)
        m = m_next
    weighted_values(N_Q_HEADS - 1)


def _attention(q, kt_ctx, kt, v_ctx, v, tq=256):
    b, hq, n, hd = q.shape
    t_ctx = kt_ctx.shape[-1]
    t_all = t_ctx + n
    return pl.pallas_call(
        _attn_body,
        grid=(b, n // tq),
        scratch_shapes=[pltpu.VMEM((tq, t_all), F32), pltpu.VMEM((tq, t_all), F32),
                        pltpu.VMEM((tq, t_all), BF16), pltpu.VMEM((tq, t_all), BF16)],
        in_specs=[
            pl.BlockSpec((1, hq, tq, hd), lambda bi, i: (bi, 0, i, 0)),
            pl.BlockSpec((1, N_KV_HEADS, hd, t_ctx), lambda bi, i: (bi, 0, 0, 0)),
            pl.BlockSpec((1, N_KV_HEADS, hd, n), lambda bi, i: (bi, 0, 0, 0)),
            pl.BlockSpec((1, N_KV_HEADS, t_ctx, 2 * hd), lambda bi, i: (bi, 0, 0, 0)),
            pl.BlockSpec((1, N_KV_HEADS, n, 2 * hd), lambda bi, i: (bi, 0, 0, 0)),
        ],
        out_specs=pl.BlockSpec((1, hq, tq, hd), lambda bi, i: (bi, 0, i, 0)),
        out_shape=jax.ShapeDtypeStruct((b, hq, n, hd), BF16),
        compiler_params=_params("arbitrary", "arbitrary"),
        name="attention",
    )(q, kt_ctx.reshape(b, N_KV_HEADS, hd, t_ctx), kt.reshape(b, N_KV_HEADS, hd, n), v_ctx, v)


CONV_ROWS = 64


def _conformer_rows(window, w_ref, cb_ref, lg_ref, lb_ref):
    tm = window.shape[0] - 2 * CONV_HALO
    ch = window.shape[1]
    win = CONV_ROWS + 2 * CONV_HALO
    off = CONV_HALO - CONF_K // 2
    cols = []
    for t in range(ch // LANES):
        lanes = slice(t * LANES, (t + 1) * LANES)
        blocks = []
        for r in range(tm // CONV_ROWS):
            x = window[r * CONV_ROWS:r * CONV_ROWS + win, lanes]
            acc = jnp.zeros((CONV_ROWS, LANES), F32)
            for sub in range(SUBLANES):
                xs = x if sub == 0 else pltpu.roll(x, win - sub, axis=0)
                for a in range(2 * CONV_HALO // SUBLANES):
                    k = a * SUBLANES + sub - off
                    if 0 <= k < CONF_K:
                        acc = acc + xs[a * SUBLANES:a * SUBLANES + CONV_ROWS, :] * w_ref[k:k + 1, lanes]
            blocks.append(acc)
        cols.append(jnp.concatenate(blocks, axis=0))
    y = jnp.concatenate(cols, axis=1) + cb_ref[...]
    mu = jnp.mean(y, axis=-1, keepdims=True)
    yc = y - mu
    var = jnp.mean(yc * yc, axis=-1, keepdims=True)
    return _silu(yc * lax.rsqrt(var + EPS) * lg_ref[...] + lb_ref[...])


def _residual_router(mix, h, mod_ref, g2_ref, wr_ref, h_out, hm_out, lg_out):
    h1 = h + mod_ref[0, 2:3, :] * mix
    h_out[0] = h1
    hm = _norm_mod(h1, g2_ref[...], mod_ref[0, 3:4, :], mod_ref[0, 4:5, :])
    hm_out[0] = _rows_to_tiles(hm)
    lg_out[0] = jnp.dot(hm.astype(BF16), wr_ref[...], preferred_element_type=F32)


def _outproj0_body(attn_ref, glu_ref, glup_ref, glun_ref, cw_ref, cb_ref, lg_ref, lb_ref, w_ref,
                   h_ref, mod_ref, g2_ref, wr_ref, h_out, hm_out, lg_out):
    i = pl.program_id(1)
    last = pl.num_programs(1) - 1
    before = jnp.where(i > 0, glup_ref[0], 0.0)
    after = jnp.where(i < last, glun_ref[0], 0.0)
    conf = _conformer_rows(jnp.concatenate([before, glu_ref[0], after], axis=0), cw_ref, cb_ref, lg_ref, lb_ref)
    a = jnp.concatenate([attn_ref[0, h] for h in range(N_Q_HEADS)] + [conf.astype(BF16)], axis=1)
    mix = jnp.dot(a, w_ref[...], preferred_element_type=F32)
    _residual_router(mix, h_ref[0], mod_ref, g2_ref, wr_ref, h_out, hm_out, lg_out)


def _tile_spec(tm, d):
    assert d == SUBLANES * LANES
    return pl.BlockSpec((1, tm, SUBLANES, LANES), lambda bi, i: (bi, i, 0, 0))


def _token_out_specs(b, n, d, tm):
    specs = [
        pl.BlockSpec((1, tm, d), lambda bi, i: (bi, i, 0)),
        _tile_spec(tm, d),
        pl.BlockSpec((1, tm, LANES), lambda bi, i: (bi, i, 0)),
    ]
    shapes = [
        jax.ShapeDtypeStruct((b, n, d), F32),
        jax.ShapeDtypeStruct((b, n, SUBLANES, d // SUBLANES), F32),
        jax.ShapeDtypeStruct((b, n, LANES), F32),
    ]
    return specs, shapes


def _outproj0(attn, glu, conv_w, conv_b, ln_g, ln_b, w_out, h, mods, g2, w_r, tm=512):
    b, n, d = h.shape
    ch = glu.shape[-1]
    per = tm // CONV_HALO
    n_halo = n // CONV_HALO
    vec = pl.BlockSpec((1, ch), lambda bi, i: (0, 0))
    out_specs, out_shape = _token_out_specs(b, n, d, tm)
    return pl.pallas_call(
        _outproj0_body,
        grid=(b, n // tm),
        in_specs=[
            pl.BlockSpec((1, N_Q_HEADS, tm, HEAD_DIM), lambda bi, i: (bi, 0, i, 0)),
            pl.BlockSpec((1, tm, ch), lambda bi, i: (bi, i, 0)),
            pl.BlockSpec((1, CONV_HALO, ch), lambda bi, i: (bi, jnp.maximum(i * per - 1, 0), 0)),
            pl.BlockSpec((1, CONV_HALO, ch), lambda bi, i: (bi, jnp.minimum((i + 1) * per, n_halo - 1), 0)),
            pl.BlockSpec((CONF_K, ch), lambda bi, i: (0, 0)),
            vec, vec, vec,
            pl.BlockSpec((d, d), lambda bi, i: (0, 0)),
            pl.BlockSpec((1, tm, d), lambda bi, i: (bi, i, 0)),
            pl.BlockSpec((1, N_MOD, d), lambda bi, i: (bi, 0, 0)),
            pl.BlockSpec((1, d), lambda bi, i: (0, 0)),
            pl.BlockSpec((d, LANES), lambda bi, i: (0, 0)),
        ],
        out_specs=out_specs,
        out_shape=out_shape,
        compiler_params=_params("arbitrary", "arbitrary"),
        name="outproj0",
    )(attn, glu, glu, glu, conv_w, conv_b, ln_g, ln_b, w_out, h, mods, g2, w_r)


def _mixer1_body(h_ref, hp_ref, hn_ref, moe_ref, moep_ref, moen_ref, modp_ref, mod_ref, g1_ref, w_ref,
                 cw_ref, wo_ref, g2_ref, wr_ref, h_out, hm_out, lg_out):
    i = pl.program_id(1)
    last = pl.num_programs(1) - 1
    tm, d = h_ref.shape[1:]
    gate_prev = modp_ref[0, 5:6, :]

    def stream(h_blk, moe_blk):
        return h_blk[0] + gate_prev * _tiles_to_rows(moe_blk[0])

    h = stream(h_ref, moe_ref)
    rows = jnp.concatenate([stream(hp_ref, moep_ref), h, stream(hn_ref, moen_ref)], axis=0)
    hn = _norm_mod(rows, g1_ref[...], mod_ref[0, 0:1, :], mod_ref[0, 1:2, :])
    proj = jnp.dot(hn.astype(BF16), w_ref[...], preferred_element_type=F32)
    z = proj[:, d:2 * d] * proj[:, 2 * d:]
    r = lax.broadcasted_iota(jnp.int32, z.shape, 0)
    outside = ((r < SUBLANES) & (i == 0)) | ((r >= tm + SUBLANES) & (i == last))
    z = jnp.where(outside, 0.0, z)
    win = tm + 2 * SUBLANES
    y = (pltpu.roll(z, 1, axis=0) * cw_ref[0:1, :] + z * cw_ref[1:2, :]
         + pltpu.roll(z, win - 1, axis=0) * cw_ref[2:3, :])[SUBLANES:SUBLANES + tm]
    gated = proj[SUBLANES:SUBLANES + tm, :d] * y
    mix = jnp.dot(gated.astype(BF16), wo_ref[...], preferred_element_type=F32)
    _residual_router(mix, h, mod_ref, g2_ref, wr_ref, h_out, hm_out, lg_out)


def _mixer1(h, moe, mods_prev, mods, g1, w_in, conv_w, w_out, g2, w_r, tm=512):
    b, n, d = h.shape
    per = tm // SUBLANES
    n8 = n // SUBLANES
    before = lambda bi, i: (bi, jnp.maximum(i * per - 1, 0), 0)
    after = lambda bi, i: (bi, jnp.minimum((i + 1) * per, n8 - 1), 0)
    tok = pl.BlockSpec((1, tm, d), lambda bi, i: (bi, i, 0))
    modspec = pl.BlockSpec((1, N_MOD, d), lambda bi, i: (bi, 0, 0))
    vec = pl.BlockSpec((1, d), lambda bi, i: (0, 0))
    out_specs, out_shape = _token_out_specs(b, n, d, tm)
    return pl.pallas_call(
        _mixer1_body,
        grid=(b, n // tm),
        in_specs=[
            tok,
            pl.BlockSpec((1, SUBLANES, d), before),
            pl.BlockSpec((1, SUBLANES, d), after),
            _tile_spec(tm, d),
            pl.BlockSpec((1, SUBLANES, SUBLANES, LANES), lambda bi, i: before(bi, i) + (0,)),
            pl.BlockSpec((1, SUBLANES, SUBLANES, LANES), lambda bi, i: after(bi, i) + (0,)),
            modspec, modspec, vec,
            pl.BlockSpec((d, 3 * d), lambda bi, i: (0, 0)),
            pl.BlockSpec((SC_K, d), lambda bi, i: (0, 0)),
            pl.BlockSpec((d, d), lambda bi, i: (0, 0)),
            vec,
            pl.BlockSpec((d, LANES), lambda bi, i: (0, 0)),
        ],
        out_specs=out_specs,
        out_shape=out_shape,
        compiler_params=_params("arbitrary", "arbitrary"),
        name="mixer1",
    )(h, h, h, moe, moe, moe, mods_prev, mods, g1, w_in, conv_w, w_out, g2, w_r)


def _stack_chunks(x):
    n = x.shape[1]
    return jnp.concatenate([x[:, c * ROUTER_CHUNK:(c + 1) * ROUTER_CHUNK] for c in range(n // ROUTER_CHUNK)], axis=0)


def _exclusive_rank(flags, utri, chunk_lt):
    incl = jnp.dot(flags.astype(BF16), utri, preferred_element_type=F32)
    tot = jnp.broadcast_to(incl[:, ROUTER_CHUNK - 1:ROUTER_CHUNK], incl.shape)
    base = jnp.dot(chunk_lt, tot.astype(BF16), preferred_element_type=F32)
    return incl - flags + base


def _router_body(lg_ref, utri_ref, lt_ref, blk_ref, idx_ref, aff_ref, loc_scr, tot_scr, end_scr, *, cap):
    n = lg_ref.shape[1]
    n_chunks = n // ROUTER_CHUNK
    lg = lg_ref[0]
    lane = lax.broadcasted_iota(jnp.int32, lg.shape, 1)
    valid = lane < N_EXPERTS
    x = jnp.where(valid, lg, -jnp.inf)
    ex = jnp.where(valid, jnp.exp(x - jnp.max(x, axis=-1, keepdims=True)), 0.0)
    aff = ex / jnp.sum(ex, axis=-1, keepdims=True)

    aff_t = aff.T[:N_EXPERTS, :]
    aff_ref[0] = aff_t

    def count_ge(t):
        return jnp.sum(jnp.where(aff_t >= t, 1.0, 0.0), axis=-1, keepdims=True)

    def bit_step(i, bits):
        cand = bits | jnp.left_shift(jnp.int32(1), 30 - i)
        return jnp.where(count_ge(lax.bitcast_convert_type(cand, F32)) >= cap, cand, bits)

    bits = lax.fori_loop(0, 31, bit_step, jnp.zeros((N_EXPERTS, 1), jnp.int32))

    def refine(i, lo_hi):
        lo, hi = lo_hi
        mid = (lo + hi) * 0.5
        ok = count_ge(mid) >= cap
        return jnp.where(ok, mid, lo), jnp.where(ok, hi, mid)

    thr, _ = lax.fori_loop(0, REFINE_STEPS, refine,
                           (lax.bitcast_convert_type(bits, F32), lax.bitcast_convert_type(bits + 1, F32)))
    gt = jnp.where(aff_t > thr, 1.0, 0.0)
    eq = jnp.where(aff_t == thr, 1.0, 0.0)
    need = cap - jnp.sum(gt, axis=-1, keepdims=True)
    utri = utri_ref[...]
    chunk_lt = lt_ref[...]
    gt_s = _stack_chunks(gt)
    eq_s = _stack_chunks(eq)
    need_s = jnp.concatenate([need] * n_chunks, axis=0)
    sel = jnp.where((gt_s > 0) | ((eq_s > 0) & (_exclusive_rank(eq_s, utri, chunk_lt) < need_s)), 1.0, 0.0)

    loc_scr[...] = jnp.dot(sel.astype(BF16), utri, preferred_element_type=F32).astype(BF16)
    sel_t = jnp.concatenate([sel[c * N_EXPERTS:(c + 1) * N_EXPERTS, :] for c in range(n_chunks)], axis=1)
    tot = jnp.dot(sel_t.astype(BF16), blk_ref[...], preferred_element_type=F32)
    tot_scr[...] = tot
    end_scr[...] = jnp.dot(tot.astype(BF16), utri, preferred_element_type=F32)

    slot = lax.broadcasted_iota(jnp.int32, (cap, 1), 0).astype(F32)
    row_id = lax.broadcasted_iota(jnp.int32, (cap, n_chunks * N_EXPERTS), 1)

    def one_expert(e, carry):
        ends = end_scr[pl.ds(e, 1), :]
        before = ends <= slot
        chunk = jnp.sum(jnp.where(before, 1.0, 0.0), axis=-1, keepdims=True)
        base = jnp.sum(jnp.where(before, tot_scr[pl.ds(e, 1), :], 0.0), axis=-1, keepdims=True)
        pick = jnp.where(row_id == chunk.astype(jnp.int32) * N_EXPERTS + e, 1.0, 0.0).astype(BF16)
        counts = jnp.dot(pick, loc_scr[...], preferred_element_type=F32)
        inside = jnp.sum(jnp.where(counts <= slot - base, 1.0, 0.0), axis=-1, keepdims=True)
        idx_ref[0, e] = (chunk * ROUTER_CHUNK + inside).astype(jnp.int32)
        return carry

    lax.fori_loop(0, N_EXPERTS, one_expert, 0)


def _router(logits, utri, chunk_lt, chunk_of_token, cap):
    b, n, _ = logits.shape
    rows = (n // ROUTER_CHUNK) * N_EXPERTS
    assert n // ROUTER_CHUNK <= LANES
    return pl.pallas_call(
        functools.partial(_router_body, cap=cap),
        grid=(b,),
        in_specs=[
            pl.BlockSpec((1, n, LANES), lambda bi: (bi, 0, 0)),
            pl.BlockSpec((ROUTER_CHUNK, ROUTER_CHUNK), lambda bi: (0, 0)),
            pl.BlockSpec((rows, rows), lambda bi: (0, 0)),
            pl.BlockSpec((n, LANES), lambda bi: (0, 0)),
        ],
        out_specs=[
            pl.BlockSpec((1, N_EXPERTS, cap, 1), lambda bi: (bi, 0, 0, 0)),
            pl.BlockSpec((1, N_EXPERTS, n), lambda bi: (bi, 0, 0)),
        ],
        out_shape=[
            jax.ShapeDtypeStruct((b, N_EXPERTS, cap, 1), jnp.int32),
            jax.ShapeDtypeStruct((b, N_EXPERTS, n), F32),
        ],
        scratch_shapes=[pltpu.VMEM((rows, ROUTER_CHUNK), BF16), pltpu.VMEM((N_EXPERTS, LANES), F32),
                        pltpu.VMEM((N_EXPERTS, LANES), F32)],
        compiler_params=_params("arbitrary"),
        name="router",
    )(logits, utri, chunk_lt, chunk_of_token)


GATHER_UNROLL = 16


def _gather_body(idx_ref, hm_ref, o_ref):
    cap = o_ref.shape[2]

    def group(g, carry):
        base = pl.multiple_of(g * GATHER_UNROLL, GATHER_UNROLL)
        halves = []
        for half in range(GATHER_UNROLL // SUBLANES):
            tiles = [hm_ref[0, idx_ref[0, 0, base + half * SUBLANES + u]][None] for u in range(SUBLANES)]
            halves.append(_tile_transpose8(tiles))
        rows = jnp.concatenate([jnp.concatenate([h[a][0] for h in halves], axis=0) for a in range(SUBLANES)], axis=1)
        o_ref[0, 0, pl.ds(base, GATHER_UNROLL), :] = rows.astype(BF16)
        return carry

    lax.fori_loop(0, cap // GATHER_UNROLL, group, 0)


def _gather(idx, hm):
    b, n = hm.shape[:2]
    cap = idx.shape[-1]
    d = SUBLANES * LANES
    return pl.pallas_call(
        _gather_body,
        grid=(b, N_EXPERTS),
        in_specs=[
            pl.BlockSpec((1, 1, cap), lambda bi, e: (bi * N_EXPERTS + e, 0, 0), memory_space=pltpu.SMEM),
            pl.BlockSpec((1, n, SUBLANES, LANES), lambda bi, e: (bi, 0, 0, 0)),
        ],
        out_specs=pl.BlockSpec((1, 1, cap, d), lambda bi, e: (bi, e, 0, 0)),
        out_shape=jax.ShapeDtypeStruct((b, N_EXPERTS, cap, d), BF16),
        compiler_params=_params("arbitrary", "arbitrary"),
        name="moe_gather",
    )(idx, hm)


def _ffn_body(x_ref, wg_ref, wu_ref, wd_ref, o_ref):
    x = x_ref[0, 0]
    hg = jnp.dot(x, wg_ref[0, 0].astype(BF16), preferred_element_type=F32)
    hu = jnp.dot(x, wu_ref[0, 0].astype(BF16), preferred_element_type=F32)
    o_ref[0, 0] = jnp.dot((_silu(hg) * hu).astype(BF16), wd_ref[0, 0].astype(BF16), preferred_element_type=F32)


def _expert_ffn(xs, w_gate, w_up, w_down, layer):
    b, n_e, cap, d = xs.shape
    f = w_gate.shape[-1]
    rows = pl.BlockSpec((1, 1, cap, d), lambda e, bi: (bi, e, 0, 0))
    return pl.pallas_call(
        _ffn_body,
        grid=(n_e, b),
        in_specs=[
            rows,
            pl.BlockSpec((1, 1, d, f), lambda e, bi: (layer, e, 0, 0)),
            pl.BlockSpec((1, 1, d, f), lambda e, bi: (layer, e, 0, 0)),
            pl.BlockSpec((1, 1, f, d), lambda e, bi: (layer, e, 0, 0)),
        ],
        out_specs=rows,
        out_shape=jax.ShapeDtypeStruct(xs.shape, F32),
        compiler_params=_params("arbitrary", "arbitrary"),
        name="moe_ffn",
    )(xs, w_gate, w_up, w_down)


SCATTER_UNROLL = 8


def _scatter_body(idx_ref, aff_ref, ys_ref, o_ref):
    cap = ys_ref.shape[2]

    @pl.when(pl.program_id(1) == 0)
    def _():
        o_ref[...] = jnp.zeros(o_ref.shape, F32)

    def group(g, carry):
        base = pl.multiple_of(g * SCATTER_UNROLL, SCATTER_UNROLL)
        y = ys_ref[0, 0, pl.ds(base, SCATTER_UNROLL), :]
        tiles = _tile_transpose8([y[:, a * LANES:(a + 1) * LANES][None] for a in range(SUBLANES)])
        rows = [idx_ref[0, 0, base + u] for u in range(SCATTER_UNROLL)]
        sums = [o_ref[0, rows[u]] + tiles[u][0] * aff_ref[0, 0, rows[u]] for u in range(SCATTER_UNROLL)]
        for u in range(SCATTER_UNROLL):
            o_ref[0, rows[u]] = sums[u]
        return carry

    lax.fori_loop(0, cap // SCATTER_UNROLL, group, 0)


def _scatter(idx, aff, ys, n):
    b, n_e, cap, d = ys.shape
    assert SCATTER_UNROLL == SUBLANES and d == SUBLANES * LANES
    return pl.pallas_call(
        _scatter_body,
        grid=(b, n_e),
        in_specs=[
            pl.BlockSpec((1, 1, cap), lambda bi, e: (bi * N_EXPERTS + e, 0, 0), memory_space=pltpu.SMEM),
            pl.BlockSpec((1, 1, n), lambda bi, e: (bi * N_EXPERTS + e, 0, 0), memory_space=pltpu.SMEM),
            pl.BlockSpec((1, 1, cap, d), lambda bi, e: (bi, e, 0, 0)),
        ],
        out_specs=pl.BlockSpec((1, n, SUBLANES, LANES), lambda bi, e: (bi, 0, 0, 0)),
        out_shape=jax.ShapeDtypeStruct((b, n, SUBLANES, LANES), F32),
        compiler_params=_params("arbitrary", "arbitrary"),
        name="moe_scatter",
    )(idx, aff, ys)


def _ec_moe(hm, logits, w_gate, w_up, w_down, layer, tables):
    b, n = hm.shape[:2]
    cap = max(1, EC_CAPACITY * n // N_EXPERTS)
    idx, aff = _router(logits, *tables, cap)
    idx = idx.reshape(b * N_EXPERTS, 1, cap)
    xs = _gather(idx, hm)
    ys = _expert_ffn(xs, w_gate, w_up, w_down, layer)
    return _scatter(idx, aff.reshape(b * N_EXPERTS, 1, n), ys, n)


def _final_body(h_ref, moe_ref, mod_ref, g_ref, o_ref):
    h = h_ref[0] + mod_ref[0, 5:6, :] * _tiles_to_rows(moe_ref[0])
    o_ref[0] = h * lax.rsqrt(jnp.mean(h * h, axis=-1, keepdims=True) + EPS) * g_ref[...]


def _final(h, moe, mods, g, tm=512):
    b, n, d = h.shape
    tok = pl.BlockSpec((1, tm, d), lambda bi, i: (bi, i, 0))
    return pl.pallas_call(
        _final_body,
        grid=(b, n // tm),
        in_specs=[tok, _tile_spec(tm, d), pl.BlockSpec((1, N_MOD, d), lambda bi, i: (bi, 0, 0)),
                  pl.BlockSpec((1, d), lambda bi, i: (0, 0))],
        out_specs=tok,
        out_shape=jax.ShapeDtypeStruct((b, n, d), F32),
        compiler_params=_params("arbitrary", "arbitrary"),
        name="final_norm",
    )(h, moe, mods, g)


def _rope_tables(n):
    rows = n // GRID_W
    row = np.repeat(np.arange(rows, dtype=np.float64), GRID_W)
    col = np.tile(np.arange(GRID_W, dtype=np.float64), rows)
    axis_dim = HEAD_DIM // 2
    inv = ROPE_THETA ** (-np.arange(0, axis_dim, 2, dtype=np.float64) / axis_dim)
    ang = np.concatenate([row[:, None] * inv, col[:, None] * inv], axis=-1)
    cos = np.cos(ang)
    sin = np.sin(ang)
    cos64 = np.concatenate([cos, cos], axis=-1)
    sin64 = np.concatenate([-sin, sin], axis=-1)
    reps = LANES // HEAD_DIM
    return jnp.asarray(np.tile(cos64, (1, reps)), F32), jnp.asarray(np.tile(sin64, (1, reps)), F32)


def _router_tables(n):
    r = np.arange(ROUTER_CHUNK)
    utri = r[:, None] <= r[None, :]
    rows = np.arange((n // ROUTER_CHUNK) * N_EXPERTS)
    same_e = (rows[:, None] % N_EXPERTS) == (rows[None, :] % N_EXPERTS)
    earlier = (rows[None, :] // N_EXPERTS) < (rows[:, None] // N_EXPERTS)
    chunk_of_token = (np.arange(n)[:, None] // ROUTER_CHUNK) == np.arange(LANES)[None, :]
    return tuple(jnp.asarray(t.astype(np.float32), BF16) for t in (utri, same_e & earlier, chunk_of_token))


def _head_ones():
    r = np.arange(MXU_DIM) // HEAD_DIM
    return jnp.asarray((r[:, None] == r[None, :]).astype(np.float32), BF16)


def kernel(x, c, ctx, c_ctx, ada_w, ada_b, norm1_g, norm2_g, ev_w_in, ev_q_g, ev_k_g, ev_conv_w, ev_conv_b,
           ev_ln_g, ev_ln_b, ev_w_out, sc_w_in, sc_conv_w, sc_w_out, moe_w_r, moe_w_gate, moe_w_up,
           moe_w_down, final_g):
    b, n, d = x.shape
    depth = ada_w.shape[0]
    assert depth == 2 and b < SUBLANES and n % ROUTER_CHUNK == 0

    cos2, sin2 = _rope_tables(n)
    tables = _router_tables(n)
    ones_bd = _head_ones()

    cvecs = jnp.zeros((SUBLANES, d), F32).at[:b].set(c).at[b].set(c_ctx)
    mods = _ada_mod(cvecs, ada_w, ada_b).reshape(depth, SUBLANES, N_MOD, d)
    w_r = jnp.pad(moe_w_r, ((0, 0), (0, 0), (0, LANES - N_EXPERTS))).astype(BF16)

    w_in0 = ev_w_in[0].astype(BF16)
    qg = jnp.tile(ev_q_g[0], N_Q_HEADS)[None, :]
    kg = jnp.tile(ev_k_g[0], N_KV_HEADS)[None, :]
    q, kt, v, glu = _inproj0(x, mods[0], norm1_g[0:1], w_in0, qg, kg, cos2, sin2, ones_bd)
    kt_ctx, v_ctx = _ctxkv(ctx, mods[0], b, norm1_g[0:1], w_in0[:, ATTN_W:ATTN_W + 2 * KV_W], kg, ones_bd)
    attn = _attention(q, kt_ctx, kt, v_ctx, v)
    h, hm, logits = _outproj0(attn, glu, ev_conv_w[0], ev_conv_b[0:1], ev_ln_g[0:1], ev_ln_b[0:1],
                              ev_w_out[0].astype(BF16), x, mods[0], norm2_g[0:1], w_r[0])
    moe0 = _ec_moe(hm, logits, moe_w_gate, moe_w_up, moe_w_down, 0, tables)

    h, hm, logits = _mixer1(h, moe0, mods[0], mods[1], norm1_g[1:2], sc_w_in[0].astype(BF16), sc_conv_w[0],
                            sc_w_out[0].astype(BF16), norm2_g[1:2], w_r[1])
    moe1 = _ec_moe(hm, logits, moe_w_gate, moe_w_up, moe_w_down, 1, tables)

    return _final(h, moe1, mods[1], final_g[None, :])
```

```python
import functools

import jax
import jax.numpy as jnp
import numpy as np
from jax import lax
from jax.experimental import pallas as pl
from jax.experimental.pallas import tpu as pltpu

F32 = jnp.float32
BF16 = jnp.bfloat16

HEAD_DIM = 64
N_Q_HEADS = 8
N_KV_HEADS = 2
GRID_W = 64
ROPE_THETA = 10000.0
CONF_K = 31
SC_K = 3
N_EXPERTS = 16
EC_CAPACITY = 2
N_MOD = 6
EPS = 1e-6

ATTN_W = N_Q_HEADS * HEAD_DIM
KV_W = N_KV_HEADS * HEAD_DIM
Q_PER_KV = N_Q_HEADS // N_KV_HEADS
QK_SCALE = HEAD_DIM ** -0.5 * 1.4426950408889634

LANES = 128
SUBLANES = 8
MXU_DIM = 256
VMEM_LIMIT_BYTES = 60000 * 1024

CONV_HALO = 16
ROUTER_CHUNK = LANES
REFINE_STEPS = 24
INPROJ_PARTS = 2
MIXER_PARTS = 1


def _params(*sem):
    return pltpu.CompilerParams(dimension_semantics=sem, vmem_limit_bytes=VMEM_LIMIT_BYTES)


def _norm_mod(x, g, shift, scale):
    ms = jnp.mean(x * x, axis=-1, keepdims=True)
    y = x * lax.rsqrt(ms + EPS) * g
    return y * (1.0 + scale) + shift


def _head_sumsq(x, ones_blockdiag):
    return jnp.dot((x * x).astype(BF16), ones_blockdiag, preferred_element_type=F32)


def _swap_half(x):
    w = x.shape[-1]
    lane = lax.broadcasted_iota(jnp.int32, x.shape, 1)
    first = (lane % HEAD_DIM) < (HEAD_DIM // 2)
    return jnp.where(first, pltpu.roll(x, w - HEAD_DIM // 2, axis=1), pltpu.roll(x, HEAD_DIM // 2, axis=1))


def _silu(x):
    return x * jax.nn.sigmoid(x)


def _tile_transpose8(vs):
    sub = lax.broadcasted_iota(jnp.int32, vs[0].shape, 1)
    for d in (4, 2, 1):
        keep = (sub & d) == 0
        out = list(vs)
        for i in range(SUBLANES):
            if i & d == 0:
                a, b = vs[i], vs[i + d]
                out[i] = jnp.where(keep, a, pltpu.roll(b, d, axis=1))
                out[i + d] = jnp.where(keep, pltpu.roll(a, SUBLANES - d, axis=1), b)
        vs = out
    return vs


def _rows_to_tiles(x):
    r = x.shape[0]
    vs = [x[:, a * LANES:(a + 1) * LANES].reshape(r // SUBLANES, SUBLANES, LANES) for a in range(SUBLANES)]
    return jnp.stack(_tile_transpose8(vs), axis=1).reshape(r, SUBLANES, LANES)


def _tiles_to_rows(x3):
    r = x3.shape[0]
    x4 = x3.reshape(r // SUBLANES, SUBLANES, SUBLANES, LANES)
    vs = _tile_transpose8([x4[:, j] for j in range(SUBLANES)])
    return jnp.concatenate([v.reshape(r, LANES) for v in vs], axis=1)


def _store_v_with_ones(v_ref, rows, v):
    ones = jnp.ones((v.shape[0], HEAD_DIM), F32)
    for j in range(N_KV_HEADS):
        v_ref[0, j, rows] = jnp.concatenate([v[:, j * HEAD_DIM:(j + 1) * HEAD_DIM], ones], axis=1).astype(BF16)


def _ada_body(c_ref, w_ref, b_ref, o_ref):
    s = _silu(c_ref[...]).astype(BF16)
    o_ref[0] = jnp.dot(s, w_ref[0].astype(BF16), preferred_element_type=F32) + b_ref[0]


def _ada_mod(cvecs, ada_w, ada_b):
    n_layers, d, n_out = ada_w.shape
    tn = n_out // 4
    return pl.pallas_call(
        _ada_body,
        grid=(n_layers, n_out // tn),
        in_specs=[
            pl.BlockSpec((SUBLANES, d), lambda l, j: (0, 0)),
            pl.BlockSpec((1, d, tn), lambda l, j: (l, 0, j)),
            pl.BlockSpec((1, 1, tn), lambda l, j: (l, 0, j)),
        ],
        out_specs=pl.BlockSpec((1, SUBLANES, tn), lambda l, j: (l, 0, j)),
        out_shape=jax.ShapeDtypeStruct((n_layers, SUBLANES, n_out), F32),
        compiler_params=_params("arbitrary", "arbitrary"),
        name="ada_mod",
    )(cvecs, ada_w, ada_b.reshape(n_layers, 1, n_out))


def _inproj0_body(h_ref, mod_ref, g1_ref, w_ref, qg_ref, kg_ref, cos_ref, sin_ref, ones_ref,
                  q_ref, kt_ref, v_ref, glu_ref):
    ones = ones_ref[...]
    tm = h_ref.shape[1]
    part = tm // INPROJ_PARTS
    for p in range(INPROJ_PARTS):
        rows = slice(p * part, (p + 1) * part)
        hn = _norm_mod(h_ref[0, rows, :], g1_ref[...], mod_ref[0, 0:1, :], mod_ref[0, 1:2, :])
        proj = jnp.dot(hn.astype(BF16), w_ref[...], preferred_element_type=F32)
        cos2 = cos_ref[rows, :]
        sin2 = sin_ref[rows, :]

        q = proj[:, :ATTN_W]
        ssq = jnp.concatenate([_head_sumsq(q[:, :MXU_DIM], ones), _head_sumsq(q[:, MXU_DIM:], ones)], axis=1)
        qn = q * lax.rsqrt(ssq * (1.0 / HEAD_DIM) + EPS) * qg_ref[...]
        cos = jnp.concatenate([cos2] * (ATTN_W // LANES), axis=1)
        sin = jnp.concatenate([sin2] * (ATTN_W // LANES), axis=1)
        qr = (qn * cos + _swap_half(qn) * sin) * QK_SCALE
        for h in range(N_Q_HEADS):
            q_ref[0, h, rows, :] = qr[:, h * HEAD_DIM:(h + 1) * HEAD_DIM].astype(BF16)

        k = proj[:, ATTN_W:ATTN_W + KV_W]
        kn = k * lax.rsqrt(_head_sumsq(k, ones[:KV_W, :KV_W]) * (1.0 / HEAD_DIM) + EPS) * kg_ref[...]
        kr = kn * cos2 + _swap_half(kn) * sin2
        kt_ref[0, :, rows] = kr.T.astype(BF16)
        _store_v_with_ones(v_ref, rows, proj[:, ATTN_W + KV_W:ATTN_W + 2 * KV_W])

        c0 = ATTN_W + 2 * KV_W
        cc = (proj.shape[1] - c0) // 2
        glu_ref[0, rows, :] = proj[:, c0:c0 + cc] * jax.nn.sigmoid(proj[:, c0 + cc:])


def _inproj0(h, mods, g1, w_in, qg, kg, cos2, sin2, ones_bd, tm=512):
    b, n, d = h.shape
    n_in = w_in.shape[1]
    conv_ch = (n_in - ATTN_W - 2 * KV_W) // 2
    return pl.pallas_call(
        _inproj0_body,
        grid=(b, n // tm),
        in_specs=[
            pl.BlockSpec((1, tm, d), lambda bi, i: (bi, i, 0)),
            pl.BlockSpec((1, N_MOD, d), lambda bi, i: (bi, 0, 0)),
            pl.BlockSpec((1, d), lambda bi, i: (0, 0)),
            pl.BlockSpec((d, n_in), lambda bi, i: (0, 0)),
            pl.BlockSpec((1, ATTN_W), lambda bi, i: (0, 0)),
            pl.BlockSpec((1, KV_W), lambda bi, i: (0, 0)),
            pl.BlockSpec((tm, LANES), lambda bi, i: (i, 0)),
            pl.BlockSpec((tm, LANES), lambda bi, i: (i, 0)),
            pl.BlockSpec((MXU_DIM, MXU_DIM), lambda bi, i: (0, 0)),
        ],
        out_specs=[
            pl.BlockSpec((1, N_Q_HEADS, tm, HEAD_DIM), lambda bi, i: (bi, 0, i, 0)),
            pl.BlockSpec((1, KV_W, tm), lambda bi, i: (bi, 0, i)),
            pl.BlockSpec((1, N_KV_HEADS, tm, 2 * HEAD_DIM), lambda bi, i: (bi, 0, i, 0)),
            pl.BlockSpec((1, tm, conv_ch), lambda bi, i: (bi, i, 0)),
        ],
        out_shape=[
            jax.ShapeDtypeStruct((b, N_Q_HEADS, n, HEAD_DIM), BF16),
            jax.ShapeDtypeStruct((b, KV_W, n), BF16),
            jax.ShapeDtypeStruct((b, N_KV_HEADS, n, 2 * HEAD_DIM), BF16),
            jax.ShapeDtypeStruct((b, n, conv_ch), F32),
        ],
        compiler_params=_params("arbitrary", "arbitrary"),
        name="inproj0",
    )(h, mods, g1, w_in, qg, kg, cos2, sin2, ones_bd)


def _ctxkv_body(x_ref, mod_ref, g1_ref, w_ref, kg_ref, ones_ref, kt_ref, v_ref):
    hn = _norm_mod(x_ref[0], g1_ref[...], mod_ref[0, 0:1, :], mod_ref[0, 1:2, :])
    proj = jnp.dot(hn.astype(BF16), w_ref[...], preferred_element_type=F32)
    k = proj[:, :KV_W]
    kn = k * lax.rsqrt(_head_sumsq(k, ones_ref[...][:KV_W, :KV_W]) * (1.0 / HEAD_DIM) + EPS) * kg_ref[...]
    kt_ref[0] = kn.T.astype(BF16)
    _store_v_with_ones(v_ref, slice(None), proj[:, KV_W:])


def _ctxkv(ctx, mods, ctx_row, g1, w_kv, kg, ones_bd):
    b, t, d = ctx.shape
    return pl.pallas_call(
        _ctxkv_body,
        grid=(b,),
        in_specs=[
            pl.BlockSpec((1, t, d), lambda bi: (bi, 0, 0)),
            pl.BlockSpec((1, N_MOD, d), lambda bi: (ctx_row, 0, 0)),
            pl.BlockSpec((1, d), lambda bi: (0, 0)),
            pl.BlockSpec((d, 2 * KV_W), lambda bi: (0, 0)),
            pl.BlockSpec((1, KV_W), lambda bi: (0, 0)),
            pl.BlockSpec((MXU_DIM, MXU_DIM), lambda bi: (0, 0)),
        ],
        out_specs=[
            pl.BlockSpec((1, KV_W, t), lambda bi: (bi, 0, 0)),
            pl.BlockSpec((1, N_KV_HEADS, t, 2 * HEAD_DIM), lambda bi: (bi, 0, 0, 0)),
        ],
        out_shape=[
            jax.ShapeDtypeStruct((b, KV_W, t), BF16),
            jax.ShapeDtypeStruct((b, N_KV_HEADS, t, 2 * HEAD_DIM), BF16),
        ],
        compiler_params=_params("arbitrary"),
        name="ctx_kv",
    )(ctx, mods, g1, w_kv, kg, ones_bd)


ATTN_KEY_CHUNK = 512


def _attn_body(q_ref, ktc_ref, kt_ref, vc_ref, v_ref, o_ref, s0, s1, p0, p1):
    s_bufs = (s0, s1)
    p_bufs = (p0, p1)
    t_ctx = ktc_ref.shape[-1]
    n = kt_ref.shape[-1]
    tq = q_ref.shape[2]

    def scores(h):
        j = h // Q_PER_KV
        q = q_ref[0, h]
        s = s_bufs[h % 2]
        sc = jnp.dot(q, ktc_ref[0, j], preferred_element_type=F32)
        s[:, :t_ctx] = sc
        run = sc[:, :LANES]
        for i in range(1, t_ctx // LANES):
            run = jnp.maximum(run, sc[:, i * LANES:(i + 1) * LANES])
        for c in range(n // ATTN_KEY_CHUNK):
            lo = c * ATTN_KEY_CHUNK
            sc = jnp.dot(q, kt_ref[0, j, :, lo:lo + ATTN_KEY_CHUNK], preferred_element_type=F32)
            s[:, t_ctx + lo:t_ctx + lo + ATTN_KEY_CHUNK] = sc
            for i in range(ATTN_KEY_CHUNK // LANES):
                run = jnp.maximum(run, sc[:, i * LANES:(i + 1) * LANES])
        return jnp.max(run, axis=-1, keepdims=True)

    def exponentials(h, m):
        s = s_bufs[h % 2]
        p = p_bufs[h % 2]
        p[:, :t_ctx] = jnp.exp2(s[:, :t_ctx] - m).astype(BF16)
        for c in range(n // ATTN_KEY_CHUNK):
            lo = t_ctx + c * ATTN_KEY_CHUNK
            p[:, lo:lo + ATTN_KEY_CHUNK] = jnp.exp2(s[:, lo:lo + ATTN_KEY_CHUNK] - m).astype(BF16)

    def weighted_values(h):
        j = h // Q_PER_KV
        p = p_bufs[h % 2]
        ol = (jnp.dot(p[:, :t_ctx], vc_ref[0, j], preferred_element_type=F32)
              + jnp.dot(p[:, t_ctx:], v_ref[0, j], preferred_element_type=F32))
        o = ol / pltpu.roll(ol, HEAD_DIM, axis=1)
        o_ref[0, h] = o[:, :HEAD_DIM].astype(BF16)

    assert t_ctx % LANES == 0 and n % ATTN_KEY_CHUNK == 0
    m = scores(0)
    for h in range(N_Q_HEADS):
        m_next = scores(h + 1) if h + 1 < N_Q_HEADS else None
        if h > 0:
            weighted_values(h - 1)
        exponentials(h, m)
        m = m_next
    weighted_values(N_Q_HEADS - 1)


def _attention(q, kt_ctx, kt, v_ctx, v, tq=256):
    b, hq, n, hd = q.shape
    t_ctx = kt_ctx.shape[-1]
    t_all = t_ctx + n
    return pl.pallas_call(
        _attn_body,
        grid=(b, n // tq),
        scratch_shapes=[pltpu.VMEM((tq, t_all), F32), pltpu.VMEM((tq, t_all), F32),
                        pltpu.VMEM((tq, t_all), BF16), pltpu.VMEM((tq, t_all), BF16)],
        in_specs=[
            pl.BlockSpec((1, hq, tq, hd), lambda bi, i: (bi, 0, i, 0)),
            pl.BlockSpec((1, N_KV_HEADS, hd, t_ctx), lambda bi, i: (bi, 0, 0, 0)),
            pl.BlockSpec((1, N_KV_HEADS, hd, n), lambda bi, i: (bi, 0, 0, 0)),
            pl.BlockSpec((1, N_KV_HEADS, t_ctx, 2 * hd), lambda bi, i: (bi, 0, 0, 0)),
            pl.BlockSpec((1, N_KV_HEADS, n, 2 * hd), lambda bi, i: (bi, 0, 0, 0)),
        ],
        out_specs=pl.BlockSpec((1, hq, tq, hd), lambda bi, i: (bi, 0, i, 0)),
        out_shape=jax.ShapeDtypeStruct((b, hq, n, hd), BF16),
        compiler_params=_params("arbitrary", "arbitrary"),
        name="attention",
    )(q, kt_ctx.reshape(b, N_KV_HEADS, hd, t_ctx), kt.reshape(b, N_KV_HEADS, hd, n), v_ctx, v)


CONV_ROWS = 64


def _conformer_rows(window, w_ref, cb_ref, lg_ref, lb_ref):
    tm = window.shape[0] - 2 * CONV_HALO
    ch = window.shape[1]
    win = CONV_ROWS + 2 * CONV_HALO
    off = CONV_HALO - CONF_K // 2
    cols = []
    for t in range(ch // LANES):
        lanes = slice(t * LANES, (t + 1) * LANES)
        blocks = []
        for r in range(tm // CONV_ROWS):
            x = window[r * CONV_ROWS:r * CONV_ROWS + win, lanes]
            acc = jnp.zeros((CONV_ROWS, LANES), F32)
            for sub in range(SUBLANES):
                xs = x if sub == 0 else pltpu.roll(x, win - sub, axis=0)
                for a in range(2 * CONV_HALO // SUBLANES):
                    k = a * SUBLANES + sub - off
                    if 0 <= k < CONF_K:
                        acc = acc + xs[a * SUBLANES:a * SUBLANES + CONV_ROWS, :] * w_ref[k:k + 1, lanes]
            blocks.append(acc)
        cols.append(jnp.concatenate(blocks, axis=0))
    y = jnp.concatenate(cols, axis=1) + cb_ref[...]
    mu = jnp.mean(y, axis=-1, keepdims=True)
    yc = y - mu
    var = jnp.mean(yc * yc, axis=-1, keepdims=True)
    return _silu(yc * lax.rsqrt(var + EPS) * lg_ref[...] + lb_ref[...])


def _residual_router(mix, h, mod_ref, g2_ref, wr_ref, h_out, hm_out, lg_out, rows=slice(None)):
    h1 = h + mod_ref[0, 2:3, :] * mix
    h_out[0, rows] = h1
    hm = _norm_mod(h1, g2_ref[...], mod_ref[0, 3:4, :], mod_ref[0, 4:5, :])
    hm_out[0, rows] = _rows_to_tiles(hm)
    lg_out[0, rows] = jnp.dot(hm.astype(BF16), wr_ref[...], preferred_element_type=F32)


def _outproj0_body(attn_ref, glu_ref, glup_ref, glun_ref, cw_ref, cb_ref, lg_ref, lb_ref, w_ref,
                   h_ref, mod_ref, g2_ref, wr_ref, h_out, hm_out, lg_out):
    i = pl.program_id(1)
    last = pl.num_programs(1) - 1
    before = jnp.where(i > 0, glup_ref[0], 0.0)
    after = jnp.where(i < last, glun_ref[0], 0.0)
    conf = _conformer_rows(jnp.concatenate([before, glu_ref[0], after], axis=0), cw_ref, cb_ref, lg_ref, lb_ref)
    a = jnp.concatenate([attn_ref[0, h] for h in range(N_Q_HEADS)] + [conf.astype(BF16)], axis=1)
    mix = jnp.dot(a, w_ref[...], preferred_element_type=F32)
    _residual_router(mix, h_ref[0], mod_ref, g2_ref, wr_ref, h_out, hm_out, lg_out)


def _tile_spec(tm, d):
    assert d == SUBLANES * LANES
    return pl.BlockSpec((1, tm, SUBLANES, LANES), lambda bi, i: (bi, i, 0, 0))


def _token_out_specs(b, n, d, tm):
    specs = [
        pl.BlockSpec((1, tm, d), lambda bi, i: (bi, i, 0)),
        _tile_spec(tm, d),
        pl.BlockSpec((1, tm, LANES), lambda bi, i: (bi, i, 0)),
    ]
    shapes = [
        jax.ShapeDtypeStruct((b, n, d), F32),
        jax.ShapeDtypeStruct((b, n, SUBLANES, d // SUBLANES), F32),
        jax.ShapeDtypeStruct((b, n, LANES), F32),
    ]
    return specs, shapes


def _outproj0(attn, glu, conv_w, conv_b, ln_g, ln_b, w_out, h, mods, g2, w_r, tm=512):
    b, n, d = h.shape
    ch = glu.shape[-1]
    per = tm // CONV_HALO
    n_halo = n // CONV_HALO
    vec = pl.BlockSpec((1, ch), lambda bi, i: (0, 0))
    out_specs, out_shape = _token_out_specs(b, n, d, tm)
    return pl.pallas_call(
        _outproj0_body,
        grid=(b, n // tm),
        in_specs=[
            pl.BlockSpec((1, N_Q_HEADS, tm, HEAD_DIM), lambda bi, i: (bi, 0, i, 0)),
            pl.BlockSpec((1, tm, ch), lambda bi, i: (bi, i, 0)),
            pl.BlockSpec((1, CONV_HALO, ch), lambda bi, i: (bi, jnp.maximum(i * per - 1, 0), 0)),
            pl.BlockSpec((1, CONV_HALO, ch), lambda bi, i: (bi, jnp.minimum((i + 1) * per, n_halo - 1), 0)),
            pl.BlockSpec((CONF_K, ch), lambda bi, i: (0, 0)),
            vec, vec, vec,
            pl.BlockSpec((d, d), lambda bi, i: (0, 0)),
            pl.BlockSpec((1, tm, d), lambda bi, i: (bi, i, 0)),
            pl.BlockSpec((1, N_MOD, d), lambda bi, i: (bi, 0, 0)),
            pl.BlockSpec((1, d), lambda bi, i: (0, 0)),
            pl.BlockSpec((d, LANES), lambda bi, i: (0, 0)),
        ],
        out_specs=out_specs,
        out_shape=out_shape,
        compiler_params=_params("arbitrary", "arbitrary"),
        name="outproj0",
    )(attn, glu, glu, glu, conv_w, conv_b, ln_g, ln_b, w_out, h, mods, g2, w_r)


def _mixer1_body(h_ref, hp_ref, hn_ref, moe_ref, moep_ref, moen_ref, modp_ref, mod_ref, g1_ref, w_ref,
                 cw_ref, wo_ref, g2_ref, wr_ref, h_out, hm_out, lg_out):
    i = pl.program_id(1)
    last = pl.num_programs(1) - 1
    tm, d = h_ref.shape[1:]
    gate_prev = modp_ref[0, 5:6, :]

    def stream(h_blk, moe_blk, rows=slice(None)):
        return h_blk[0, rows] + gate_prev * _tiles_to_rows(moe_blk[0, rows])

    part = tm // MIXER_PARTS
    win = part + 2 * SUBLANES
    for p in range(MIXER_PARTS):
        rows = slice(p * part, (p + 1) * part)
        h = stream(h_ref, moe_ref, rows)
        above = (stream(hp_ref, moep_ref) if p == 0
                 else stream(h_ref, moe_ref, slice(p * part - SUBLANES, p * part)))
        below = (stream(hn_ref, moen_ref) if p == MIXER_PARTS - 1
                 else stream(h_ref, moe_ref, slice((p + 1) * part, (p + 1) * part + SUBLANES)))
        hn = _norm_mod(jnp.concatenate([above, h, below], axis=0), g1_ref[...], mod_ref[0, 0:1, :], mod_ref[0, 1:2, :])
        proj = jnp.dot(hn.astype(BF16), w_ref[...], preferred_element_type=F32)
        z = proj[:, d:2 * d] * proj[:, 2 * d:]
        r = lax.broadcasted_iota(jnp.int32, z.shape, 0)
        if p == 0:
            z = jnp.where((r < SUBLANES) & (i == 0), 0.0, z)
        if p == MIXER_PARTS - 1:
            z = jnp.where((r >= part + SUBLANES) & (i == last), 0.0, z)
        y = (pltpu.roll(z, 1, axis=0) * cw_ref[0:1, :] + z * cw_ref[1:2, :]
             + pltpu.roll(z, win - 1, axis=0) * cw_ref[2:3, :])[SUBLANES:SUBLANES + part]
        gated = proj[SUBLANES:SUBLANES + part, :d] * y
        mix = jnp.dot(gated.astype(BF16), wo_ref[...], preferred_element_type=F32)
        _residual_router(mix, h, mod_ref, g2_ref, wr_ref, h_out, hm_out, lg_out, rows)


def _mixer1(h, moe, mods_prev, mods, g1, w_in, conv_w, w_out, g2, w_r, tm=512):
    b, n, d = h.shape
    per = tm // SUBLANES
    n8 = n // SUBLANES
    before = lambda bi, i: (bi, jnp.maximum(i * per - 1, 0), 0)
    after = lambda bi, i: (bi, jnp.minimum((i + 1) * per, n8 - 1), 0)
    tok = pl.BlockSpec((1, tm, d), lambda bi, i: (bi, i, 0))
    modspec = pl.BlockSpec((1, N_MOD, d), lambda bi, i: (bi, 0, 0))
    vec = pl.BlockSpec((1, d), lambda bi, i: (0, 0))
    out_specs, out_shape = _token_out_specs(b, n, d, tm)
    return pl.pallas_call(
        _mixer1_body,
        grid=(b, n // tm),
        in_specs=[
            tok,
            pl.BlockSpec((1, SUBLANES, d), before),
            pl.BlockSpec((1, SUBLANES, d), after),
            _tile_spec(tm, d),
            pl.BlockSpec((1, SUBLANES, SUBLANES, LANES), lambda bi, i: before(bi, i) + (0,)),
            pl.BlockSpec((1, SUBLANES, SUBLANES, LANES), lambda bi, i: after(bi, i) + (0,)),
            modspec, modspec, vec,
            pl.BlockSpec((d, 3 * d), lambda bi, i: (0, 0)),
            pl.BlockSpec((SC_K, d), lambda bi, i: (0, 0)),
            pl.BlockSpec((d, d), lambda bi, i: (0, 0)),
            vec,
            pl.BlockSpec((d, LANES), lambda bi, i: (0, 0)),
        ],
        out_specs=out_specs,
        out_shape=out_shape,
        compiler_params=_params("arbitrary", "arbitrary"),
        name="mixer1",
    )(h, h, h, moe, moe, moe, mods_prev, mods, g1, w_in, conv_w, w_out, g2, w_r)


def _stack_chunks(x):
    n = x.shape[1]
    return jnp.concatenate([x[:, c * ROUTER_CHUNK:(c + 1) * ROUTER_CHUNK] for c in range(n // ROUTER_CHUNK)], axis=0)


def _exclusive_rank(flags, utri, chunk_lt):
    incl = jnp.dot(flags.astype(BF16), utri, preferred_element_type=F32)
    tot = jnp.broadcast_to(incl[:, ROUTER_CHUNK - 1:ROUTER_CHUNK], incl.shape)
    base = jnp.dot(chunk_lt, tot.astype(BF16), preferred_element_type=F32)
    return incl - flags + base


def _router_body(lg_ref, utri_ref, lt_ref, blk_ref, idx_ref, aff_ref, loc_scr, tot_scr, end_scr, *, cap):
    n = lg_ref.shape[1]
    n_chunks = n // ROUTER_CHUNK
    lg = lg_ref[0]
    lane = lax.broadcasted_iota(jnp.int32, lg.shape, 1)
    valid = lane < N_EXPERTS
    x = jnp.where(valid, lg, -jnp.inf)
    ex = jnp.where(valid, jnp.exp(x - jnp.max(x, axis=-1, keepdims=True)), 0.0)
    aff = ex / jnp.sum(ex, axis=-1, keepdims=True)

    aff_t = aff.T[:N_EXPERTS, :]
    aff_ref[0] = aff_t

    def count_ge(t):
        return jnp.sum(jnp.where(aff_t >= t, 1.0, 0.0), axis=-1, keepdims=True)

    def bit_step(i, bits):
        cand = bits | jnp.left_shift(jnp.int32(1), 30 - i)
        return jnp.where(count_ge(lax.bitcast_convert_type(cand, F32)) >= cap, cand, bits)

    bits = lax.fori_loop(0, 31, bit_step, jnp.zeros((N_EXPERTS, 1), jnp.int32))

    def refine(i, lo_hi):
        lo, hi = lo_hi
        mid = (lo + hi) * 0.5
        ok = count_ge(mid) >= cap
        return jnp.where(ok, mid, lo), jnp.where(ok, hi, mid)

    thr, _ = lax.fori_loop(0, REFINE_STEPS, refine,
                           (lax.bitcast_convert_type(bits, F32), lax.bitcast_convert_type(bits + 1, F32)))
    gt = jnp.where(aff_t > thr, 1.0, 0.0)
    eq = jnp.where(aff_t == thr, 1.0, 0.0)
    need = cap - jnp.sum(gt, axis=-1, keepdims=True)
    utri = utri_ref[...]
    chunk_lt = lt_ref[...]
    gt_s = _stack_chunks(gt)
    eq_s = _stack_chunks(eq)
    need_s = jnp.concatenate([need] * n_chunks, axis=0)
    sel = jnp.where((gt_s > 0) | ((eq_s > 0) & (_exclusive_rank(eq_s, utri, chunk_lt) < need_s)), 1.0, 0.0)

    loc_scr[...] = jnp.dot(sel.astype(BF16), utri, preferred_element_type=F32).astype(BF16)
    sel_t = jnp.concatenate([sel[c * N_EXPERTS:(c + 1) * N_EXPERTS, :] for c in range(n_chunks)], axis=1)
    tot = jnp.dot(sel_t.astype(BF16), blk_ref[...], preferred_element_type=F32)
    tot_scr[...] = tot
    end_scr[...] = jnp.dot(tot.astype(BF16), utri, preferred_element_type=F32)

    slot = lax.broadcasted_iota(jnp.int32, (cap, 1), 0).astype(F32)
    row_id = lax.broadcasted_iota(jnp.int32, (cap, n_chunks * N_EXPERTS), 1)

    def one_expert(e, carry):
        ends = end_scr[pl.ds(e, 1), :]
        before = ends <= slot
        chunk = jnp.sum(jnp.where(before, 1.0, 0.0), axis=-1, keepdims=True)
        base = jnp.sum(jnp.where(before, tot_scr[pl.ds(e, 1), :], 0.0), axis=-1, keepdims=True)
        pick = jnp.where(row_id == chunk.astype(jnp.int32) * N_EXPERTS + e, 1.0, 0.0).astype(BF16)
        counts = jnp.dot(pick, loc_scr[...], preferred_element_type=F32)
        inside = jnp.sum(jnp.where(counts <= slot - base, 1.0, 0.0), axis=-1, keepdims=True)
        idx_ref[0, e] = (chunk * ROUTER_CHUNK + inside).astype(jnp.int32)
        return carry

    lax.fori_loop(0, N_EXPERTS, one_expert, 0)


def _router(logits, utri, chunk_lt, chunk_of_token, cap):
    b, n, _ = logits.shape
    rows = (n // ROUTER_CHUNK) * N_EXPERTS
    assert n // ROUTER_CHUNK <= LANES
    return pl.pallas_call(
        functools.partial(_router_body, cap=cap),
        grid=(b,),
        in_specs=[
            pl.BlockSpec((1, n, LANES), lambda bi: (bi, 0, 0)),
            pl.BlockSpec((ROUTER_CHUNK, ROUTER_CHUNK), lambda bi: (0, 0)),
            pl.BlockSpec((rows, rows), lambda bi: (0, 0)),
            pl.BlockSpec((n, LANES), lambda bi: (0, 0)),
        ],
        out_specs=[
            pl.BlockSpec((1, N_EXPERTS, cap, 1), lambda bi: (bi, 0, 0, 0)),
            pl.BlockSpec((1, N_EXPERTS, n), lambda bi: (bi, 0, 0)),
        ],
        out_shape=[
            jax.ShapeDtypeStruct((b, N_EXPERTS, cap, 1), jnp.int32),
            jax.ShapeDtypeStruct((b, N_EXPERTS, n), F32),
        ],
        scratch_shapes=[pltpu.VMEM((rows, ROUTER_CHUNK), BF16), pltpu.VMEM((N_EXPERTS, LANES), F32),
                        pltpu.VMEM((N_EXPERTS, LANES), F32)],
        compiler_params=_params("arbitrary"),
        name="router",
    )(logits, utri, chunk_lt, chunk_of_token)


GATHER_UNROLL = 16


def _gather_body(idx_ref, hm_ref, o_ref):
    cap = o_ref.shape[2]

    def group(g, carry):
        base = pl.multiple_of(g * GATHER_UNROLL, GATHER_UNROLL)
        halves = []
        for half in range(GATHER_UNROLL // SUBLANES):
            tiles = [hm_ref[0, idx_ref[0, 0, base + half * SUBLANES + u]][None] for u in range(SUBLANES)]
            halves.append(_tile_transpose8(tiles))
        rows = jnp.concatenate([jnp.concatenate([h[a][0] for h in halves], axis=0) for a in range(SUBLANES)], axis=1)
        o_ref[0, 0, pl.ds(base, GATHER_UNROLL), :] = rows.astype(BF16)
        return carry

    lax.fori_loop(0, cap // GATHER_UNROLL, group, 0)


def _gather(idx, hm):
    b, n = hm.shape[:2]
    cap = idx.shape[-1]
    d = SUBLANES * LANES
    return pl.pallas_call(
        _gather_body,
        grid=(b, N_EXPERTS),
        in_specs=[
            pl.BlockSpec((1, 1, cap), lambda bi, e: (bi * N_EXPERTS + e, 0, 0), memory_space=pltpu.SMEM),
            pl.BlockSpec((1, n, SUBLANES, LANES), lambda bi, e: (bi, 0, 0, 0)),
        ],
        out_specs=pl.BlockSpec((1, 1, cap, d), lambda bi, e: (bi, e, 0, 0)),
        out_shape=jax.ShapeDtypeStruct((b, N_EXPERTS, cap, d), BF16),
        compiler_params=_params("arbitrary", "arbitrary"),
        name="moe_gather",
    )(idx, hm)


FFN_SAMPLES_PER_STEP = 2


def _ffn_body(x_ref, wg_ref, wu_ref, wd_ref, o_ref):
    cap = x_ref.shape[2]
    x = jnp.concatenate([x_ref[s, 0] for s in range(x_ref.shape[0])], axis=0)
    hg = jnp.dot(x, wg_ref[0, 0].astype(BF16), preferred_element_type=F32)
    hu = jnp.dot(x, wu_ref[0, 0].astype(BF16), preferred_element_type=F32)
    y = jnp.dot((_silu(hg) * hu).astype(BF16), wd_ref[0, 0].astype(BF16), preferred_element_type=F32)
    for s in range(x_ref.shape[0]):
        o_ref[s, 0] = y[s * cap:(s + 1) * cap]


def _expert_ffn(xs, w_gate, w_up, w_down, layer):
    b, n_e, cap, d = xs.shape
    f = w_gate.shape[-1]
    sps = FFN_SAMPLES_PER_STEP if b % FFN_SAMPLES_PER_STEP == 0 else 1
    rows = pl.BlockSpec((sps, 1, cap, d), lambda e, bi: (bi, e, 0, 0))
    return pl.pallas_call(
        _ffn_body,
        grid=(n_e, b // sps),
        in_specs=[
            rows,
            pl.BlockSpec((1, 1, d, f), lambda e, bi: (layer, e, 0, 0)),
            pl.BlockSpec((1, 1, d, f), lambda e, bi: (layer, e, 0, 0)),
            pl.BlockSpec((1, 1, f, d), lambda e, bi: (layer, e, 0, 0)),
        ],
        out_specs=rows,
        out_shape=jax.ShapeDtypeStruct(xs.shape, F32),
        compiler_params=_params("arbitrary", "arbitrary"),
        name="moe_ffn",
    )(xs, w_gate, w_up, w_down)


SCATTER_UNROLL = 8


def _scatter_body(idx_ref, aff_ref, ys_ref, o_ref):
    cap = ys_ref.shape[2]

    @pl.when(pl.program_id(1) == 0)
    def _():
        o_ref[...] = jnp.zeros(o_ref.shape, F32)

    def group(g, carry):
        base = pl.multiple_of(g * SCATTER_UNROLL, SCATTER_UNROLL)
        y = ys_ref[0, 0, pl.ds(base, SCATTER_UNROLL), :]
        tiles = _tile_transpose8([y[:, a * LANES:(a + 1) * LANES][None] for a in range(SUBLANES)])
        rows = [idx_ref[0, 0, base + u] for u in range(SCATTER_UNROLL)]
        sums = [o_ref[0, rows[u]] + tiles[u][0] * aff_ref[0, 0, rows[u]] for u in range(SCATTER_UNROLL)]
        for u in range(SCATTER_UNROLL):
            o_ref[0, rows[u]] = sums[u]
        return carry

    lax.fori_loop(0, cap // SCATTER_UNROLL, group, 0)


def _scatter(idx, aff, ys, n):
    b, n_e, cap, d = ys.shape
    assert SCATTER_UNROLL == SUBLANES and d == SUBLANES * LANES
    return pl.pallas_call(
        _scatter_body,
        grid=(b, n_e),
        in_specs=[
            pl.BlockSpec((1, 1, cap), lambda bi, e: (bi * N_EXPERTS + e, 0, 0), memory_space=pltpu.SMEM),
            pl.BlockSpec((1, 1, n), lambda bi, e: (bi * N_EXPERTS + e, 0, 0), memory_space=pltpu.SMEM),
            pl.BlockSpec((1, 1, cap, d), lambda bi, e: (bi, e, 0, 0)),
        ],
        out_specs=pl.BlockSpec((1, n, SUBLANES, LANES), lambda bi, e: (bi, 0, 0, 0)),
        out_shape=jax.ShapeDtypeStruct((b, n, SUBLANES, LANES), F32),
        compiler_params=_params("arbitrary", "arbitrary"),
        name="moe_scatter",
    )(idx, aff, ys)


def _ec_moe(hm, logits, w_gate, w_up, w_down, layer, tables):
    b, n = hm.shape[:2]
    cap = max(1, EC_CAPACITY * n // N_EXPERTS)
    idx, aff = _router(logits, *tables, cap)
    idx = idx.reshape(b * N_EXPERTS, 1, cap)
    xs = _gather(idx, hm)
    ys = _expert_ffn(xs, w_gate, w_up, w_down, layer)
    return _scatter(idx, aff.reshape(b * N_EXPERTS, 1, n), ys, n)


def _final_body(h_ref, moe_ref, mod_ref, g_ref, o_ref):
    h = h_ref[0] + mod_ref[0, 5:6, :] * _tiles_to_rows(moe_ref[0])
    o_ref[0] = h * lax.rsqrt(jnp.mean(h * h, axis=-1, keepdims=True) + EPS) * g_ref[...]


def _final(h, moe, mods, g, tm=512):
    b, n, d = h.shape
    tok = pl.BlockSpec((1, tm, d), lambda bi, i: (bi, i, 0))
    return pl.pallas_call(
        _final_body,
        grid=(b, n // tm),
        in_specs=[tok, _tile_spec(tm, d), pl.BlockSpec((1, N_MOD, d), lambda bi, i: (bi, 0, 0)),
                  pl.BlockSpec((1, d), lambda bi, i: (0, 0))],
        out_specs=tok,
        out_shape=jax.ShapeDtypeStruct((b, n, d), F32),
        compiler_params=_params("arbitrary", "arbitrary"),
        name="final_norm",
    )(h, moe, mods, g)


def _rope_tables(n):
    rows = n // GRID_W
    row = np.repeat(np.arange(rows, dtype=np.float64), GRID_W)
    col = np.tile(np.arange(GRID_W, dtype=np.float64), rows)
    axis_dim = HEAD_DIM // 2
    inv = ROPE_THETA ** (-np.arange(0, axis_dim, 2, dtype=np.float64) / axis_dim)
    ang = np.concatenate([row[:, None] * inv, col[:, None] * inv], axis=-1)
    cos = np.cos(ang)
    sin = np.sin(ang)
    cos64 = np.concatenate([cos, cos], axis=-1)
    sin64 = np.concatenate([-sin, sin], axis=-1)
    reps = LANES // HEAD_DIM
    return jnp.asarray(np.tile(cos64, (1, reps)), F32), jnp.asarray(np.tile(sin64, (1, reps)), F32)


def _router_tables(n):
    r = np.arange(ROUTER_CHUNK)
    utri = r[:, None] <= r[None, :]
    rows = np.arange((n // ROUTER_CHUNK) * N_EXPERTS)
    same_e = (rows[:, None] % N_EXPERTS) == (rows[None, :] % N_EXPERTS)
    earlier = (rows[None, :] // N_EXPERTS) < (rows[:, None] // N_EXPERTS)
    chunk_of_token = (np.arange(n)[:, None] // ROUTER_CHUNK) == np.arange(LANES)[None, :]
    return tuple(jnp.asarray(t.astype(np.float32), BF16) for t in (utri, same_e & earlier, chunk_of_token))


def _head_ones():
    r = np.arange(MXU_DIM) // HEAD_DIM
    return jnp.asarray((r[:, None] == r[None, :]).astype(np.float32), BF16)


def kernel(x, c, ctx, c_ctx, ada_w, ada_b, norm1_g, norm2_g, ev_w_in, ev_q_g, ev_k_g, ev_conv_w, ev_conv_b,
           ev_ln_g, ev_ln_b, ev_w_out, sc_w_in, sc_conv_w, sc_w_out, moe_w_r, moe_w_gate, moe_w_up,
           moe_w_down, final_g):
    b, n, d = x.shape
    depth = ada_w.shape[0]
    assert depth == 2 and b < SUBLANES and n % ROUTER_CHUNK == 0

    cos2, sin2 = _rope_tables(n)
    tables = _router_tables(n)
    ones_bd = _head_ones()

    cvecs = jnp.zeros((SUBLANES, d), F32).at[:b].set(c).at[b].set(c_ctx)
    mods = _ada_mod(cvecs, ada_w, ada_b).reshape(depth, SUBLANES, N_MOD, d)
    w_r = jnp.pad(moe_w_r, ((0, 0), (0, 0), (0, LANES - N_EXPERTS))).astype(BF16)

    w_in0 = ev_w_in[0].astype(BF16)
    qg = jnp.tile(ev_q_g[0], N_Q_HEADS)[None, :]
    kg = jnp.tile(ev_k_g[0], N_KV_HEADS)[None, :]
    q, kt, v, glu = _inproj0(x, mods[0], norm1_g[0:1], w_in0, qg, kg, cos2, sin2, ones_bd)
    kt_ctx, v_ctx = _ctxkv(ctx, mods[0], b, norm1_g[0:1], w_in0[:, ATTN_W:ATTN_W + 2 * KV_W], kg, ones_bd)
    attn = _attention(q, kt_ctx, kt, v_ctx, v)
    h, hm, logits = _outproj0(attn, glu, ev_conv_w[0], ev_conv_b[0:1], ev_ln_g[0:1], ev_ln_b[0:1],
                              ev_w_out[0].astype(BF16), x, mods[0], norm2_g[0:1], w_r[0])
    moe0 = _ec_moe(hm, logits, moe_w_gate, moe_w_up, moe_w_down, 0, tables)

    h, hm, logits = _mixer1(h, moe0, mods[0], mods[1], norm1_g[1:2], sc_w_in[0].astype(BF16), sc_conv_w[0],
                            sc_w_out[0].astype(BF16), norm2_g[1:2], w_r[1])
    moe1 = _ec_moe(hm, logits, moe_w_gate, moe_w_up, moe_w_down, 1, tables)

    return _final(h, moe1, mods[1], final_g[None, :])
```

```python
import functools

import jax
import jax.numpy as jnp
import numpy as np
from jax import lax
from jax.experimental import pallas as pl
from jax.experimental.pallas import tpu as pltpu

F32 = jnp.float32
BF16 = jnp.bfloat16

HEAD_DIM = 64
N_Q_HEADS = 8
N_KV_HEADS = 2
GRID_W = 64
ROPE_THETA = 10000.0
CONF_K = 31
SC_K = 3
N_EXPERTS = 16
EC_CAPACITY = 2
N_MOD = 6
EPS = 1e-6

ATTN_W = N_Q_HEADS * HEAD_DIM
KV_W = N_KV_HEADS * HEAD_DIM
Q_PER_KV = N_Q_HEADS // N_KV_HEADS
QK_SCALE = HEAD_DIM ** -0.5 * 1.4426950408889634

LANES = 128
SUBLANES = 8
MXU_DIM = 256
VMEM_LIMIT_BYTES = 60000 * 1024

CONV_HALO = 16
ROUTER_CHUNK = LANES
REFINE_STEPS = 24
INPROJ_PARTS = 2
MIXER_PARTS = 1


def _params(*sem):
    return pltpu.CompilerParams(dimension_semantics=sem, vmem_limit_bytes=VMEM_LIMIT_BYTES)


def _norm_mod(x, g, shift, scale):
    ms = jnp.mean(x * x, axis=-1, keepdims=True)
    y = x * lax.rsqrt(ms + EPS) * g
    return y * (1.0 + scale) + shift


def _head_sumsq(x, ones_blockdiag):
    return jnp.dot((x * x).astype(BF16), ones_blockdiag, preferred_element_type=F32)


def _swap_half(x):
    w = x.shape[-1]
    lane = lax.broadcasted_iota(jnp.int32, x.shape, 1)
    first = (lane % HEAD_DIM) < (HEAD_DIM // 2)
    return jnp.where(first, pltpu.roll(x, w - HEAD_DIM // 2, axis=1), pltpu.roll(x, HEAD_DIM // 2, axis=1))


def _silu(x):
    return x * jax.nn.sigmoid(x)


def _tile_transpose8(vs):
    sub = lax.broadcasted_iota(jnp.int32, vs[0].shape, 1)
    for d in (4, 2, 1):
        keep = (sub & d) == 0
        out = list(vs)
        for i in range(SUBLANES):
            if i & d == 0:
                a, b = vs[i], vs[i + d]
                out[i] = jnp.where(keep, a, pltpu.roll(b, d, axis=1))
                out[i + d] = jnp.where(keep, pltpu.roll(a, SUBLANES - d, axis=1), b)
        vs = out
    return vs


def _rows_to_tiles(x):
    r = x.shape[0]
    vs = [x[:, a * LANES:(a + 1) * LANES].reshape(r // SUBLANES, SUBLANES, LANES) for a in range(SUBLANES)]
    return jnp.stack(_tile_transpose8(vs), axis=1).reshape(r, SUBLANES, LANES)


def _tiles_to_rows(x3):
    r = x3.shape[0]
    x4 = x3.reshape(r // SUBLANES, SUBLANES, SUBLANES, LANES)
    vs = _tile_transpose8([x4[:, j] for j in range(SUBLANES)])
    return jnp.concatenate([v.reshape(r, LANES) for v in vs], axis=1)


def _store_vt_with_ones(vt_ref, cols, v):
    vt = v.T
    ones = jnp.ones((HEAD_DIM, v.shape[0]), F32)
    for j in range(N_KV_HEADS):
        vt_ref[0, j, :, cols] = jnp.concatenate([vt[j * HEAD_DIM:(j + 1) * HEAD_DIM, :], ones], axis=0).astype(BF16)


def _ada_body(c_ref, w_ref, b_ref, o_ref):
    s = _silu(c_ref[...]).astype(BF16)
    o_ref[0] = jnp.dot(s, w_ref[0].astype(BF16), preferred_element_type=F32) + b_ref[0]


def _ada_mod(cvecs, ada_w, ada_b):
    n_layers, d, n_out = ada_w.shape
    tn = n_out // 4
    return pl.pallas_call(
        _ada_body,
        grid=(n_layers, n_out // tn),
        in_specs=[
            pl.BlockSpec((SUBLANES, d), lambda l, j: (0, 0)),
            pl.BlockSpec((1, d, tn), lambda l, j: (l, 0, j)),
            pl.BlockSpec((1, 1, tn), lambda l, j: (l, 0, j)),
        ],
        out_specs=pl.BlockSpec((1, SUBLANES, tn), lambda l, j: (l, 0, j)),
        out_shape=jax.ShapeDtypeStruct((n_layers, SUBLANES, n_out), F32),
        compiler_params=_params("arbitrary", "arbitrary"),
        name="ada_mod",
    )(cvecs, ada_w, ada_b.reshape(n_layers, 1, n_out))


def _inproj0_body(h_ref, mod_ref, g1_ref, w_ref, qg_ref, kg_ref, cos_ref, sin_ref, ones_ref,
                  qt_ref, k_ref, vt_ref, glu_ref):
    ones = ones_ref[...]
    tm = h_ref.shape[1]
    part = tm // INPROJ_PARTS
    for p in range(INPROJ_PARTS):
        rows = slice(p * part, (p + 1) * part)
        hn = _norm_mod(h_ref[0, rows, :], g1_ref[...], mod_ref[0, 0:1, :], mod_ref[0, 1:2, :])
        proj = jnp.dot(hn.astype(BF16), w_ref[...], preferred_element_type=F32)
        cos2 = cos_ref[rows, :]
        sin2 = sin_ref[rows, :]

        q = proj[:, :ATTN_W]
        ssq = jnp.concatenate([_head_sumsq(q[:, :MXU_DIM], ones), _head_sumsq(q[:, MXU_DIM:], ones)], axis=1)
        qn = q * lax.rsqrt(ssq * (1.0 / HEAD_DIM) + EPS) * qg_ref[...]
        cos = jnp.concatenate([cos2] * (ATTN_W // LANES), axis=1)
        sin = jnp.concatenate([sin2] * (ATTN_W // LANES), axis=1)
        qr = (qn * cos + _swap_half(qn) * sin) * QK_SCALE
        qt_ref[0, :, rows] = qr.T.astype(BF16)

        k = proj[:, ATTN_W:ATTN_W + KV_W]
        kn = k * lax.rsqrt(_head_sumsq(k, ones[:KV_W, :KV_W]) * (1.0 / HEAD_DIM) + EPS) * kg_ref[...]
        k_ref[0, rows, :] = (kn * cos2 + _swap_half(kn) * sin2).astype(BF16)
        _store_vt_with_ones(vt_ref, rows, proj[:, ATTN_W + KV_W:ATTN_W + 2 * KV_W])

        c0 = ATTN_W + 2 * KV_W
        cc = (proj.shape[1] - c0) // 2
        glu_ref[0, rows, :] = proj[:, c0:c0 + cc] * jax.nn.sigmoid(proj[:, c0 + cc:])


def _inproj0(h, mods, g1, w_in, qg, kg, cos2, sin2, ones_bd, tm=512):
    b, n, d = h.shape
    n_in = w_in.shape[1]
    conv_ch = (n_in - ATTN_W - 2 * KV_W) // 2
    return pl.pallas_call(
        _inproj0_body,
        grid=(b, n // tm),
        in_specs=[
            pl.BlockSpec((1, tm, d), lambda bi, i: (bi, i, 0)),
            pl.BlockSpec((1, N_MOD, d), lambda bi, i: (bi, 0, 0)),
            pl.BlockSpec((1, d), lambda bi, i: (0, 0)),
            pl.BlockSpec((d, n_in), lambda bi, i: (0, 0)),
            pl.BlockSpec((1, ATTN_W), lambda bi, i: (0, 0)),
            pl.BlockSpec((1, KV_W), lambda bi, i: (0, 0)),
            pl.BlockSpec((tm, LANES), lambda bi, i: (i, 0)),
            pl.BlockSpec((tm, LANES), lambda bi, i: (i, 0)),
            pl.BlockSpec((MXU_DIM, MXU_DIM), lambda bi, i: (0, 0)),
        ],
        out_specs=[
            pl.BlockSpec((1, ATTN_W, tm), lambda bi, i: (bi, 0, i)),
            pl.BlockSpec((1, tm, KV_W), lambda bi, i: (bi, i, 0)),
            pl.BlockSpec((1, N_KV_HEADS, 2 * HEAD_DIM, tm), lambda bi, i: (bi, 0, 0, i)),
            pl.BlockSpec((1, tm, conv_ch), lambda bi, i: (bi, i, 0)),
        ],
        out_shape=[
            jax.ShapeDtypeStruct((b, ATTN_W, n), BF16),
            jax.ShapeDtypeStruct((b, n, KV_W), BF16),
            jax.ShapeDtypeStruct((b, N_KV_HEADS, 2 * HEAD_DIM, n), BF16),
            jax.ShapeDtypeStruct((b, n, conv_ch), F32),
        ],
        compiler_params=_params("arbitrary", "arbitrary"),
        name="inproj0",
    )(h, mods, g1, w_in, qg, kg, cos2, sin2, ones_bd)


def _ctxkv_body(x_ref, mod_ref, g1_ref, w_ref, kg_ref, ones_ref, k_ref, vt_ref):
    hn = _norm_mod(x_ref[0], g1_ref[...], mod_ref[0, 0:1, :], mod_ref[0, 1:2, :])
    proj = jnp.dot(hn.astype(BF16), w_ref[...], preferred_element_type=F32)
    k = proj[:, :KV_W]
    kn = k * lax.rsqrt(_head_sumsq(k, ones_ref[...][:KV_W, :KV_W]) * (1.0 / HEAD_DIM) + EPS) * kg_ref[...]
    k_ref[0] = kn.astype(BF16)
    _store_vt_with_ones(vt_ref, slice(None), proj[:, KV_W:])


def _ctxkv(ctx, mods, ctx_row, g1, w_kv, kg, ones_bd):
    b, t, d = ctx.shape
    return pl.pallas_call(
        _ctxkv_body,
        grid=(b,),
        in_specs=[
            pl.BlockSpec((1, t, d), lambda bi: (bi, 0, 0)),
            pl.BlockSpec((1, N_MOD, d), lambda bi: (ctx_row, 0, 0)),
            pl.BlockSpec((1, d), lambda bi: (0, 0)),
            pl.BlockSpec((d, 2 * KV_W), lambda bi: (0, 0)),
            pl.BlockSpec((1, KV_W), lambda bi: (0, 0)),
            pl.BlockSpec((MXU_DIM, MXU_DIM), lambda bi: (0, 0)),
        ],
        out_specs=[
            pl.BlockSpec((1, t, KV_W), lambda bi: (bi, 0, 0)),
            pl.BlockSpec((1, N_KV_HEADS, 2 * HEAD_DIM, t), lambda bi: (bi, 0, 0, 0)),
        ],
        out_shape=[
            jax.ShapeDtypeStruct((b, t, KV_W), BF16),
            jax.ShapeDtypeStruct((b, N_KV_HEADS, 2 * HEAD_DIM, t), BF16),
        ],
        compiler_params=_params("arbitrary"),
        name="ctx_kv",
    )(ctx, mods, g1, w_kv, kg, ones_bd)


ATTN_KEY_CHUNK = 512


def _attn_body(qt_ref, kc_ref, k_ref, vtc_ref, vt_ref, o_ref, s0, s1, p0, p1, ot):
    s_bufs = (s0, s1)
    p_bufs = (p0, p1)
    t_ctx = kc_ref.shape[1]
    n = k_ref.shape[1]
    tq = qt_ref.shape[2]

    def column_max(run, sc):
        part = jnp.max(sc.reshape(sc.shape[0] // SUBLANES, SUBLANES, tq), axis=0)
        return part if run is None else jnp.maximum(run, part)

    assert t_ctx % SUBLANES == 0 and n % ATTN_KEY_CHUNK == 0
    chunks = [(0, t_ctx, None)] + [(t_ctx + lo, ATTN_KEY_CHUNK, lo) for lo in range(0, n, ATTN_KEY_CHUNK)]

    def padded_query(h):
        qt = qt_ref[0, h * HEAD_DIM:(h + 1) * HEAD_DIM, :]
        zeros = jnp.zeros_like(qt)
        return jnp.concatenate([qt, zeros] if h // Q_PER_KV == 0 else [zeros, qt], axis=0)

    def score_chunk(h, qpad, run, chunk):
        row, size, lo = chunk
        keys = kc_ref[0] if lo is None else k_ref[0, lo:lo + size, :]
        sc = jnp.dot(keys, qpad, preferred_element_type=F32)
        s_bufs[h % 2][row:row + size, :] = sc
        return column_max(run, sc)

    def exp_chunk(h, m, chunk):
        row, size, _ = chunk
        p_bufs[h % 2][row:row + size, :] = jnp.exp2(s_bufs[h % 2][row:row + size, :] - m).astype(BF16)

    def value_chunk(h, acc, chunk):
        row, size, lo = chunk
        j = h // Q_PER_KV
        vt = vtc_ref[0, j] if lo is None else vt_ref[0, j, :, lo:lo + size]
        part = jnp.dot(vt, p_bufs[h % 2][row:row + size, :], preferred_element_type=F32)
        return part if acc is None else acc + part

    m = None
    for h in range(-1, N_Q_HEADS + 1):
        qpad = padded_query(h + 1) if h + 1 < N_Q_HEADS else None
        run = acc = None
        for chunk in chunks:
            if qpad is not None:
                run = score_chunk(h + 1, qpad, run, chunk)
            if 0 <= h - 1:
                acc = value_chunk(h - 1, acc, chunk)
            if 0 <= h < N_Q_HEADS:
                exp_chunk(h, m, chunk)
        if acc is not None:
            ot[(h - 1) * HEAD_DIM:h * HEAD_DIM, :] = acc[:HEAD_DIM, :] / acc[HEAD_DIM:HEAD_DIM + 1, :]
        m = jnp.max(run, axis=0, keepdims=True) if run is not None else None
    o_ref[0] = ot[...].T.astype(BF16)


def _attention(qt, k_ctx, k, vt_ctx, vt, tq=256):
    b, width, n = qt.shape
    t_ctx = k_ctx.shape[1]
    t_all = t_ctx + n
    return pl.pallas_call(
        _attn_body,
        grid=(b, n // tq),
        scratch_shapes=[pltpu.VMEM((t_all, tq), F32), pltpu.VMEM((t_all, tq), F32),
                        pltpu.VMEM((t_all, tq), BF16), pltpu.VMEM((t_all, tq), BF16),
                        pltpu.VMEM((width, tq), F32)],
        in_specs=[
            pl.BlockSpec((1, width, tq), lambda bi, i: (bi, 0, i)),
            pl.BlockSpec((1, t_ctx, KV_W), lambda bi, i: (bi, 0, 0)),
            pl.BlockSpec((1, n, KV_W), lambda bi, i: (bi, 0, 0)),
            pl.BlockSpec((1, N_KV_HEADS, 2 * HEAD_DIM, t_ctx), lambda bi, i: (bi, 0, 0, 0)),
            pl.BlockSpec((1, N_KV_HEADS, 2 * HEAD_DIM, n), lambda bi, i: (bi, 0, 0, 0)),
        ],
        out_specs=pl.BlockSpec((1, tq, width), lambda bi, i: (bi, i, 0)),
        out_shape=jax.ShapeDtypeStruct((b, n, width), BF16),
        compiler_params=_params("arbitrary", "arbitrary"),
        name="attention",
    )(qt, k_ctx, k, vt_ctx, vt)


CONV_ROWS = 64


def _conformer_rows(window, w_ref, cb_ref, lg_ref, lb_ref):
    tm = window.shape[0] - 2 * CONV_HALO
    ch = window.shape[1]
    win = CONV_ROWS + 2 * CONV_HALO
    off = CONV_HALO - CONF_K // 2
    cols = []
    for t in range(ch // LANES):
        lanes = slice(t * LANES, (t + 1) * LANES)
        blocks = []
        for r in range(tm // CONV_ROWS):
            x = window[r * CONV_ROWS:r * CONV_ROWS + win, lanes]
            acc = jnp.zeros((CONV_ROWS, LANES), F32)
            for sub in range(SUBLANES):
                xs = x if sub == 0 else pltpu.roll(x, win - sub, axis=0)
                for a in range(2 * CONV_HALO // SUBLANES):
                    k = a * SUBLANES + sub - off
                    if 0 <= k < CONF_K:
                        acc = acc + xs[a * SUBLANES:a * SUBLANES + CONV_ROWS, :] * w_ref[k:k + 1, lanes]
            blocks.append(acc)
        cols.append(jnp.concatenate(blocks, axis=0))
    y = jnp.concatenate(cols, axis=1) + cb_ref[...]
    mu = jnp.mean(y, axis=-1, keepdims=True)
    yc = y - mu
    var = jnp.mean(yc * yc, axis=-1, keepdims=True)
    return _silu(yc * lax.rsqrt(var + EPS) * lg_ref[...] + lb_ref[...])


def _residual_router(mix, h, mod_ref, g2_ref, wr_ref, h_out, hm_out, lg_out, rows=slice(None)):
    h1 = h + mod_ref[0, 2:3, :] * mix
    h_out[0, rows] = h1
    hm = _norm_mod(h1, g2_ref[...], mod_ref[0, 3:4, :], mod_ref[0, 4:5, :])
    hm_out[0, rows] = _rows_to_tiles(hm)
    lg_out[0, rows] = jnp.dot(hm.astype(BF16), wr_ref[...], preferred_element_type=F32)


def _outproj0_body(attn_ref, glu_ref, glup_ref, glun_ref, cw_ref, cb_ref, lg_ref, lb_ref, w_ref,
                   h_ref, mod_ref, g2_ref, wr_ref, h_out, hm_out, lg_out):
    i = pl.program_id(1)
    last = pl.num_programs(1) - 1
    before = jnp.where(i > 0, glup_ref[0], 0.0)
    after = jnp.where(i < last, glun_ref[0], 0.0)
    conf = _conformer_rows(jnp.concatenate([before, glu_ref[0], after], axis=0), cw_ref, cb_ref, lg_ref, lb_ref)
    a = jnp.concatenate([attn_ref[0], conf.astype(BF16)], axis=1)
    mix = jnp.dot(a, w_ref[...], preferred_element_type=F32)
    _residual_router(mix, h_ref[0], mod_ref, g2_ref, wr_ref, h_out, hm_out, lg_out)


def _tile_spec(tm, d):
    assert d == SUBLANES * LANES
    return pl.BlockSpec((1, tm, SUBLANES, LANES), lambda bi, i: (bi, i, 0, 0))


def _token_out_specs(b, n, d, tm):
    specs = [
        pl.BlockSpec((1, tm, d), lambda bi, i: (bi, i, 0)),
        _tile_spec(tm, d),
        pl.BlockSpec((1, tm, LANES), lambda bi, i: (bi, i, 0)),
    ]
    shapes = [
        jax.ShapeDtypeStruct((b, n, d), F32),
        jax.ShapeDtypeStruct((b, n, SUBLANES, d // SUBLANES), F32),
        jax.ShapeDtypeStruct((b, n, LANES), F32),
    ]
    return specs, shapes


def _outproj0(attn, glu, conv_w, conv_b, ln_g, ln_b, w_out, h, mods, g2, w_r, tm=512):
    b, n, d = h.shape
    ch = glu.shape[-1]
    per = tm // CONV_HALO
    n_halo = n // CONV_HALO
    vec = pl.BlockSpec((1, ch), lambda bi, i: (0, 0))
    out_specs, out_shape = _token_out_specs(b, n, d, tm)
    return pl.pallas_call(
        _outproj0_body,
        grid=(b, n // tm),
        in_specs=[
            pl.BlockSpec((1, tm, ATTN_W), lambda bi, i: (bi, i, 0)),
            pl.BlockSpec((1, tm, ch), lambda bi, i: (bi, i, 0)),
            pl.BlockSpec((1, CONV_HALO, ch), lambda bi, i: (bi, jnp.maximum(i * per - 1, 0), 0)),
            pl.BlockSpec((1, CONV_HALO, ch), lambda bi, i: (bi, jnp.minimum((i + 1) * per, n_halo - 1), 0)),
            pl.BlockSpec((CONF_K, ch), lambda bi, i: (0, 0)),
            vec, vec, vec,
            pl.BlockSpec((d, d), lambda bi, i: (0, 0)),
            pl.BlockSpec((1, tm, d), lambda bi, i: (bi, i, 0)),
            pl.BlockSpec((1, N_MOD, d), lambda bi, i: (bi, 0, 0)),
            pl.BlockSpec((1, d), lambda bi, i: (0, 0)),
            pl.BlockSpec((d, LANES), lambda bi, i: (0, 0)),
        ],
        out_specs=out_specs,
        out_shape=out_shape,
        compiler_params=_params("arbitrary", "arbitrary"),
        name="outproj0",
    )(attn, glu, glu, glu, conv_w, conv_b, ln_g, ln_b, w_out, h, mods, g2, w_r)


def _mixer1_body(h_ref, hp_ref, hn_ref, moe_ref, moep_ref, moen_ref, modp_ref, mod_ref, g1_ref, w_ref,
                 cw_ref, wo_ref, g2_ref, wr_ref, h_out, hm_out, lg_out):
    i = pl.program_id(1)
    last = pl.num_programs(1) - 1
    tm, d = h_ref.shape[1:]
    gate_prev = modp_ref[0, 5:6, :]

    def stream(h_blk, moe_blk, rows=slice(None)):
        return h_blk[0, rows] + gate_prev * _tiles_to_rows(moe_blk[0, rows])

    part = tm // MIXER_PARTS
    win = part + 2 * SUBLANES
    for p in range(MIXER_PARTS):
        rows = slice(p * part, (p + 1) * part)
        h = stream(h_ref, moe_ref, rows)
        above = (stream(hp_ref, moep_ref) if p == 0
                 else stream(h_ref, moe_ref, slice(p * part - SUBLANES, p * part)))
        below = (stream(hn_ref, moen_ref) if p == MIXER_PARTS - 1
                 else stream(h_ref, moe_ref, slice((p + 1) * part, (p + 1) * part + SUBLANES)))
        hn = _norm_mod(jnp.concatenate([above, h, below], axis=0), g1_ref[...], mod_ref[0, 0:1, :], mod_ref[0, 1:2, :])
        proj = jnp.dot(hn.astype(BF16), w_ref[...], preferred_element_type=F32)
        z = proj[:, d:2 * d] * proj[:, 2 * d:]
        r = lax.broadcasted_iota(jnp.int32, z.shape, 0)
        if p == 0:
            z = jnp.where((r < SUBLANES) & (i == 0), 0.0, z)
        if p == MIXER_PARTS - 1:
            z = jnp.where((r >= part + SUBLANES) & (i == last), 0.0, z)
        y = (pltpu.roll(z, 1, axis=0) * cw_ref[0:1, :] + z * cw_ref[1:2, :]
             + pltpu.roll(z, win - 1, axis=0) * cw_ref[2:3, :])[SUBLANES:SUBLANES + part]
        gated = proj[SUBLANES:SUBLANES + part, :d] * y
        mix = jnp.dot(gated.astype(BF16), wo_ref[...], preferred_element_type=F32)
        _residual_router(mix, h, mod_ref, g2_ref, wr_ref, h_out, hm_out, lg_out, rows)


def _mixer1(h, moe, mods_prev, mods, g1, w_in, conv_w, w_out, g2, w_r, tm=512):
    b, n, d = h.shape
    per = tm // SUBLANES
    n8 = n // SUBLANES
    before = lambda bi, i: (bi, jnp.maximum(i * per - 1, 0), 0)
    after = lambda bi, i: (bi, jnp.minimum((i + 1) * per, n8 - 1), 0)
    tok = pl.BlockSpec((1, tm, d), lambda bi, i: (bi, i, 0))
    modspec = pl.BlockSpec((1, N_MOD, d), lambda bi, i: (bi, 0, 0))
    vec = pl.BlockSpec((1, d), lambda bi, i: (0, 0))
    out_specs, out_shape = _token_out_specs(b, n, d, tm)
    return pl.pallas_call(
        _mixer1_body,
        grid=(b, n // tm),
        in_specs=[
            tok,
            pl.BlockSpec((1, SUBLANES, d), before),
            pl.BlockSpec((1, SUBLANES, d), after),
            _tile_spec(tm, d),
            pl.BlockSpec((1, SUBLANES, SUBLANES, LANES), lambda bi, i: before(bi, i) + (0,)),
            pl.BlockSpec((1, SUBLANES, SUBLANES, LANES), lambda bi, i: after(bi, i) + (0,)),
            modspec, modspec, vec,
            pl.BlockSpec((d, 3 * d), lambda bi, i: (0, 0)),
            pl.BlockSpec((SC_K, d), lambda bi, i: (0, 0)),
            pl.BlockSpec((d, d), lambda bi, i: (0, 0)),
            vec,
            pl.BlockSpec((d, LANES), lambda bi, i: (0, 0)),
        ],
        out_specs=out_specs,
        out_shape=out_shape,
        compiler_params=_params("arbitrary", "arbitrary"),
        name="mixer1",
    )(h, h, h, moe, moe, moe, mods_prev, mods, g1, w_in, conv_w, w_out, g2, w_r)


def _stack_chunks(x):
    n = x.shape[1]
    return jnp.concatenate([x[:, c * ROUTER_CHUNK:(c + 1) * ROUTER_CHUNK] for c in range(n // ROUTER_CHUNK)], axis=0)


def _exclusive_rank(flags, utri, chunk_lt):
    incl = jnp.dot(flags.astype(BF16), utri, preferred_element_type=F32)
    tot = jnp.broadcast_to(incl[:, ROUTER_CHUNK - 1:ROUTER_CHUNK], incl.shape)
    base = jnp.dot(chunk_lt, tot.astype(BF16), preferred_element_type=F32)
    return incl - flags + base


def _router_body(lg_ref, utri_ref, lt_ref, blk_ref, idx_ref, aff_ref, loc_scr, tot_scr, end_scr, *, cap):
    n = lg_ref.shape[1]
    n_chunks = n // ROUTER_CHUNK
    lg = lg_ref[0]
    lane = lax.broadcasted_iota(jnp.int32, lg.shape, 1)
    valid = lane < N_EXPERTS
    x = jnp.where(valid, lg, -jnp.inf)
    ex = jnp.where(valid, jnp.exp(x - jnp.max(x, axis=-1, keepdims=True)), 0.0)
    aff = ex / jnp.sum(ex, axis=-1, keepdims=True)

    aff_t = aff.T[:N_EXPERTS, :]
    aff_ref[0] = aff_t

    def count_ge(t):
        return jnp.sum(jnp.where(aff_t >= t, 1.0, 0.0), axis=-1, keepdims=True)

    def bit_step(i, bits):
        cand = bits | jnp.left_shift(jnp.int32(1), 30 - i)
        return jnp.where(count_ge(lax.bitcast_convert_type(cand, F32)) >= cap, cand, bits)

    bits = lax.fori_loop(0, 31, bit_step, jnp.zeros((N_EXPERTS, 1), jnp.int32))

    def refine(i, lo_hi):
        lo, hi = lo_hi
        mid = (lo + hi) * 0.5
        ok = count_ge(mid) >= cap
        return jnp.where(ok, mid, lo), jnp.where(ok, hi, mid)

    thr, _ = lax.fori_loop(0, REFINE_STEPS, refine,
                           (lax.bitcast_convert_type(bits, F32), lax.bitcast_convert_type(bits + 1, F32)))
    gt = jnp.where(aff_t > thr, 1.0, 0.0)
    eq = jnp.where(aff_t == thr, 1.0, 0.0)
    need = cap - jnp.sum(gt, axis=-1, keepdims=True)
    utri = utri_ref[...]
    chunk_lt = lt_ref[...]
    gt_s = _stack_chunks(gt)
    eq_s = _stack_chunks(eq)
    need_s = jnp.concatenate([need] * n_chunks, axis=0)
    sel = jnp.where((gt_s > 0) | ((eq_s > 0) & (_exclusive_rank(eq_s, utri, chunk_lt) < need_s)), 1.0, 0.0)

    loc_scr[...] = jnp.dot(sel.astype(BF16), utri, preferred_element_type=F32).astype(BF16)
    sel_t = jnp.concatenate([sel[c * N_EXPERTS:(c + 1) * N_EXPERTS, :] for c in range(n_chunks)], axis=1)
    tot = jnp.dot(sel_t.astype(BF16), blk_ref[...], preferred_element_type=F32)
    tot_scr[...] = tot
    end_scr[...] = jnp.dot(tot.astype(BF16), utri, preferred_element_type=F32)

    slot = lax.broadcasted_iota(jnp.int32, (cap, 1), 0).astype(F32)
    row_id = lax.broadcasted_iota(jnp.int32, (cap, n_chunks * N_EXPERTS), 1)

    def one_expert(e, carry):
        ends = end_scr[pl.ds(e, 1), :]
        before = ends <= slot
        chunk = jnp.sum(jnp.where(before, 1.0, 0.0), axis=-1, keepdims=True)
        base = jnp.sum(jnp.where(before, tot_scr[pl.ds(e, 1), :], 0.0), axis=-1, keepdims=True)
        pick = jnp.where(row_id == chunk.astype(jnp.int32) * N_EXPERTS + e, 1.0, 0.0).astype(BF16)
        counts = jnp.dot(pick, loc_scr[...], preferred_element_type=F32)
        inside = jnp.sum(jnp.where(counts <= slot - base, 1.0, 0.0), axis=-1, keepdims=True)
        idx_ref[0, e] = (chunk * ROUTER_CHUNK + inside).astype(jnp.int32)
        return carry

    lax.fori_loop(0, N_EXPERTS, one_expert, 0)


def _router(logits, utri, chunk_lt, chunk_of_token, cap):
    b, n, _ = logits.shape
    rows = (n // ROUTER_CHUNK) * N_EXPERTS
    assert n // ROUTER_CHUNK <= LANES
    return pl.pallas_call(
        functools.partial(_router_body, cap=cap),
        grid=(b,),
        in_specs=[
            pl.BlockSpec((1, n, LANES), lambda bi: (bi, 0, 0)),
            pl.BlockSpec((ROUTER_CHUNK, ROUTER_CHUNK), lambda bi: (0, 0)),
            pl.BlockSpec((rows, rows), lambda bi: (0, 0)),
            pl.BlockSpec((n, LANES), lambda bi: (0, 0)),
        ],
        out_specs=[
            pl.BlockSpec((1, N_EXPERTS, cap, 1), lambda bi: (bi, 0, 0, 0)),
            pl.BlockSpec((1, N_EXPERTS, n), lambda bi: (bi, 0, 0)),
        ],
        out_shape=[
            jax.ShapeDtypeStruct((b, N_EXPERTS, cap, 1), jnp.int32),
            jax.ShapeDtypeStruct((b, N_EXPERTS, n), F32),
        ],
        scratch_shapes=[pltpu.VMEM((rows, ROUTER_CHUNK), BF16), pltpu.VMEM((N_EXPERTS, LANES), F32),
                        pltpu.VMEM((N_EXPERTS, LANES), F32)],
        compiler_params=_params("arbitrary"),
        name="router",
    )(logits, utri, chunk_lt, chunk_of_token)


GATHER_UNROLL = 32
MOE_EXPERTS_PER_STEP = 2


def _gather_body(idx_ref, hm_ref, o_ref):
    cap = o_ref.shape[2]

    for k in range(o_ref.shape[1]):
        def group(g, carry, k=k):
            base = pl.multiple_of(g * GATHER_UNROLL, GATHER_UNROLL)
            halves = []
            for half in range(GATHER_UNROLL // SUBLANES):
                tiles = [hm_ref[0, idx_ref[k, 0, base + half * SUBLANES + u]][None] for u in range(SUBLANES)]
                halves.append(_tile_transpose8(tiles))
            rows = jnp.concatenate([jnp.concatenate([h[a][0] for h in halves], axis=0) for a in range(SUBLANES)],
                                   axis=1)
            o_ref[0, k, pl.ds(base, GATHER_UNROLL), :] = rows.astype(BF16)
            return carry

        lax.fori_loop(0, cap // GATHER_UNROLL, group, 0)


def _gather(idx, hm):
    b, n = hm.shape[:2]
    cap = idx.shape[-1]
    d = SUBLANES * LANES
    eps = MOE_EXPERTS_PER_STEP
    steps = N_EXPERTS // eps
    return pl.pallas_call(
        _gather_body,
        grid=(b, steps),
        in_specs=[
            pl.BlockSpec((eps, 1, cap), lambda bi, e: (bi * steps + e, 0, 0), memory_space=pltpu.SMEM),
            pl.BlockSpec((1, n, SUBLANES, LANES), lambda bi, e: (bi, 0, 0, 0)),
        ],
        out_specs=pl.BlockSpec((1, eps, cap, d), lambda bi, e: (bi, e, 0, 0)),
        out_shape=jax.ShapeDtypeStruct((b, N_EXPERTS, cap, d), BF16),
        compiler_params=_params("arbitrary", "arbitrary"),
        name="moe_gather",
    )(idx, hm)


FFN_SAMPLES_PER_STEP = 2


def _ffn_body(x_ref, wg_ref, wu_ref, wd_ref, o_ref):
    cap = x_ref.shape[2]
    x = jnp.concatenate([x_ref[s, 0] for s in range(x_ref.shape[0])], axis=0)
    hg = jnp.dot(x, wg_ref[0, 0].astype(BF16), preferred_element_type=F32)
    hu = jnp.dot(x, wu_ref[0, 0].astype(BF16), preferred_element_type=F32)
    y = jnp.dot((_silu(hg) * hu).astype(BF16), wd_ref[0, 0].astype(BF16), preferred_element_type=F32)
    for s in range(x_ref.shape[0]):
        o_ref[s, 0] = y[s * cap:(s + 1) * cap]


def _expert_ffn(xs, w_gate, w_up, w_down, layer):
    b, n_e, cap, d = xs.shape
    f = w_gate.shape[-1]
    sps = FFN_SAMPLES_PER_STEP if b % FFN_SAMPLES_PER_STEP == 0 else 1
    rows = pl.BlockSpec((sps, 1, cap, d), lambda e, bi: (bi, e, 0, 0))
    return pl.pallas_call(
        _ffn_body,
        grid=(n_e, b // sps),
        in_specs=[
            rows,
            pl.BlockSpec((1, 1, d, f), lambda e, bi: (layer, e, 0, 0)),
            pl.BlockSpec((1, 1, d, f), lambda e, bi: (layer, e, 0, 0)),
            pl.BlockSpec((1, 1, f, d), lambda e, bi: (layer, e, 0, 0)),
        ],
        out_specs=rows,
        out_shape=jax.ShapeDtypeStruct(xs.shape, F32),
        compiler_params=_params("arbitrary", "arbitrary"),
        name="moe_ffn",
    )(xs, w_gate, w_up, w_down)


SCATTER_UNROLL = 16


def _scatter_body(idx_ref, aff_ref, ys_ref, o_ref):
    cap = ys_ref.shape[2]

    @pl.when(pl.program_id(1) == 0)
    def _():
        o_ref[...] = jnp.zeros(o_ref.shape, F32)

    for k in range(ys_ref.shape[1]):
        def group(g, carry, k=k):
            base = pl.multiple_of(g * SCATTER_UNROLL, SCATTER_UNROLL)
            tiles = []
            for part in range(SCATTER_UNROLL // SUBLANES):
                y = ys_ref[0, k, pl.ds(base + part * SUBLANES, SUBLANES), :]
                tiles += _tile_transpose8([y[:, a * LANES:(a + 1) * LANES][None] for a in range(SUBLANES)])
            rows = [idx_ref[k, 0, base + u] for u in range(SCATTER_UNROLL)]
            sums = [o_ref[0, rows[u]] + tiles[u][0] * aff_ref[k, 0, rows[u]] for u in range(SCATTER_UNROLL)]
            for u in range(SCATTER_UNROLL):
                o_ref[0, rows[u]] = sums[u]
            return carry

        lax.fori_loop(0, cap // SCATTER_UNROLL, group, 0)


def _scatter(idx, aff, ys, n):
    b, n_e, cap, d = ys.shape
    assert SCATTER_UNROLL % SUBLANES == 0 and cap % SCATTER_UNROLL == 0 and d == SUBLANES * LANES
    eps = MOE_EXPERTS_PER_STEP
    steps = n_e // eps
    return pl.pallas_call(
        _scatter_body,
        grid=(b, steps),
        in_specs=[
            pl.BlockSpec((eps, 1, cap), lambda bi, e: (bi * steps + e, 0, 0), memory_space=pltpu.SMEM),
            pl.BlockSpec((eps, 1, n), lambda bi, e: (bi * steps + e, 0, 0), memory_space=pltpu.SMEM),
            pl.BlockSpec((1, eps, cap, d), lambda bi, e: (bi, e, 0, 0)),
        ],
        out_specs=pl.BlockSpec((1, n, SUBLANES, LANES), lambda bi, e: (bi, 0, 0, 0)),
        out_shape=jax.ShapeDtypeStruct((b, n, SUBLANES, LANES), F32),
        compiler_params=_params("arbitrary", "arbitrary"),
        name="moe_scatter",
    )(idx, aff, ys)


def _ec_moe(hm, logits, w_gate, w_up, w_down, layer, tables):
    b, n = hm.shape[:2]
    cap = max(1, EC_CAPACITY * n // N_EXPERTS)
    idx, aff = _router(logits, *tables, cap)
    idx = idx.reshape(b * N_EXPERTS, 1, cap)
    xs = _gather(idx, hm)
    ys = _expert_ffn(xs, w_gate, w_up, w_down, layer)
    return _scatter(idx, aff.reshape(b * N_EXPERTS, 1, n), ys, n)


def _final_body(h_ref, moe_ref, mod_ref, g_ref, o_ref):
    h = h_ref[0] + mod_ref[0, 5:6, :] * _tiles_to_rows(moe_ref[0])
    o_ref[0] = h * lax.rsqrt(jnp.mean(h * h, axis=-1, keepdims=True) + EPS) * g_ref[...]


def _final(h, moe, mods, g, tm=512):
    b, n, d = h.shape
    tok = pl.BlockSpec((1, tm, d), lambda bi, i: (bi, i, 0))
    return pl.pallas_call(
        _final_body,
        grid=(b, n // tm),
        in_specs=[tok, _tile_spec(tm, d), pl.BlockSpec((1, N_MOD, d), lambda bi, i: (bi, 0, 0)),
                  pl.BlockSpec((1, d), lambda bi, i: (0, 0))],
        out_specs=tok,
        out_shape=jax.ShapeDtypeStruct((b, n, d), F32),
        compiler_params=_params("arbitrary", "arbitrary"),
        name="final_norm",
    )(h, moe, mods, g)


def _rope_tables(n):
    rows = n // GRID_W
    row = np.repeat(np.arange(rows, dtype=np.float64), GRID_W)
    col = np.tile(np.arange(GRID_W, dtype=np.float64), rows)
    axis_dim = HEAD_DIM // 2
    inv = ROPE_THETA ** (-np.arange(0, axis_dim, 2, dtype=np.float64) / axis_dim)
    ang = np.concatenate([row[:, None] * inv, col[:, None] * inv], axis=-1)
    cos = np.cos(ang)
    sin = np.sin(ang)
    cos64 = np.concatenate([cos, cos], axis=-1)
    sin64 = np.concatenate([-sin, sin], axis=-1)
    reps = LANES // HEAD_DIM
    return jnp.asarray(np.tile(cos64, (1, reps)), F32), jnp.asarray(np.tile(sin64, (1, reps)), F32)


def _router_tables(n):
    r = np.arange(ROUTER_CHUNK)
    utri = r[:, None] <= r[None, :]
    rows = np.arange((n // ROUTER_CHUNK) * N_EXPERTS)
    same_e = (rows[:, None] % N_EXPERTS) == (rows[None, :] % N_EXPERTS)
    earlier = (rows[None, :] // N_EXPERTS) < (rows[:, None] // N_EXPERTS)
    chunk_of_token = (np.arange(n)[:, None] // ROUTER_CHUNK) == np.arange(LANES)[None, :]
    return tuple(jnp.asarray(t.astype(np.float32), BF16) for t in (utri, same_e & earlier, chunk_of_token))


def _head_ones():
    r = np.arange(MXU_DIM) // HEAD_DIM
    return jnp.asarray((r[:, None] == r[None, :]).astype(np.float32), BF16)


def kernel(x, c, ctx, c_ctx, ada_w, ada_b, norm1_g, norm2_g, ev_w_in, ev_q_g, ev_k_g, ev_conv_w, ev_conv_b,
           ev_ln_g, ev_ln_b, ev_w_out, sc_w_in, sc_conv_w, sc_w_out, moe_w_r, moe_w_gate, moe_w_up,
           moe_w_down, final_g):
    b, n, d = x.shape
    depth = ada_w.shape[0]
    assert depth == 2 and b < SUBLANES and n % ROUTER_CHUNK == 0

    cos2, sin2 = _rope_tables(n)
    tables = _router_tables(n)
    ones_bd = _head_ones()

    cvecs = jnp.zeros((SUBLANES, d), F32).at[:b].set(c).at[b].set(c_ctx)
    mods = _ada_mod(cvecs, ada_w, ada_b).reshape(depth, SUBLANES, N_MOD, d)
    w_r = jnp.pad(moe_w_r, ((0, 0), (0, 0), (0, LANES - N_EXPERTS))).astype(BF16)

    w_in0 = ev_w_in[0].astype(BF16)
    qg = jnp.tile(ev_q_g[0], N_Q_HEADS)[None, :]
    kg = jnp.tile(ev_k_g[0], N_KV_HEADS)[None, :]
    qt, k, vt, glu = _inproj0(x, mods[0], norm1_g[0:1], w_in0, qg, kg, cos2, sin2, ones_bd)
    k_ctx, vt_ctx = _ctxkv(ctx, mods[0], b, norm1_g[0:1], w_in0[:, ATTN_W:ATTN_W + 2 * KV_W], kg, ones_bd)
    attn = _attention(qt, k_ctx, k, vt_ctx, vt)
    h, hm, logits = _outproj0(attn, glu, ev_conv_w[0], ev_conv_b[0:1], ev_ln_g[0:1], ev_ln_b[0:1],
                              ev_w_out[0].astype(BF16), x, mods[0], norm2_g[0:1], w_r[0])
    moe0 = _ec_moe(hm, logits, moe_w_gate, moe_w_up, moe_w_down, 0, tables)

    h, hm, logits = _mixer1(h, moe0, mods[0], mods[1], norm1_g[1:2], sc_w_in[0].astype(BF16), sc_conv_w[0],
                            sc_w_out[0].astype(BF16), norm2_g[1:2], w_r[1])
    moe1 = _ec_moe(hm, logits, moe_w_gate, moe_w_up, moe_w_down, 1, tables)

    return _final(h, moe1, mods[1], final_g[None, :])
```

```python
import functools

import jax
import jax.numpy as jnp
import numpy as np
from jax import lax
from jax.experimental import pallas as pl
from jax.experimental.pallas import tpu as pltpu

F32 = jnp.float32
BF16 = jnp.bfloat16

HEAD_DIM = 64
N_Q_HEADS = 8
N_KV_HEADS = 2
GRID_W = 64
ROPE_THETA = 10000.0
CONF_K = 31
SC_K = 3
N_EXPERTS = 16
EC_CAPACITY = 2
N_MOD = 6
EPS = 1e-6

ATTN_W = N_Q_HEADS * HEAD_DIM
KV_W = N_KV_HEADS * HEAD_DIM
Q_PER_KV = N_Q_HEADS // N_KV_HEADS
QK_SCALE = HEAD_DIM ** -0.5 * 1.4426950408889634

LANES = 128
SUBLANES = 8
MXU_DIM = 256
VMEM_LIMIT_BYTES = 60000 * 1024

CONV_HALO = 16
ROUTER_CHUNK = LANES
REFINE_STEPS = 24
INPROJ_PARTS = 2
MIXER_PARTS = 1


def _params(*sem):
    return pltpu.CompilerParams(dimension_semantics=sem, vmem_limit_bytes=VMEM_LIMIT_BYTES)


def _norm_mod(x, g, shift, scale):
    ms = jnp.mean(x * x, axis=-1, keepdims=True)
    y = x * lax.rsqrt(ms + EPS) * g
    return y * (1.0 + scale) + shift


def _head_sumsq(x, ones_blockdiag):
    return jnp.dot((x * x).astype(BF16), ones_blockdiag, preferred_element_type=F32)


def _swap_half(x):
    w = x.shape[-1]
    lane = lax.broadcasted_iota(jnp.int32, x.shape, 1)
    first = (lane % HEAD_DIM) < (HEAD_DIM // 2)
    return jnp.where(first, pltpu.roll(x, w - HEAD_DIM // 2, axis=1), pltpu.roll(x, HEAD_DIM // 2, axis=1))


def _silu(x):
    return x * jax.nn.sigmoid(x)


def _tile_transpose8(vs):
    sub = lax.broadcasted_iota(jnp.int32, vs[0].shape, 1)
    for d in (4, 2, 1):
        keep = (sub & d) == 0
        out = list(vs)
        for i in range(SUBLANES):
            if i & d == 0:
                a, b = vs[i], vs[i + d]
                out[i] = jnp.where(keep, a, pltpu.roll(b, d, axis=1))
                out[i + d] = jnp.where(keep, pltpu.roll(a, SUBLANES - d, axis=1), b)
        vs = out
    return vs


def _rows_to_tiles(x):
    r = x.shape[0]
    vs = [x[:, a * LANES:(a + 1) * LANES].reshape(r // SUBLANES, SUBLANES, LANES) for a in range(SUBLANES)]
    return jnp.stack(_tile_transpose8(vs), axis=1).reshape(r, SUBLANES, LANES)


def _tiles_to_rows(x3):
    r = x3.shape[0]
    x4 = x3.reshape(r // SUBLANES, SUBLANES, SUBLANES, LANES)
    vs = _tile_transpose8([x4[:, j] for j in range(SUBLANES)])
    return jnp.concatenate([v.reshape(r, LANES) for v in vs], axis=1)


def _store_vt_with_ones(vt_ref, cols, v):
    vt = v.T
    ones = jnp.ones((HEAD_DIM, v.shape[0]), F32)
    for j in range(N_KV_HEADS):
        vt_ref[0, j, :, cols] = jnp.concatenate([vt[j * HEAD_DIM:(j + 1) * HEAD_DIM, :], ones], axis=0).astype(BF16)


def _ada_body(c_ref, w_ref, b_ref, o_ref):
    s = _silu(c_ref[...]).astype(BF16)
    o_ref[0] = jnp.dot(s, w_ref[0].astype(BF16), preferred_element_type=F32) + b_ref[0]


def _ada_mod(cvecs, ada_w, ada_b):
    n_layers, d, n_out = ada_w.shape
    tn = n_out // 4
    return pl.pallas_call(
        _ada_body,
        grid=(n_layers, n_out // tn),
        in_specs=[
            pl.BlockSpec((SUBLANES, d), lambda l, j: (0, 0)),
            pl.BlockSpec((1, d, tn), lambda l, j: (l, 0, j)),
            pl.BlockSpec((1, 1, tn), lambda l, j: (l, 0, j)),
        ],
        out_specs=pl.BlockSpec((1, SUBLANES, tn), lambda l, j: (l, 0, j)),
        out_shape=jax.ShapeDtypeStruct((n_layers, SUBLANES, n_out), F32),
        compiler_params=_params("arbitrary", "arbitrary"),
        name="ada_mod",
    )(cvecs, ada_w, ada_b.reshape(n_layers, 1, n_out))


def _inproj0_body(h_ref, mod_ref, g1_ref, w_ref, qg_ref, kg_ref, cos_ref, sin_ref, ones_ref,
                  qt_ref, k_ref, vt_ref, glu_ref):
    ones = ones_ref[...]
    tm = h_ref.shape[1]
    part = tm // INPROJ_PARTS
    for p in range(INPROJ_PARTS):
        rows = slice(p * part, (p + 1) * part)
        hn = _norm_mod(h_ref[0, rows, :], g1_ref[...], mod_ref[0, 0:1, :], mod_ref[0, 1:2, :])
        proj = jnp.dot(hn.astype(BF16), w_ref[...], preferred_element_type=F32)
        cos2 = cos_ref[rows, :]
        sin2 = sin_ref[rows, :]

        q = proj[:, :ATTN_W]
        ssq = jnp.concatenate([_head_sumsq(q[:, :MXU_DIM], ones), _head_sumsq(q[:, MXU_DIM:], ones)], axis=1)
        qn = q * lax.rsqrt(ssq * (1.0 / HEAD_DIM) + EPS) * qg_ref[...]
        cos = jnp.concatenate([cos2] * (ATTN_W // LANES), axis=1)
        sin = jnp.concatenate([sin2] * (ATTN_W // LANES), axis=1)
        qr = (qn * cos + _swap_half(qn) * sin) * QK_SCALE
        qt_ref[0, :, rows] = qr.T.astype(BF16)

        k = proj[:, ATTN_W:ATTN_W + KV_W]
        kn = k * lax.rsqrt(_head_sumsq(k, ones[:KV_W, :KV_W]) * (1.0 / HEAD_DIM) + EPS) * kg_ref[...]
        k_ref[0, rows, :] = (kn * cos2 + _swap_half(kn) * sin2).astype(BF16)
        _store_vt_with_ones(vt_ref, rows, proj[:, ATTN_W + KV_W:ATTN_W + 2 * KV_W])

        c0 = ATTN_W + 2 * KV_W
        cc = (proj.shape[1] - c0) // 2
        glu_ref[0, rows, :] = proj[:, c0:c0 + cc] * jax.nn.sigmoid(proj[:, c0 + cc:])


def _inproj0(h, mods, g1, w_in, qg, kg, cos2, sin2, ones_bd, tm=512):
    b, n, d = h.shape
    n_in = w_in.shape[1]
    conv_ch = (n_in - ATTN_W - 2 * KV_W) // 2
    return pl.pallas_call(
        _inproj0_body,
        grid=(b, n // tm),
        in_specs=[
            pl.BlockSpec((1, tm, d), lambda bi, i: (bi, i, 0)),
            pl.BlockSpec((1, N_MOD, d), lambda bi, i: (bi, 0, 0)),
            pl.BlockSpec((1, d), lambda bi, i: (0, 0)),
            pl.BlockSpec((d, n_in), lambda bi, i: (0, 0)),
            pl.BlockSpec((1, ATTN_W), lambda bi, i: (0, 0)),
            pl.BlockSpec((1, KV_W), lambda bi, i: (0, 0)),
            pl.BlockSpec((tm, LANES), lambda bi, i: (i, 0)),
            pl.BlockSpec((tm, LANES), lambda bi, i: (i, 0)),
            pl.BlockSpec((MXU_DIM, MXU_DIM), lambda bi, i: (0, 0)),
        ],
        out_specs=[
            pl.BlockSpec((1, ATTN_W, tm), lambda bi, i: (bi, 0, i)),
            pl.BlockSpec((1, tm, KV_W), lambda bi, i: (bi, i, 0)),
            pl.BlockSpec((1, N_KV_HEADS, 2 * HEAD_DIM, tm), lambda bi, i: (bi, 0, 0, i)),
            pl.BlockSpec((1, tm, conv_ch), lambda bi, i: (bi, i, 0)),
        ],
        out_shape=[
            jax.ShapeDtypeStruct((b, ATTN_W, n), BF16),
            jax.ShapeDtypeStruct((b, n, KV_W), BF16),
            jax.ShapeDtypeStruct((b, N_KV_HEADS, 2 * HEAD_DIM, n), BF16),
            jax.ShapeDtypeStruct((b, n, conv_ch), F32),
        ],
        compiler_params=_params("arbitrary", "arbitrary"),
        name="inproj0",
    )(h, mods, g1, w_in, qg, kg, cos2, sin2, ones_bd)


def _ctxkv_body(x_ref, mod_ref, g1_ref, w_ref, kg_ref, ones_ref, k_ref, vt_ref):
    hn = _norm_mod(x_ref[0], g1_ref[...], mod_ref[0, 0:1, :], mod_ref[0, 1:2, :])
    proj = jnp.dot(hn.astype(BF16), w_ref[...], preferred_element_type=F32)
    k = proj[:, :KV_W]
    kn = k * lax.rsqrt(_head_sumsq(k, ones_ref[...][:KV_W, :KV_W]) * (1.0 / HEAD_DIM) + EPS) * kg_ref[...]
    k_ref[0] = kn.astype(BF16)
    _store_vt_with_ones(vt_ref, slice(None), proj[:, KV_W:])


def _ctxkv(ctx, mods, ctx_row, g1, w_kv, kg, ones_bd):
    b, t, d = ctx.shape
    return pl.pallas_call(
        _ctxkv_body,
        grid=(b,),
        in_specs=[
            pl.BlockSpec((1, t, d), lambda bi: (bi, 0, 0)),
            pl.BlockSpec((1, N_MOD, d), lambda bi: (ctx_row, 0, 0)),
            pl.BlockSpec((1, d), lambda bi: (0, 0)),
            pl.BlockSpec((d, 2 * KV_W), lambda bi: (0, 0)),
            pl.BlockSpec((1, KV_W), lambda bi: (0, 0)),
            pl.BlockSpec((MXU_DIM, MXU_DIM), lambda bi: (0, 0)),
        ],
        out_specs=[
            pl.BlockSpec((1, t, KV_W), lambda bi: (bi, 0, 0)),
            pl.BlockSpec((1, N_KV_HEADS, 2 * HEAD_DIM, t), lambda bi: (bi, 0, 0, 0)),
        ],
        out_shape=[
            jax.ShapeDtypeStruct((b, t, KV_W), BF16),
            jax.ShapeDtypeStruct((b, N_KV_HEADS, 2 * HEAD_DIM, t), BF16),
        ],
        compiler_params=_params("arbitrary"),
        name="ctx_kv",
    )(ctx, mods, g1, w_kv, kg, ones_bd)


ATTN_KEY_CHUNK = 512


def _attn_body(qt_ref, kc_ref, k_ref, vtc_ref, vt_ref, o_ref, s0, s1, p0, p1, ot):
    s_bufs = (s0, s1)
    p_bufs = (p0, p1)
    t_ctx = kc_ref.shape[1]
    n = k_ref.shape[1]
    tq = qt_ref.shape[2]

    def column_max(run, sc):
        part = jnp.max(sc.reshape(sc.shape[0] // SUBLANES, SUBLANES, tq), axis=0)
        return part if run is None else jnp.maximum(run, part)

    assert t_ctx % SUBLANES == 0 and n % ATTN_KEY_CHUNK == 0
    chunks = [(0, t_ctx, None)] + [(t_ctx + lo, ATTN_KEY_CHUNK, lo) for lo in range(0, n, ATTN_KEY_CHUNK)]

    def padded_query(h):
        qt = qt_ref[0, h * HEAD_DIM:(h + 1) * HEAD_DIM, :]
        zeros = jnp.zeros_like(qt)
        return jnp.concatenate([qt, zeros] if h // Q_PER_KV == 0 else [zeros, qt], axis=0)

    def score_chunk(h, qpad, run, chunk):
        row, size, lo = chunk
        keys = kc_ref[0] if lo is None else k_ref[0, lo:lo + size, :]
        sc = jnp.dot(keys, qpad, preferred_element_type=F32)
        s_bufs[h % 2][row:row + size, :] = sc
        return column_max(run, sc)

    def exp_chunk(h, m, chunk):
        row, size, _ = chunk
        p_bufs[h % 2][row:row + size, :] = jnp.exp2(s_bufs[h % 2][row:row + size, :] - m).astype(BF16)

    def value_chunk(h, acc, chunk):
        row, size, lo = chunk
        j = h // Q_PER_KV
        vt = vtc_ref[0, j] if lo is None else vt_ref[0, j, :, lo:lo + size]
        part = jnp.dot(vt, p_bufs[h % 2][row:row + size, :], preferred_element_type=F32)
        return part if acc is None else acc + part

    m = None
    for h in range(-1, N_Q_HEADS + 1):
        qpad = padded_query(h + 1) if h + 1 < N_Q_HEADS else None
        run = acc = None
        for chunk in chunks:
            if qpad is not None:
                run = score_chunk(h + 1, qpad, run, chunk)
            if 0 <= h - 1:
                acc = value_chunk(h - 1, acc, chunk)
            if 0 <= h < N_Q_HEADS:
                exp_chunk(h, m, chunk)
        if acc is not None:
            ot[(h - 1) * HEAD_DIM:h * HEAD_DIM, :] = acc[:HEAD_DIM, :] / acc[HEAD_DIM:HEAD_DIM + 1, :]
        m = jnp.max(run, axis=0, keepdims=True) if run is not None else None
    o_ref[0] = ot[...].T.astype(BF16)


def _attention(qt, k_ctx, k, vt_ctx, vt, tq=512):
    b, width, n = qt.shape
    t_ctx = k_ctx.shape[1]
    t_all = t_ctx + n
    return pl.pallas_call(
        _attn_body,
        grid=(b, n // tq),
        scratch_shapes=[pltpu.VMEM((t_all, tq), F32), pltpu.VMEM((t_all, tq), F32),
                        pltpu.VMEM((t_all, tq), BF16), pltpu.VMEM((t_all, tq), BF16),
                        pltpu.VMEM((width, tq), F32)],
        in_specs=[
            pl.BlockSpec((1, width, tq), lambda bi, i: (bi, 0, i)),
            pl.BlockSpec((1, t_ctx, KV_W), lambda bi, i: (bi, 0, 0)),
            pl.BlockSpec((1, n, KV_W), lambda bi, i: (bi, 0, 0)),
            pl.BlockSpec((1, N_KV_HEADS, 2 * HEAD_DIM, t_ctx), lambda bi, i: (bi, 0, 0, 0)),
            pl.BlockSpec((1, N_KV_HEADS, 2 * HEAD_DIM, n), lambda bi, i: (bi, 0, 0, 0)),
        ],
        out_specs=pl.BlockSpec((1, tq, width), lambda bi, i: (bi, i, 0)),
        out_shape=jax.ShapeDtypeStruct((b, n, width), BF16),
        compiler_params=_params("arbitrary", "arbitrary"),
        name="attention",
    )(qt, k_ctx, k, vt_ctx, vt)


CONV_ROWS = 64


def _conformer_rows(window, w_ref, cb_ref, lg_ref, lb_ref):
    tm = window.shape[0] - 2 * CONV_HALO
    ch = window.shape[1]
    win = CONV_ROWS + 2 * CONV_HALO
    off = CONV_HALO - CONF_K // 2
    cols = []
    for t in range(ch // LANES):
        lanes = slice(t * LANES, (t + 1) * LANES)
        blocks = []
        for r in range(tm // CONV_ROWS):
            x = window[r * CONV_ROWS:r * CONV_ROWS + win, lanes]
            acc = jnp.zeros((CONV_ROWS, LANES), F32)
            for sub in range(SUBLANES):
                xs = x if sub == 0 else pltpu.roll(x, win - sub, axis=0)
                for a in range(2 * CONV_HALO // SUBLANES):
                    k = a * SUBLANES + sub - off
                    if 0 <= k < CONF_K:
                        acc = acc + xs[a * SUBLANES:a * SUBLANES + CONV_ROWS, :] * w_ref[k:k + 1, lanes]
            blocks.append(acc)
        cols.append(jnp.concatenate(blocks, axis=0))
    y = jnp.concatenate(cols, axis=1) + cb_ref[...]
    mu = jnp.mean(y, axis=-1, keepdims=True)
    yc = y - mu
    var = jnp.mean(yc * yc, axis=-1, keepdims=True)
    return _silu(yc * lax.rsqrt(var + EPS) * lg_ref[...] + lb_ref[...])


def _residual_router(mix, h, mod_ref, g2_ref, wr_ref, h_out, hm_out, lg_out, rows=slice(None)):
    h1 = h + mod_ref[0, 2:3, :] * mix
    h_out[0, rows] = h1
    hm = _norm_mod(h1, g2_ref[...], mod_ref[0, 3:4, :], mod_ref[0, 4:5, :])
    hm_out[0, rows] = _rows_to_tiles(hm)
    lg_out[0, rows] = jnp.dot(hm.astype(BF16), wr_ref[...], preferred_element_type=F32)


def _outproj0_body(attn_ref, glu_ref, glup_ref, glun_ref, cw_ref, cb_ref, lg_ref, lb_ref, w_ref,
                   h_ref, mod_ref, g2_ref, wr_ref, h_out, hm_out, lg_out):
    i = pl.program_id(1)
    last = pl.num_programs(1) - 1
    before = jnp.where(i > 0, glup_ref[0], 0.0)
    after = jnp.where(i < last, glun_ref[0], 0.0)
    conf = _conformer_rows(jnp.concatenate([before, glu_ref[0], after], axis=0), cw_ref, cb_ref, lg_ref, lb_ref)
    a = jnp.concatenate([attn_ref[0], conf.astype(BF16)], axis=1)
    mix = jnp.dot(a, w_ref[...], preferred_element_type=F32)
    _residual_router(mix, h_ref[0], mod_ref, g2_ref, wr_ref, h_out, hm_out, lg_out)


def _tile_spec(tm, d):
    assert d == SUBLANES * LANES
    return pl.BlockSpec((1, tm, SUBLANES, LANES), lambda bi, i: (bi, i, 0, 0))


def _token_out_specs(b, n, d, tm):
    specs = [
        pl.BlockSpec((1, tm, d), lambda bi, i: (bi, i, 0)),
        _tile_spec(tm, d),
        pl.BlockSpec((1, tm, LANES), lambda bi, i: (bi, i, 0)),
    ]
    shapes = [
        jax.ShapeDtypeStruct((b, n, d), F32),
        jax.ShapeDtypeStruct((b, n, SUBLANES, d // SUBLANES), F32),
        jax.ShapeDtypeStruct((b, n, LANES), F32),
    ]
    return specs, shapes


def _outproj0(attn, glu, conv_w, conv_b, ln_g, ln_b, w_out, h, mods, g2, w_r, tm=512):
    b, n, d = h.shape
    ch = glu.shape[-1]
    per = tm // CONV_HALO
    n_halo = n // CONV_HALO
    vec = pl.BlockSpec((1, ch), lambda bi, i: (0, 0))
    out_specs, out_shape = _token_out_specs(b, n, d, tm)
    return pl.pallas_call(
        _outproj0_body,
        grid=(b, n // tm),
        in_specs=[
            pl.BlockSpec((1, tm, ATTN_W), lambda bi, i: (bi, i, 0)),
            pl.BlockSpec((1, tm, ch), lambda bi, i: (bi, i, 0)),
            pl.BlockSpec((1, CONV_HALO, ch), lambda bi, i: (bi, jnp.maximum(i * per - 1, 0), 0)),
            pl.BlockSpec((1, CONV_HALO, ch), lambda bi, i: (bi, jnp.minimum((i + 1) * per, n_halo - 1), 0)),
            pl.BlockSpec((CONF_K, ch), lambda bi, i: (0, 0)),
            vec, vec, vec,
            pl.BlockSpec((d, d), lambda bi, i: (0, 0)),
            pl.BlockSpec((1, tm, d), lambda bi, i: (bi, i, 0)),
            pl.BlockSpec((1, N_MOD, d), lambda bi, i: (bi, 0, 0)),
            pl.BlockSpec((1, d), lambda bi, i: (0, 0)),
            pl.BlockSpec((d, LANES), lambda bi, i: (0, 0)),
        ],
        out_specs=out_specs,
        out_shape=out_shape,
        compiler_params=_params("arbitrary", "arbitrary"),
        name="outproj0",
    )(attn, glu, glu, glu, conv_w, conv_b, ln_g, ln_b, w_out, h, mods, g2, w_r)


def _mixer1_body(h_ref, hp_ref, hn_ref, moe_ref, moep_ref, moen_ref, modp_ref, mod_ref, g1_ref, w_ref,
                 cw_ref, wo_ref, g2_ref, wr_ref, h_out, hm_out, lg_out):
    i = pl.program_id(1)
    last = pl.num_programs(1) - 1
    tm, d = h_ref.shape[1:]
    gate_prev = modp_ref[0, 5:6, :]

    def stream(h_blk, moe_blk, rows=slice(None)):
        return h_blk[0, rows] + gate_prev * _tiles_to_rows(moe_blk[0, rows])

    part = tm // MIXER_PARTS
    win = part + 2 * SUBLANES
    for p in range(MIXER_PARTS):
        rows = slice(p * part, (p + 1) * part)
        h = stream(h_ref, moe_ref, rows)
        above = (stream(hp_ref, moep_ref) if p == 0
                 else stream(h_ref, moe_ref, slice(p * part - SUBLANES, p * part)))
        below = (stream(hn_ref, moen_ref) if p == MIXER_PARTS - 1
                 else stream(h_ref, moe_ref, slice((p + 1) * part, (p + 1) * part + SUBLANES)))
        hn = _norm_mod(jnp.concatenate([above, h, below], axis=0), g1_ref[...], mod_ref[0, 0:1, :], mod_ref[0, 1:2, :])
        proj = jnp.dot(hn.astype(BF16), w_ref[...], preferred_element_type=F32)
        z = proj[:, d:2 * d] * proj[:, 2 * d:]
        r = lax.broadcasted_iota(jnp.int32, z.shape, 0)
        if p == 0:
            z = jnp.where((r < SUBLANES) & (i == 0), 0.0, z)
        if p == MIXER_PARTS - 1:
            z = jnp.where((r >= part + SUBLANES) & (i == last), 0.0, z)
        y = (pltpu.roll(z, 1, axis=0) * cw_ref[0:1, :] + z * cw_ref[1:2, :]
             + pltpu.roll(z, win - 1, axis=0) * cw_ref[2:3, :])[SUBLANES:SUBLANES + part]
        gated = proj[SUBLANES:SUBLANES + part, :d] * y
        mix = jnp.dot(gated.astype(BF16), wo_ref[...], preferred_element_type=F32)
        _residual_router(mix, h, mod_ref, g2_ref, wr_ref, h_out, hm_out, lg_out, rows)


def _mixer1(h, moe, mods_prev, mods, g1, w_in, conv_w, w_out, g2, w_r, tm=512):
    b, n, d = h.shape
    per = tm // SUBLANES
    n8 = n // SUBLANES
    before = lambda bi, i: (bi, jnp.maximum(i * per - 1, 0), 0)
    after = lambda bi, i: (bi, jnp.minimum((i + 1) * per, n8 - 1), 0)
    tok = pl.BlockSpec((1, tm, d), lambda bi, i: (bi, i, 0))
    modspec = pl.BlockSpec((1, N_MOD, d), lambda bi, i: (bi, 0, 0))
    vec = pl.BlockSpec((1, d), lambda bi, i: (0, 0))
    out_specs, out_shape = _token_out_specs(b, n, d, tm)
    return pl.pallas_call(
        _mixer1_body,
        grid=(b, n // tm),
        in_specs=[
            tok,
            pl.BlockSpec((1, SUBLANES, d), before),
            pl.BlockSpec((1, SUBLANES, d), after),
            _tile_spec(tm, d),
            pl.BlockSpec((1, SUBLANES, SUBLANES, LANES), lambda bi, i: before(bi, i) + (0,)),
            pl.BlockSpec((1, SUBLANES, SUBLANES, LANES), lambda bi, i: after(bi, i) + (0,)),
            modspec, modspec, vec,
            pl.BlockSpec((d, 3 * d), lambda bi, i: (0, 0)),
            pl.BlockSpec((SC_K, d), lambda bi, i: (0, 0)),
            pl.BlockSpec((d, d), lambda bi, i: (0, 0)),
            vec,
            pl.BlockSpec((d, LANES), lambda bi, i: (0, 0)),
        ],
        out_specs=out_specs,
        out_shape=out_shape,
        compiler_params=_params("arbitrary", "arbitrary"),
        name="mixer1",
    )(h, h, h, moe, moe, moe, mods_prev, mods, g1, w_in, conv_w, w_out, g2, w_r)


def _stack_chunks(x):
    n = x.shape[1]
    return jnp.concatenate([x[:, c * ROUTER_CHUNK:(c + 1) * ROUTER_CHUNK] for c in range(n // ROUTER_CHUNK)], axis=0)


def _exclusive_rank(flags, utri, chunk_lt):
    incl = jnp.dot(flags.astype(BF16), utri, preferred_element_type=F32)
    tot = jnp.broadcast_to(incl[:, ROUTER_CHUNK - 1:ROUTER_CHUNK], incl.shape)
    base = jnp.dot(chunk_lt, tot.astype(BF16), preferred_element_type=F32)
    return incl - flags + base


def _router_body(lg_ref, utri_ref, lt_ref, blk_ref, idx_ref, gate_ref, loc_scr, tot_scr, end_scr, *, cap):
    n = lg_ref.shape[1]
    n_chunks = n // ROUTER_CHUNK
    lg = lg_ref[0]
    lane = lax.broadcasted_iota(jnp.int32, lg.shape, 1)
    valid = lane < N_EXPERTS
    x = jnp.where(valid, lg, -jnp.inf)
    ex = jnp.where(valid, jnp.exp(x - jnp.max(x, axis=-1, keepdims=True)), 0.0)
    aff = ex / jnp.sum(ex, axis=-1, keepdims=True)

    aff_t = aff.T[:N_EXPERTS, :]

    def count_ge(t):
        return jnp.sum(jnp.where(aff_t >= t, 1.0, 0.0), axis=-1, keepdims=True)

    def bit_step(i, bits):
        cand = bits | jnp.left_shift(jnp.int32(1), 30 - i)
        return jnp.where(count_ge(lax.bitcast_convert_type(cand, F32)) >= cap, cand, bits)

    bits = lax.fori_loop(0, 31, bit_step, jnp.zeros((N_EXPERTS, 1), jnp.int32))

    def refine(i, lo_hi):
        lo, hi = lo_hi
        mid = (lo + hi) * 0.5
        ok = count_ge(mid) >= cap
        return jnp.where(ok, mid, lo), jnp.where(ok, hi, mid)

    thr, _ = lax.fori_loop(0, REFINE_STEPS, refine,
                           (lax.bitcast_convert_type(bits, F32), lax.bitcast_convert_type(bits + 1, F32)))
    gt = jnp.where(aff_t > thr, 1.0, 0.0)
    eq = jnp.where(aff_t == thr, 1.0, 0.0)
    need = cap - jnp.sum(gt, axis=-1, keepdims=True)
    utri = utri_ref[...]
    chunk_lt = lt_ref[...]
    gt_s = _stack_chunks(gt)
    eq_s = _stack_chunks(eq)
    need_s = jnp.concatenate([need] * n_chunks, axis=0)
    sel = jnp.where((gt_s > 0) | ((eq_s > 0) & (_exclusive_rank(eq_s, utri, chunk_lt) < need_s)), 1.0, 0.0)

    loc_scr[:, :LANES] = jnp.dot(sel.astype(BF16), utri, preferred_element_type=F32).astype(BF16)
    aff_s = _stack_chunks(aff_t)
    hi = aff_s.astype(BF16)
    r1 = aff_s - hi.astype(F32)
    mid = r1.astype(BF16)
    loc_scr[:, LANES:2 * LANES] = hi
    loc_scr[:, 2 * LANES:3 * LANES] = mid
    loc_scr[:, 3 * LANES:] = (r1 - mid.astype(F32)).astype(BF16)
    sel_t = jnp.concatenate([sel[c * N_EXPERTS:(c + 1) * N_EXPERTS, :] for c in range(n_chunks)], axis=1)
    tot = jnp.dot(sel_t.astype(BF16), blk_ref[...], preferred_element_type=F32)
    tot_scr[...] = tot
    end_scr[...] = jnp.dot(tot.astype(BF16), utri, preferred_element_type=F32)

    slot = lax.broadcasted_iota(jnp.int32, (cap, 1), 0).astype(F32)
    row_id = lax.broadcasted_iota(jnp.int32, (cap, n_chunks * N_EXPERTS), 1)
    chunk_lane = lax.broadcasted_iota(jnp.int32, (cap, ROUTER_CHUNK), 1)

    def one_expert(e, carry):
        ends = end_scr[pl.ds(e, 1), :]
        before = ends <= slot
        chunk = jnp.sum(jnp.where(before, 1.0, 0.0), axis=-1, keepdims=True)
        base = jnp.sum(jnp.where(before, tot_scr[pl.ds(e, 1), :], 0.0), axis=-1, keepdims=True)
        pick = jnp.where(row_id == chunk.astype(jnp.int32) * N_EXPERTS + e, 1.0, 0.0).astype(BF16)
        picked = jnp.dot(pick, loc_scr[...], preferred_element_type=F32)
        inside = jnp.sum(jnp.where(picked[:, :LANES] <= slot - base, 1.0, 0.0), axis=-1, keepdims=True)
        idx_ref[0, e] = (chunk * ROUTER_CHUNK + inside).astype(jnp.int32)
        aff_row = picked[:, LANES:2 * LANES] + picked[:, 2 * LANES:3 * LANES] + picked[:, 3 * LANES:]
        gate_ref[0, e] = jnp.sum(jnp.where(chunk_lane == inside.astype(jnp.int32), aff_row, 0.0),
                                 axis=-1, keepdims=True)
        return carry

    lax.fori_loop(0, N_EXPERTS, one_expert, 0)


def _router(logits, utri, chunk_lt, chunk_of_token, cap):
    b, n, _ = logits.shape
    rows = (n // ROUTER_CHUNK) * N_EXPERTS
    assert n // ROUTER_CHUNK <= LANES
    slots = pl.BlockSpec((1, N_EXPERTS, cap, 1), lambda bi: (bi, 0, 0, 0))
    return pl.pallas_call(
        functools.partial(_router_body, cap=cap),
        grid=(b,),
        in_specs=[
            pl.BlockSpec((1, n, LANES), lambda bi: (bi, 0, 0)),
            pl.BlockSpec((ROUTER_CHUNK, ROUTER_CHUNK), lambda bi: (0, 0)),
            pl.BlockSpec((rows, rows), lambda bi: (0, 0)),
            pl.BlockSpec((n, LANES), lambda bi: (0, 0)),
        ],
        out_specs=[slots, slots],
        out_shape=[
            jax.ShapeDtypeStruct((b, N_EXPERTS, cap, 1), jnp.int32),
            jax.ShapeDtypeStruct((b, N_EXPERTS, cap, 1), F32),
        ],
        scratch_shapes=[pltpu.VMEM((rows, 4 * ROUTER_CHUNK), BF16), pltpu.VMEM((N_EXPERTS, LANES), F32),
                        pltpu.VMEM((N_EXPERTS, LANES), F32)],
        compiler_params=_params("arbitrary"),
        name="router",
    )(logits, utri, chunk_lt, chunk_of_token)


GATHER_UNROLL = 16
MOE_EXPERTS_PER_STEP = 1


def _gather_body(idx_ref, hm_ref, o_ref):
    cap = o_ref.shape[2]

    for k in range(o_ref.shape[1]):
        def group(g, carry, k=k):
            base = pl.multiple_of(g * GATHER_UNROLL, GATHER_UNROLL)
            halves = []
            for half in range(GATHER_UNROLL // SUBLANES):
                tiles = [hm_ref[0, idx_ref[k, 0, base + half * SUBLANES + u]][None] for u in range(SUBLANES)]
                halves.append(_tile_transpose8(tiles))
            rows = jnp.concatenate([jnp.concatenate([h[a][0] for h in halves], axis=0) for a in range(SUBLANES)],
                                   axis=1)
            o_ref[0, k, pl.ds(base, GATHER_UNROLL), :] = rows.astype(BF16)
            return carry

        lax.fori_loop(0, cap // GATHER_UNROLL, group, 0)


def _gather(idx, hm):
    b, n = hm.shape[:2]
    cap = idx.shape[-1]
    d = SUBLANES * LANES
    eps = MOE_EXPERTS_PER_STEP
    steps = N_EXPERTS // eps
    return pl.pallas_call(
        _gather_body,
        grid=(b, steps),
        in_specs=[
            pl.BlockSpec((eps, 1, cap), lambda bi, e: (bi * steps + e, 0, 0), memory_space=pltpu.SMEM),
            pl.BlockSpec((1, n, SUBLANES, LANES), lambda bi, e: (bi, 0, 0, 0)),
        ],
        out_specs=pl.BlockSpec((1, eps, cap, d), lambda bi, e: (bi, e, 0, 0)),
        out_shape=jax.ShapeDtypeStruct((b, N_EXPERTS, cap, d), BF16),
        compiler_params=_params("arbitrary", "arbitrary"),
        name="moe_gather",
    )(idx, hm)


FFN_SAMPLES_PER_STEP = 2


def _ffn_body(x_ref, gate_ref, wg_ref, wu_ref, wd_ref, o_ref):
    cap = x_ref.shape[2]
    x = jnp.concatenate([x_ref[s, 0] for s in range(x_ref.shape[0])], axis=0)
    hg = jnp.dot(x, wg_ref[0, 0].astype(BF16), preferred_element_type=F32)
    hu = jnp.dot(x, wu_ref[0, 0].astype(BF16), preferred_element_type=F32)
    y = jnp.dot((_silu(hg) * hu).astype(BF16), wd_ref[0, 0].astype(BF16), preferred_element_type=F32)
    for s in range(x_ref.shape[0]):
        o_ref[s, 0] = y[s * cap:(s + 1) * cap] * gate_ref[s, 0]


def _expert_ffn(xs, gates, w_gate, w_up, w_down, layer):
    b, n_e, cap, d = xs.shape
    f = w_gate.shape[-1]
    sps = FFN_SAMPLES_PER_STEP if b % FFN_SAMPLES_PER_STEP == 0 else 1
    rows = pl.BlockSpec((sps, 1, cap, d), lambda e, bi: (bi, e, 0, 0))
    return pl.pallas_call(
        _ffn_body,
        grid=(n_e, b // sps),
        in_specs=[
            rows,
            pl.BlockSpec((sps, 1, cap, 1), lambda e, bi: (bi, e, 0, 0)),
            pl.BlockSpec((1, 1, d, f), lambda e, bi: (layer, e, 0, 0)),
            pl.BlockSpec((1, 1, d, f), lambda e, bi: (layer, e, 0, 0)),
            pl.BlockSpec((1, 1, f, d), lambda e, bi: (layer, e, 0, 0)),
        ],
        out_specs=rows,
        out_shape=jax.ShapeDtypeStruct(xs.shape, F32),
        compiler_params=_params("arbitrary", "arbitrary"),
        name="moe_ffn",
    )(xs, gates, w_gate, w_up, w_down)


SCATTER_UNROLL = 8


def _scatter_body(idx_ref, ys_ref, o_ref):
    cap = ys_ref.shape[2]

    @pl.when(pl.program_id(1) == 0)
    def _():
        o_ref[...] = jnp.zeros(o_ref.shape, F32)

    for k in range(ys_ref.shape[1]):
        def group(g, carry, k=k):
            base = pl.multiple_of(g * SCATTER_UNROLL, SCATTER_UNROLL)
            tiles = []
            for part in range(SCATTER_UNROLL // SUBLANES):
                y = ys_ref[0, k, pl.ds(base + part * SUBLANES, SUBLANES), :]
                tiles += _tile_transpose8([y[:, a * LANES:(a + 1) * LANES][None] for a in range(SUBLANES)])
            rows = [idx_ref[k, 0, base + u] for u in range(SCATTER_UNROLL)]
            sums = [o_ref[0, rows[u]] + tiles[u][0] for u in range(SCATTER_UNROLL)]
            for u in range(SCATTER_UNROLL):
                o_ref[0, rows[u]] = sums[u]
            return carry

        lax.fori_loop(0, cap // SCATTER_UNROLL, group, 0)


def _scatter(idx, ys, n):
    b, n_e, cap, d = ys.shape
    assert SCATTER_UNROLL % SUBLANES == 0 and cap % SCATTER_UNROLL == 0 and d == SUBLANES * LANES
    eps = MOE_EXPERTS_PER_STEP
    steps = n_e // eps
    return pl.pallas_call(
        _scatter_body,
        grid=(b, steps),
        in_specs=[
            pl.BlockSpec((eps, 1, cap), lambda bi, e: (bi * steps + e, 0, 0), memory_space=pltpu.SMEM),
            pl.BlockSpec((1, eps, cap, d), lambda bi, e: (bi, e, 0, 0)),
        ],
        out_specs=pl.BlockSpec((1, n, SUBLANES, LANES), lambda bi, e: (bi, 0, 0, 0)),
        out_shape=jax.ShapeDtypeStruct((b, n, SUBLANES, LANES), F32),
        compiler_params=_params("arbitrary", "arbitrary"),
        name="moe_scatter",
    )(idx, ys)


def _ec_moe(hm, logits, w_gate, w_up, w_down, layer, tables):
    b, n = hm.shape[:2]
    cap = max(1, EC_CAPACITY * n // N_EXPERTS)
    idx, gates = _router(logits, *tables, cap)
    idx = idx.reshape(b * N_EXPERTS, 1, cap)
    xs = _gather(idx, hm)
    ys = _expert_ffn(xs, gates, w_gate, w_up, w_down, layer)
    return _scatter(idx, ys, n)


def _final_body(h_ref, moe_ref, mod_ref, g_ref, o_ref):
    h = h_ref[0] + mod_ref[0, 5:6, :] * _tiles_to_rows(moe_ref[0])
    o_ref[0] = h * lax.rsqrt(jnp.mean(h * h, axis=-1, keepdims=True) + EPS) * g_ref[...]


def _final(h, moe, mods, g, tm=512):
    b, n, d = h.shape
    tok = pl.BlockSpec((1, tm, d), lambda bi, i: (bi, i, 0))
    return pl.pallas_call(
        _final_body,
        grid=(b, n // tm),
        in_specs=[tok, _tile_spec(tm, d), pl.BlockSpec((1, N_MOD, d), lambda bi, i: (bi, 0, 0)),
                  pl.BlockSpec((1, d), lambda bi, i: (0, 0))],
        out_specs=tok,
        out_shape=jax.ShapeDtypeStruct((b, n, d), F32),
        compiler_params=_params("arbitrary", "arbitrary"),
        name="final_norm",
    )(h, moe, mods, g)


def _rope_tables(n):
    rows = n // GRID_W
    row = np.repeat(np.arange(rows, dtype=np.float64), GRID_W)
    col = np.tile(np.arange(GRID_W, dtype=np.float64), rows)
    axis_dim = HEAD_DIM // 2
    inv = ROPE_THETA ** (-np.arange(0, axis_dim, 2, dtype=np.float64) / axis_dim)
    ang = np.concatenate([row[:, None] * inv, col[:, None] * inv], axis=-1)
    cos = np.cos(ang)
    sin = np.sin(ang)
    cos64 = np.concatenate([cos, cos], axis=-1)
    sin64 = np.concatenate([-sin, sin], axis=-1)
    reps = LANES // HEAD_DIM
    return jnp.asarray(np.tile(cos64, (1, reps)), F32), jnp.asarray(np.tile(sin64, (1, reps)), F32)


def _router_tables(n):
    r = np.arange(ROUTER_CHUNK)
    utri = r[:, None] <= r[None, :]
    rows = np.arange((n // ROUTER_CHUNK) * N_EXPERTS)
    same_e = (rows[:, None] % N_EXPERTS) == (rows[None, :] % N_EXPERTS)
    earlier = (rows[None, :] // N_EXPERTS) < (rows[:, None] // N_EXPERTS)
    chunk_of_token = (np.arange(n)[:, None] // ROUTER_CHUNK) == np.arange(LANES)[None, :]
    return tuple(jnp.asarray(t.astype(np.float32), BF16) for t in (utri, same_e & earlier, chunk_of_token))


def _head_ones():
    r = np.arange(MXU_DIM) // HEAD_DIM
    return jnp.asarray((r[:, None] == r[None, :]).astype(np.float32), BF16)


def kernel(x, c, ctx, c_ctx, ada_w, ada_b, norm1_g, norm2_g, ev_w_in, ev_q_g, ev_k_g, ev_conv_w, ev_conv_b,
           ev_ln_g, ev_ln_b, ev_w_out, sc_w_in, sc_conv_w, sc_w_out, moe_w_r, moe_w_gate, moe_w_up,
           moe_w_down, final_g):
    b, n, d = x.shape
    depth = ada_w.shape[0]
    assert depth == 2 and b < SUBLANES and n % ROUTER_CHUNK == 0

    cos2, sin2 = _rope_tables(n)
    tables = _router_tables(n)
    ones_bd = _head_ones()

    cvecs = jnp.zeros((SUBLANES, d), F32).at[:b].set(c).at[b].set(c_ctx)
    mods = _ada_mod(cvecs, ada_w, ada_b).reshape(depth, SUBLANES, N_MOD, d)
    w_r = jnp.pad(moe_w_r, ((0, 0), (0, 0), (0, LANES - N_EXPERTS))).astype(BF16)

    w_in0 = ev_w_in[0].astype(BF16)
    qg = jnp.tile(ev_q_g[0], N_Q_HEADS)[None, :]
    kg = jnp.tile(ev_k_g[0], N_KV_HEADS)[None, :]
    qt, k, vt, glu = _inproj0(x, mods[0], norm1_g[0:1], w_in0, qg, kg, cos2, sin2, ones_bd)
    k_ctx, vt_ctx = _ctxkv(ctx, mods[0], b, norm1_g[0:1], w_in0[:, ATTN_W:ATTN_W + 2 * KV_W], kg, ones_bd)
    attn = _attention(qt, k_ctx, k, vt_ctx, vt)
    h, hm, logits = _outproj0(attn, glu, ev_conv_w[0], ev_conv_b[0:1], ev_ln_g[0:1], ev_ln_b[0:1],
                              ev_w_out[0].astype(BF16), x, mods[0], norm2_g[0:1], w_r[0])
    moe0 = _ec_moe(hm, logits, moe_w_gate, moe_w_up, moe_w_down, 0, tables)

    h, hm, logits = _mixer1(h, moe0, mods[0], mods[1], norm1_g[1:2], sc_w_in[0].astype(BF16), sc_conv_w[0],
                            sc_w_out[0].astype(BF16), norm2_g[1:2], w_r[1])
    moe1 = _ec_moe(hm, logits, moe_w_gate, moe_w_up, moe_w_down, 1, tables)

    return _final(h, moe1, mods[1], final_g[None, :])
```

```python
import functools

import jax
import jax.numpy as jnp
import numpy as np
from jax import lax
from jax.experimental import pallas as pl
from jax.experimental.pallas import tpu as pltpu

F32 = jnp.float32
BF16 = jnp.bfloat16

HEAD_DIM = 64
N_Q_HEADS = 8
N_KV_HEADS = 2
GRID_W = 64
ROPE_THETA = 10000.0
CONF_K = 31
SC_K = 3
N_EXPERTS = 16
EC_CAPACITY = 2
N_MOD = 6
EPS = 1e-6

ATTN_W = N_Q_HEADS * HEAD_DIM
KV_W = N_KV_HEADS * HEAD_DIM
Q_PER_KV = N_Q_HEADS // N_KV_HEADS
QK_SCALE = HEAD_DIM ** -0.5 * 1.4426950408889634

LANES = 128
SUBLANES = 8
MXU_DIM = 256
VMEM_LIMIT_BYTES = 60000 * 1024

CONV_HALO = 16
ROUTER_CHUNK = LANES
REFINE_STEPS = 24
INPROJ_PARTS = 2
MIXER_PARTS = 1


def _params(*sem):
    return pltpu.CompilerParams(dimension_semantics=sem, vmem_limit_bytes=VMEM_LIMIT_BYTES)


def _norm_mod(x, g, shift, scale):
    ms = jnp.mean(x * x, axis=-1, keepdims=True)
    y = x * lax.rsqrt(ms + EPS) * g
    return y * (1.0 + scale) + shift


def _head_sumsq(x, ones_blockdiag):
    return jnp.dot((x * x).astype(BF16), ones_blockdiag, preferred_element_type=F32)


def _swap_half(x):
    w = x.shape[-1]
    lane = lax.broadcasted_iota(jnp.int32, x.shape, 1)
    first = (lane % HEAD_DIM) < (HEAD_DIM // 2)
    return jnp.where(first, pltpu.roll(x, w - HEAD_DIM // 2, axis=1), pltpu.roll(x, HEAD_DIM // 2, axis=1))


def _silu(x):
    return x * jax.nn.sigmoid(x)


def _tile_transpose8(vs):
    sub = lax.broadcasted_iota(jnp.int32, vs[0].shape, 1)
    for d in (4, 2, 1):
        keep = (sub & d) == 0
        out = list(vs)
        for i in range(SUBLANES):
            if i & d == 0:
                a, b = vs[i], vs[i + d]
                out[i] = jnp.where(keep, a, pltpu.roll(b, d, axis=1))
                out[i + d] = jnp.where(keep, pltpu.roll(a, SUBLANES - d, axis=1), b)
        vs = out
    return vs


def _rows_to_tiles(x):
    r = x.shape[0]
    vs = [x[:, a * LANES:(a + 1) * LANES].reshape(r // SUBLANES, SUBLANES, LANES) for a in range(SUBLANES)]
    return jnp.stack(_tile_transpose8(vs), axis=1).reshape(r, SUBLANES, LANES)


def _tiles_to_rows(x3):
    r = x3.shape[0]
    x4 = x3.reshape(r // SUBLANES, SUBLANES, SUBLANES, LANES)
    vs = _tile_transpose8([x4[:, j] for j in range(SUBLANES)])
    return jnp.concatenate([v.reshape(r, LANES) for v in vs], axis=1)


def _store_vt_with_ones(vt_ref, cols, v):
    vt = v.T
    ones = jnp.ones((HEAD_DIM, v.shape[0]), F32)
    for j in range(N_KV_HEADS):
        vt_ref[0, j, :, cols] = jnp.concatenate([vt[j * HEAD_DIM:(j + 1) * HEAD_DIM, :], ones], axis=0).astype(BF16)


def _ada_body(c_ref, w_ref, b_ref, o_ref):
    s = _silu(c_ref[...]).astype(BF16)
    o_ref[0] = jnp.dot(s, w_ref[0].astype(BF16), preferred_element_type=F32) + b_ref[0]


def _ada_mod(cvecs, ada_w, ada_b):
    n_layers, d, n_out = ada_w.shape
    tn = n_out // 4
    return pl.pallas_call(
        _ada_body,
        grid=(n_layers, n_out // tn),
        in_specs=[
            pl.BlockSpec((SUBLANES, d), lambda l, j: (0, 0)),
            pl.BlockSpec((1, d, tn), lambda l, j: (l, 0, j)),
            pl.BlockSpec((1, 1, tn), lambda l, j: (l, 0, j)),
        ],
        out_specs=pl.BlockSpec((1, SUBLANES, tn), lambda l, j: (l, 0, j)),
        out_shape=jax.ShapeDtypeStruct((n_layers, SUBLANES, n_out), F32),
        compiler_params=_params("arbitrary", "arbitrary"),
        name="ada_mod",
    )(cvecs, ada_w, ada_b.reshape(n_layers, 1, n_out))


def _inproj0_body(h_ref, mod_ref, g1_ref, w_ref, qg_ref, kg_ref, cos_ref, sin_ref, ones_ref,
                  qt_ref, k_ref, vt_ref, glu_ref):
    ones = ones_ref[...]
    tm = h_ref.shape[1]
    part = tm // INPROJ_PARTS
    for p in range(INPROJ_PARTS):
        rows = slice(p * part, (p + 1) * part)
        hn = _norm_mod(h_ref[0, rows, :], g1_ref[...], mod_ref[0, 0:1, :], mod_ref[0, 1:2, :])
        proj = jnp.dot(hn.astype(BF16), w_ref[0].astype(BF16), preferred_element_type=F32)
        cos2 = cos_ref[rows, :]
        sin2 = sin_ref[rows, :]

        q = proj[:, :ATTN_W]
        ssq = jnp.concatenate([_head_sumsq(q[:, :MXU_DIM], ones), _head_sumsq(q[:, MXU_DIM:], ones)], axis=1)
        qn = q * lax.rsqrt(ssq * (1.0 / HEAD_DIM) + EPS) * qg_ref[...]
        cos = jnp.concatenate([cos2] * (ATTN_W // LANES), axis=1)
        sin = jnp.concatenate([sin2] * (ATTN_W // LANES), axis=1)
        qr = (qn * cos + _swap_half(qn) * sin) * QK_SCALE
        qt_ref[0, :, rows] = qr.T.astype(BF16)

        k = proj[:, ATTN_W:ATTN_W + KV_W]
        kn = k * lax.rsqrt(_head_sumsq(k, ones[:KV_W, :KV_W]) * (1.0 / HEAD_DIM) + EPS) * kg_ref[...]
        k_ref[0, rows, :] = (kn * cos2 + _swap_half(kn) * sin2).astype(BF16)
        _store_vt_with_ones(vt_ref, rows, proj[:, ATTN_W + KV_W:ATTN_W + 2 * KV_W])

        c0 = ATTN_W + 2 * KV_W
        cc = (proj.shape[1] - c0) // 2
        glu_ref[0, rows, :] = proj[:, c0:c0 + cc] * jax.nn.sigmoid(proj[:, c0 + cc:])


def _inproj0(h, mods, g1, w_in, qg, kg, cos2, sin2, ones_bd, tm=512):
    b, n, d = h.shape
    n_in = w_in.shape[-1]
    conv_ch = (n_in - ATTN_W - 2 * KV_W) // 2
    return pl.pallas_call(
        _inproj0_body,
        grid=(b, n // tm),
        in_specs=[
            pl.BlockSpec((1, tm, d), lambda bi, i: (bi, i, 0)),
            pl.BlockSpec((1, N_MOD, d), lambda bi, i: (bi, 0, 0)),
            pl.BlockSpec((1, d), lambda bi, i: (0, 0)),
            pl.BlockSpec((1, d, n_in), lambda bi, i: (0, 0, 0), pipeline_mode=pl.Buffered(1)),
            pl.BlockSpec((1, ATTN_W), lambda bi, i: (0, 0)),
            pl.BlockSpec((1, KV_W), lambda bi, i: (0, 0)),
            pl.BlockSpec((tm, LANES), lambda bi, i: (i, 0)),
            pl.BlockSpec((tm, LANES), lambda bi, i: (i, 0)),
            pl.BlockSpec((MXU_DIM, MXU_DIM), lambda bi, i: (0, 0)),
        ],
        out_specs=[
            pl.BlockSpec((1, ATTN_W, tm), lambda bi, i: (bi, 0, i)),
            pl.BlockSpec((1, tm, KV_W), lambda bi, i: (bi, i, 0)),
            pl.BlockSpec((1, N_KV_HEADS, 2 * HEAD_DIM, tm), lambda bi, i: (bi, 0, 0, i)),
            pl.BlockSpec((1, tm, conv_ch), lambda bi, i: (bi, i, 0)),
        ],
        out_shape=[
            jax.ShapeDtypeStruct((b, ATTN_W, n), BF16),
            jax.ShapeDtypeStruct((b, n, KV_W), BF16),
            jax.ShapeDtypeStruct((b, N_KV_HEADS, 2 * HEAD_DIM, n), BF16),
            jax.ShapeDtypeStruct((b, n, conv_ch), F32),
        ],
        compiler_params=_params("arbitrary", "arbitrary"),
        name="inproj0",
    )(h, mods, g1, w_in, qg, kg, cos2, sin2, ones_bd)


def _ctxkv_body(x_ref, mod_ref, g1_ref, w_ref, kg_ref, ones_ref, k_ref, vt_ref):
    hn = _norm_mod(x_ref[0], g1_ref[...], mod_ref[0, 0:1, :], mod_ref[0, 1:2, :])
    proj = jnp.dot(hn.astype(BF16), w_ref[0].astype(BF16), preferred_element_type=F32)
    k = proj[:, :KV_W]
    kn = k * lax.rsqrt(_head_sumsq(k, ones_ref[...][:KV_W, :KV_W]) * (1.0 / HEAD_DIM) + EPS) * kg_ref[...]
    k_ref[0] = kn.astype(BF16)
    _store_vt_with_ones(vt_ref, slice(None), proj[:, KV_W:])


def _ctxkv(ctx, mods, ctx_row, g1, w_in, kg, ones_bd):
    b, t, d = ctx.shape
    assert ATTN_W % (2 * KV_W) == 0
    kv_block = ATTN_W // (2 * KV_W)
    return pl.pallas_call(
        _ctxkv_body,
        grid=(b,),
        in_specs=[
            pl.BlockSpec((1, t, d), lambda bi: (bi, 0, 0)),
            pl.BlockSpec((1, N_MOD, d), lambda bi: (ctx_row, 0, 0)),
            pl.BlockSpec((1, d), lambda bi: (0, 0)),
            pl.BlockSpec((1, d, 2 * KV_W), lambda bi: (0, 0, kv_block)),
            pl.BlockSpec((1, KV_W), lambda bi: (0, 0)),
            pl.BlockSpec((MXU_DIM, MXU_DIM), lambda bi: (0, 0)),
        ],
        out_specs=[
            pl.BlockSpec((1, t, KV_W), lambda bi: (bi, 0, 0)),
            pl.BlockSpec((1, N_KV_HEADS, 2 * HEAD_DIM, t), lambda bi: (bi, 0, 0, 0)),
        ],
        out_shape=[
            jax.ShapeDtypeStruct((b, t, KV_W), BF16),
            jax.ShapeDtypeStruct((b, N_KV_HEADS, 2 * HEAD_DIM, t), BF16),
        ],
        compiler_params=_params("arbitrary"),
        name="ctx_kv",
    )(ctx, mods, g1, w_in, kg, ones_bd)


ATTN_KEY_CHUNK = 512


def _attn_body(qt_ref, kc_ref, k_ref, vtc_ref, vt_ref, o_ref, s0, s1, p0, p1, ot):
    s_bufs = (s0, s1)
    p_bufs = (p0, p1)
    t_ctx = kc_ref.shape[1]
    n = k_ref.shape[1]
    tq = qt_ref.shape[2]

    def column_max(run, sc):
        part = jnp.max(sc.reshape(sc.shape[0] // SUBLANES, SUBLANES, tq), axis=0)
        return part if run is None else jnp.maximum(run, part)

    assert t_ctx % SUBLANES == 0 and n % ATTN_KEY_CHUNK == 0
    chunks = [(0, t_ctx, None)] + [(t_ctx + lo, ATTN_KEY_CHUNK, lo) for lo in range(0, n, ATTN_KEY_CHUNK)]

    def padded_query(h):
        qt = qt_ref[0, h * HEAD_DIM:(h + 1) * HEAD_DIM, :]
        zeros = jnp.zeros_like(qt)
        return jnp.concatenate([qt, zeros] if h // Q_PER_KV == 0 else [zeros, qt], axis=0)

    def score_chunk(h, qpad, run, chunk):
        row, size, lo = chunk
        keys = kc_ref[0] if lo is None else k_ref[0, lo:lo + size, :]
        sc = jnp.dot(keys, qpad, preferred_element_type=F32)
        s_bufs[h % 2][row:row + size, :] = sc
        return column_max(run, sc)

    def exp_chunk(h, m, chunk):
        row, size, _ = chunk
        p_bufs[h % 2][row:row + size, :] = jnp.exp2(s_bufs[h % 2][row:row + size, :] - m).astype(BF16)

    def value_chunk(h, acc, chunk):
        row, size, lo = chunk
        j = h // Q_PER_KV
        vt = vtc_ref[0, j] if lo is None else vt_ref[0, j, :, lo:lo + size]
        part = jnp.dot(vt, p_bufs[h % 2][row:row + size, :], preferred_element_type=F32)
        return part if acc is None else acc + part

    m = None
    for h in range(-1, N_Q_HEADS + 1):
        qpad = padded_query(h + 1) if h + 1 < N_Q_HEADS else None
        run = acc = None
        for chunk in chunks:
            if qpad is not None:
                run = score_chunk(h + 1, qpad, run, chunk)
            if 0 <= h - 1:
                acc = value_chunk(h - 1, acc, chunk)
            if 0 <= h < N_Q_HEADS:
                exp_chunk(h, m, chunk)
        if acc is not None:
            ot[(h - 1) * HEAD_DIM:h * HEAD_DIM, :] = acc[:HEAD_DIM, :] / acc[HEAD_DIM:HEAD_DIM + 1, :]
        m = jnp.max(run, axis=0, keepdims=True) if run is not None else None
    o_ref[0] = ot[...].T.astype(BF16)


def _attention(qt, k_ctx, k, vt_ctx, vt, tq=512):
    b, width, n = qt.shape
    t_ctx = k_ctx.shape[1]
    t_all = t_ctx + n
    return pl.pallas_call(
        _attn_body,
        grid=(b, n // tq),
        scratch_shapes=[pltpu.VMEM((t_all, tq), F32), pltpu.VMEM((t_all, tq), F32),
                        pltpu.VMEM((t_all, tq), BF16), pltpu.VMEM((t_all, tq), BF16),
                        pltpu.VMEM((width, tq), F32)],
        in_specs=[
            pl.BlockSpec((1, width, tq), lambda bi, i: (bi, 0, i)),
            pl.BlockSpec((1, t_ctx, KV_W), lambda bi, i: (bi, 0, 0)),
            pl.BlockSpec((1, n, KV_W), lambda bi, i: (bi, 0, 0)),
            pl.BlockSpec((1, N_KV_HEADS, 2 * HEAD_DIM, t_ctx), lambda bi, i: (bi, 0, 0, 0)),
            pl.BlockSpec((1, N_KV_HEADS, 2 * HEAD_DIM, n), lambda bi, i: (bi, 0, 0, 0)),
        ],
        out_specs=pl.BlockSpec((1, tq, width), lambda bi, i: (bi, i, 0)),
        out_shape=jax.ShapeDtypeStruct((b, n, width), BF16),
        compiler_params=_params("arbitrary", "arbitrary"),
        name="attention",
    )(qt, k_ctx, k, vt_ctx, vt)


CONV_ROWS = 64


def _conformer_rows(window, w_ref, cb_ref, lg_ref, lb_ref):
    tm = window.shape[0] - 2 * CONV_HALO
    ch = window.shape[1]
    win = CONV_ROWS + 2 * CONV_HALO
    off = CONV_HALO - CONF_K // 2
    cols = []
    for t in range(ch // LANES):
        lanes = slice(t * LANES, (t + 1) * LANES)
        blocks = []
        for r in range(tm // CONV_ROWS):
            x = window[r * CONV_ROWS:r * CONV_ROWS + win, lanes]
            acc = jnp.zeros((CONV_ROWS, LANES), F32)
            for sub in range(SUBLANES):
                xs = x if sub == 0 else pltpu.roll(x, win - sub, axis=0)
                for a in range(2 * CONV_HALO // SUBLANES):
                    k = a * SUBLANES + sub - off
                    if 0 <= k < CONF_K:
                        acc = acc + xs[a * SUBLANES:a * SUBLANES + CONV_ROWS, :] * w_ref[k:k + 1, lanes]
            blocks.append(acc)
        cols.append(jnp.concatenate(blocks, axis=0))
    y = jnp.concatenate(cols, axis=1) + cb_ref[...]
    mu = jnp.mean(y, axis=-1, keepdims=True)
    yc = y - mu
    var = jnp.mean(yc * yc, axis=-1, keepdims=True)
    return _silu(yc * lax.rsqrt(var + EPS) * lg_ref[...] + lb_ref[...])


def _residual_router(mix, h, mod_ref, g2_ref, wr_ref, h_out, hm_out, lg_out, rows=slice(None)):
    h1 = h + mod_ref[0, 2:3, :] * mix
    h_out[0, rows] = h1
    hm = _norm_mod(h1, g2_ref[...], mod_ref[0, 3:4, :], mod_ref[0, 4:5, :])
    hm_out[0, rows] = _rows_to_tiles(hm)
    lg_out[0, rows] = jnp.dot(hm.astype(BF16), wr_ref[...], preferred_element_type=F32)


def _outproj0_body(attn_ref, glu_ref, glup_ref, glun_ref, cw_ref, cb_ref, lg_ref, lb_ref, w_ref,
                   h_ref, mod_ref, g2_ref, wr_ref, h_out, hm_out, lg_out):
    i = pl.program_id(1)
    last = pl.num_programs(1) - 1
    before = jnp.where(i > 0, glup_ref[0], 0.0)
    after = jnp.where(i < last, glun_ref[0], 0.0)
    conf = _conformer_rows(jnp.concatenate([before, glu_ref[0], after], axis=0), cw_ref, cb_ref, lg_ref, lb_ref)
    a = jnp.concatenate([attn_ref[0], conf.astype(BF16)], axis=1)
    mix = jnp.dot(a, w_ref[...], preferred_element_type=F32)
    _residual_router(mix, h_ref[0], mod_ref, g2_ref, wr_ref, h_out, hm_out, lg_out)


def _tile_spec(tm, d):
    assert d == SUBLANES * LANES
    return pl.BlockSpec((1, tm, SUBLANES, LANES), lambda bi, i: (bi, i, 0, 0))


def _token_out_specs(b, n, d, tm):
    specs = [
        pl.BlockSpec((1, tm, d), lambda bi, i: (bi, i, 0)),
        _tile_spec(tm, d),
        pl.BlockSpec((1, tm, LANES), lambda bi, i: (bi, i, 0)),
    ]
    shapes = [
        jax.ShapeDtypeStruct((b, n, d), F32),
        jax.ShapeDtypeStruct((b, n, SUBLANES, d // SUBLANES), F32),
        jax.ShapeDtypeStruct((b, n, LANES), F32),
    ]
    return specs, shapes


def _outproj0(attn, glu, conv_w, conv_b, ln_g, ln_b, w_out, h, mods, g2, w_r, tm=512):
    b, n, d = h.shape
    ch = glu.shape[-1]
    per = tm // CONV_HALO
    n_halo = n // CONV_HALO
    vec = pl.BlockSpec((1, ch), lambda bi, i: (0, 0))
    out_specs, out_shape = _token_out_specs(b, n, d, tm)
    return pl.pallas_call(
        _outproj0_body,
        grid=(b, n // tm),
        in_specs=[
            pl.BlockSpec((1, tm, ATTN_W), lambda bi, i: (bi, i, 0)),
            pl.BlockSpec((1, tm, ch), lambda bi, i: (bi, i, 0)),
            pl.BlockSpec((1, CONV_HALO, ch), lambda bi, i: (bi, jnp.maximum(i * per - 1, 0), 0)),
            pl.BlockSpec((1, CONV_HALO, ch), lambda bi, i: (bi, jnp.minimum((i + 1) * per, n_halo - 1), 0)),
            pl.BlockSpec((CONF_K, ch), lambda bi, i: (0, 0)),
            vec, vec, vec,
            pl.BlockSpec((d, d), lambda bi, i: (0, 0)),
            pl.BlockSpec((1, tm, d), lambda bi, i: (bi, i, 0)),
            pl.BlockSpec((1, N_MOD, d), lambda bi, i: (bi, 0, 0)),
            pl.BlockSpec((1, d), lambda bi, i: (0, 0)),
            pl.BlockSpec((d, LANES), lambda bi, i: (0, 0)),
        ],
        out_specs=out_specs,
        out_shape=out_shape,
        compiler_params=_params("arbitrary", "arbitrary"),
        name="outproj0",
    )(attn, glu, glu, glu, conv_w, conv_b, ln_g, ln_b, w_out, h, mods, g2, w_r)


def _mixer1_body(h_ref, hp_ref, hn_ref, moe_ref, moep_ref, moen_ref, modp_ref, mod_ref, g1_ref, w_ref,
                 cw_ref, wo_ref, g2_ref, wr_ref, h_out, hm_out, lg_out):
    i = pl.program_id(1)
    last = pl.num_programs(1) - 1
    tm, d = h_ref.shape[1:]
    gate_prev = modp_ref[0, 5:6, :]

    def stream(h_blk, moe_blk, rows=slice(None)):
        return h_blk[0, rows] + gate_prev * _tiles_to_rows(moe_blk[0, rows])

    part = tm // MIXER_PARTS
    win = part + 2 * SUBLANES
    for p in range(MIXER_PARTS):
        rows = slice(p * part, (p + 1) * part)
        h = stream(h_ref, moe_ref, rows)
        above = (stream(hp_ref, moep_ref) if p == 0
                 else stream(h_ref, moe_ref, slice(p * part - SUBLANES, p * part)))
        below = (stream(hn_ref, moen_ref) if p == MIXER_PARTS - 1
                 else stream(h_ref, moe_ref, slice((p + 1) * part, (p + 1) * part + SUBLANES)))
        hn = _norm_mod(jnp.concatenate([above, h, below], axis=0), g1_ref[...], mod_ref[0, 0:1, :], mod_ref[0, 1:2, :])
        proj = jnp.dot(hn.astype(BF16), w_ref[0].astype(BF16), preferred_element_type=F32)
        z = proj[:, d:2 * d] * proj[:, 2 * d:]
        r = lax.broadcasted_iota(jnp.int32, z.shape, 0)
        if p == 0:
            z = jnp.where((r < SUBLANES) & (i == 0), 0.0, z)
        if p == MIXER_PARTS - 1:
            z = jnp.where((r >= part + SUBLANES) & (i == last), 0.0, z)
        y = (pltpu.roll(z, 1, axis=0) * cw_ref[0:1, :] + z * cw_ref[1:2, :]
             + pltpu.roll(z, win - 1, axis=0) * cw_ref[2:3, :])[SUBLANES:SUBLANES + part]
        gated = proj[SUBLANES:SUBLANES + part, :d] * y
        mix = jnp.dot(gated.astype(BF16), wo_ref[0].astype(BF16), preferred_element_type=F32)
        _residual_router(mix, h, mod_ref, g2_ref, wr_ref, h_out, hm_out, lg_out, rows)


def _mixer1(h, moe, mods_prev, mods, g1, w_in, conv_w, w_out, g2, w_r, tm=512):
    b, n, d = h.shape
    per = tm // SUBLANES
    n8 = n // SUBLANES
    before = lambda bi, i: (bi, jnp.maximum(i * per - 1, 0), 0)
    after = lambda bi, i: (bi, jnp.minimum((i + 1) * per, n8 - 1), 0)
    tok = pl.BlockSpec((1, tm, d), lambda bi, i: (bi, i, 0))
    modspec = pl.BlockSpec((1, N_MOD, d), lambda bi, i: (bi, 0, 0))
    vec = pl.BlockSpec((1, d), lambda bi, i: (0, 0))
    out_specs, out_shape = _token_out_specs(b, n, d, tm)
    return pl.pallas_call(
        _mixer1_body,
        grid=(b, n // tm),
        in_specs=[
            tok,
            pl.BlockSpec((1, SUBLANES, d), before),
            pl.BlockSpec((1, SUBLANES, d), after),
            _tile_spec(tm, d),
            pl.BlockSpec((1, SUBLANES, SUBLANES, LANES), lambda bi, i: before(bi, i) + (0,)),
            pl.BlockSpec((1, SUBLANES, SUBLANES, LANES), lambda bi, i: after(bi, i) + (0,)),
            modspec, modspec, vec,
            pl.BlockSpec((1, d, 3 * d), lambda bi, i: (0, 0, 0), pipeline_mode=pl.Buffered(1)),
            pl.BlockSpec((SC_K, d), lambda bi, i: (0, 0)),
            pl.BlockSpec((1, d, d), lambda bi, i: (0, 0, 0), pipeline_mode=pl.Buffered(1)),
            vec,
            pl.BlockSpec((d, LANES), lambda bi, i: (0, 0)),
        ],
        out_specs=out_specs,
        out_shape=out_shape,
        compiler_params=_params("arbitrary", "arbitrary"),
        name="mixer1",
    )(h, h, h, moe, moe, moe, mods_prev, mods, g1, w_in, conv_w, w_out, g2, w_r)


def _stack_chunks(x):
    n = x.shape[1]
    return jnp.concatenate([x[:, c * ROUTER_CHUNK:(c + 1) * ROUTER_CHUNK] for c in range(n // ROUTER_CHUNK)], axis=0)


def _exclusive_rank(flags, utri, chunk_lt):
    incl = jnp.dot(flags.astype(BF16), utri, preferred_element_type=F32)
    tot = jnp.broadcast_to(incl[:, ROUTER_CHUNK - 1:ROUTER_CHUNK], incl.shape)
    base = jnp.dot(chunk_lt, tot.astype(BF16), preferred_element_type=F32)
    return incl - flags + base


def _router_body(lg_ref, utri_ref, lt_ref, blk_ref, idx_ref, aff_ref, loc_scr, tot_scr, end_scr, *, cap):
    n = lg_ref.shape[1]
    n_chunks = n // ROUTER_CHUNK
    lg = lg_ref[0]
    lane = lax.broadcasted_iota(jnp.int32, lg.shape, 1)
    valid = lane < N_EXPERTS
    x = jnp.where(valid, lg, -jnp.inf)
    ex = jnp.where(valid, jnp.exp(x - jnp.max(x, axis=-1, keepdims=True)), 0.0)
    aff = ex / jnp.sum(ex, axis=-1, keepdims=True)

    aff_t = aff.T[:N_EXPERTS, :]
    aff_ref[0] = aff_t

    def count_ge(t):
        return jnp.sum(jnp.where(aff_t >= t, 1.0, 0.0), axis=-1, keepdims=True)

    def bit_step(i, bits):
        cand = bits | jnp.left_shift(jnp.int32(1), 30 - i)
        return jnp.where(count_ge(lax.bitcast_convert_type(cand, F32)) >= cap, cand, bits)

    bits = lax.fori_loop(0, 31, bit_step, jnp.zeros((N_EXPERTS, 1), jnp.int32))

    def refine(i, lo_hi):
        lo, hi = lo_hi
        mid = (lo + hi) * 0.5
        ok = count_ge(mid) >= cap
        return jnp.where(ok, mid, lo), jnp.where(ok, hi, mid)

    thr, _ = lax.fori_loop(0, REFINE_STEPS, refine,
                           (lax.bitcast_convert_type(bits, F32), lax.bitcast_convert_type(bits + 1, F32)))
    gt = jnp.where(aff_t > thr, 1.0, 0.0)
    eq = jnp.where(aff_t == thr, 1.0, 0.0)
    need = cap - jnp.sum(gt, axis=-1, keepdims=True)
    utri = utri_ref[...]
    chunk_lt = lt_ref[...]
    gt_s = _stack_chunks(gt)
    eq_s = _stack_chunks(eq)
    need_s = jnp.concatenate([need] * n_chunks, axis=0)
    sel = jnp.where((gt_s > 0) | ((eq_s > 0) & (_exclusive_rank(eq_s, utri, chunk_lt) < need_s)), 1.0, 0.0)

    loc_scr[...] = jnp.dot(sel.astype(BF16), utri, preferred_element_type=F32).astype(BF16)
    sel_t = jnp.concatenate([sel[c * N_EXPERTS:(c + 1) * N_EXPERTS, :] for c in range(n_chunks)], axis=1)
    tot = jnp.dot(sel_t.astype(BF16), blk_ref[...], preferred_element_type=F32)
    tot_scr[...] = tot
    end_scr[...] = jnp.dot(tot.astype(BF16), utri, preferred_element_type=F32)

    slot = lax.broadcasted_iota(jnp.int32, (cap, 1), 0).astype(F32)
    row_id = lax.broadcasted_iota(jnp.int32, (cap, n_chunks * N_EXPERTS), 1)

    def one_expert(e, carry):
        ends = end_scr[pl.ds(e, 1), :]
        before = ends <= slot
        chunk = jnp.sum(jnp.where(before, 1.0, 0.0), axis=-1, keepdims=True)
        base = jnp.sum(jnp.where(before, tot_scr[pl.ds(e, 1), :], 0.0), axis=-1, keepdims=True)
        pick = jnp.where(row_id == chunk.astype(jnp.int32) * N_EXPERTS + e, 1.0, 0.0).astype(BF16)
        counts = jnp.dot(pick, loc_scr[...], preferred_element_type=F32)
        inside = jnp.sum(jnp.where(counts <= slot - base, 1.0, 0.0), axis=-1, keepdims=True)
        idx_ref[0, e] = (chunk * ROUTER_CHUNK + inside).astype(jnp.int32)
        return carry

    lax.fori_loop(0, N_EXPERTS, one_expert, 0)


def _router(logits, utri, chunk_lt, chunk_of_token, cap):
    b, n, _ = logits.shape
    rows = (n // ROUTER_CHUNK) * N_EXPERTS
    assert n // ROUTER_CHUNK <= LANES
    return pl.pallas_call(
        functools.partial(_router_body, cap=cap),
        grid=(b,),
        in_specs=[
            pl.BlockSpec((1, n, LANES), lambda bi: (bi, 0, 0)),
            pl.BlockSpec((ROUTER_CHUNK, ROUTER_CHUNK), lambda bi: (0, 0)),
            pl.BlockSpec((rows, rows), lambda bi: (0, 0)),
            pl.BlockSpec((n, LANES), lambda bi: (0, 0)),
        ],
        out_specs=[
            pl.BlockSpec((1, N_EXPERTS, cap, 1), lambda bi: (bi, 0, 0, 0)),
            pl.BlockSpec((1, N_EXPERTS, n), lambda bi: (bi, 0, 0)),
        ],
        out_shape=[
            jax.ShapeDtypeStruct((b, N_EXPERTS, cap, 1), jnp.int32),
            jax.ShapeDtypeStruct((b, N_EXPERTS, n), F32),
        ],
        scratch_shapes=[pltpu.VMEM((rows, ROUTER_CHUNK), BF16), pltpu.VMEM((N_EXPERTS, LANES), F32),
                        pltpu.VMEM((N_EXPERTS, LANES), F32)],
        compiler_params=_params("arbitrary"),
        name="router",
    )(logits, utri, chunk_lt, chunk_of_token)


GATHER_UNROLL = 16
MOE_EXPERTS_PER_STEP = 2


def _gather_body(idx_ref, hm_ref, o_ref):
    cap = o_ref.shape[2]

    for k in range(o_ref.shape[1]):
        def group(g, carry, k=k):
            base = pl.multiple_of(g * GATHER_UNROLL, GATHER_UNROLL)
            halves = []
            for half in range(GATHER_UNROLL // SUBLANES):
                tiles = [hm_ref[0, idx_ref[k, 0, base + half * SUBLANES + u]][None] for u in range(SUBLANES)]
                halves.append(_tile_transpose8(tiles))
            rows = jnp.concatenate([jnp.concatenate([h[a][0] for h in halves], axis=0) for a in range(SUBLANES)],
                                   axis=1)
            o_ref[0, k, pl.ds(base, GATHER_UNROLL), :] = rows.astype(BF16)
            return carry

        lax.fori_loop(0, cap // GATHER_UNROLL, group, 0)


def _gather(idx, hm):
    b, n = hm.shape[:2]
    cap = idx.shape[-1]
    d = SUBLANES * LANES
    eps = MOE_EXPERTS_PER_STEP
    steps = N_EXPERTS // eps
    return pl.pallas_call(
        _gather_body,
        grid=(b, steps),
        in_specs=[
            pl.BlockSpec((eps, 1, cap), lambda bi, e: (bi * steps + e, 0, 0), memory_space=pltpu.SMEM),
            pl.BlockSpec((1, n, SUBLANES, LANES), lambda bi, e: (bi, 0, 0, 0)),
        ],
        out_specs=pl.BlockSpec((1, eps, cap, d), lambda bi, e: (bi, e, 0, 0)),
        out_shape=jax.ShapeDtypeStruct((b, N_EXPERTS, cap, d), BF16),
        compiler_params=_params("arbitrary", "arbitrary"),
        name="moe_gather",
    )(idx, hm)


FFN_SAMPLES_PER_STEP = 2


def _ffn_body(x_ref, wg_ref, wu_ref, wd_ref, o_ref):
    cap = x_ref.shape[2]
    x = jnp.concatenate([x_ref[s, 0] for s in range(x_ref.shape[0])], axis=0)
    hg = jnp.dot(x, wg_ref[0, 0].astype(BF16), preferred_element_type=F32)
    hu = jnp.dot(x, wu_ref[0, 0].astype(BF16), preferred_element_type=F32)
    y = jnp.dot((_silu(hg) * hu).astype(BF16), wd_ref[0, 0].astype(BF16), preferred_element_type=F32)
    for s in range(x_ref.shape[0]):
        o_ref[s, 0] = y[s * cap:(s + 1) * cap]


def _expert_ffn(xs, w_gate, w_up, w_down, layer):
    b, n_e, cap, d = xs.shape
    f = w_gate.shape[-1]
    sps = FFN_SAMPLES_PER_STEP if b % FFN_SAMPLES_PER_STEP == 0 else 1
    rows = pl.BlockSpec((sps, 1, cap, d), lambda e, bi: (bi, e, 0, 0))
    return pl.pallas_call(
        _ffn_body,
        grid=(n_e, b // sps),
        in_specs=[
            rows,
            pl.BlockSpec((1, 1, d, f), lambda e, bi: (layer, e, 0, 0)),
            pl.BlockSpec((1, 1, d, f), lambda e, bi: (layer, e, 0, 0)),
            pl.BlockSpec((1, 1, f, d), lambda e, bi: (layer, e, 0, 0)),
        ],
        out_specs=rows,
        out_shape=jax.ShapeDtypeStruct(xs.shape, F32),
        compiler_params=_params("arbitrary", "arbitrary"),
        name="moe_ffn",
    )(xs, w_gate, w_up, w_down)


SCATTER_UNROLL = 8


def _scatter_body(idx_ref, aff_ref, ys_ref, o_ref):
    cap = ys_ref.shape[2]

    @pl.when(pl.program_id(1) == 0)
    def _():
        o_ref[...] = jnp.zeros(o_ref.shape, F32)

    for k in range(ys_ref.shape[1]):
        def group(g, carry, k=k):
            base = pl.multiple_of(g * SCATTER_UNROLL, SCATTER_UNROLL)
            tiles = []
            for part in range(SCATTER_UNROLL // SUBLANES):
                y = ys_ref[0, k, pl.ds(base + part * SUBLANES, SUBLANES), :]
                tiles += _tile_transpose8([y[:, a * LANES:(a + 1) * LANES][None] for a in range(SUBLANES)])
            rows = [idx_ref[k, 0, base + u] for u in range(SCATTER_UNROLL)]
            sums = [o_ref[0, rows[u]] + tiles[u][0] * aff_ref[k, 0, rows[u]] for u in range(SCATTER_UNROLL)]
            for u in range(SCATTER_UNROLL):
                o_ref[0, rows[u]] = sums[u]
            return carry

        lax.fori_loop(0, cap // SCATTER_UNROLL, group, 0)


def _scatter(idx, aff, ys, n):
    b, n_e, cap, d = ys.shape
    assert SCATTER_UNROLL % SUBLANES == 0 and cap % SCATTER_UNROLL == 0 and d == SUBLANES * LANES
    eps = MOE_EXPERTS_PER_STEP
    steps = n_e // eps
    return pl.pallas_call(
        _scatter_body,
        grid=(b, steps),
        in_specs=[
            pl.BlockSpec((eps, 1, cap), lambda bi, e: (bi * steps + e, 0, 0), memory_space=pltpu.SMEM),
            pl.BlockSpec((eps, 1, n), lambda bi, e: (bi * steps + e, 0, 0), memory_space=pltpu.SMEM),
            pl.BlockSpec((1, eps, cap, d), lambda bi, e: (bi, e, 0, 0)),
        ],
        out_specs=pl.BlockSpec((1, n, SUBLANES, LANES), lambda bi, e: (bi, 0, 0, 0)),
        out_shape=jax.ShapeDtypeStruct((b, n, SUBLANES, LANES), F32),
        compiler_params=_params("arbitrary", "arbitrary"),
        name="moe_scatter",
    )(idx, aff, ys)


def _ec_moe(hm, logits, w_gate, w_up, w_down, layer, tables):
    b, n = hm.shape[:2]
    cap = max(1, EC_CAPACITY * n // N_EXPERTS)
    idx, aff = _router(logits, *tables, cap)
    idx = idx.reshape(b * N_EXPERTS, 1, cap)
    xs = _gather(idx, hm)
    ys = _expert_ffn(xs, w_gate, w_up, w_down, layer)
    return _scatter(idx, aff.reshape(b * N_EXPERTS, 1, n), ys, n)


def _final_body(h_ref, moe_ref, mod_ref, g_ref, o_ref):
    h = h_ref[0] + mod_ref[0, 5:6, :] * _tiles_to_rows(moe_ref[0])
    o_ref[0] = h * lax.rsqrt(jnp.mean(h * h, axis=-1, keepdims=True) + EPS) * g_ref[...]


def _final(h, moe, mods, g, tm=512):
    b, n, d = h.shape
    tok = pl.BlockSpec((1, tm, d), lambda bi, i: (bi, i, 0))
    return pl.pallas_call(
        _final_body,
        grid=(b, n // tm),
        in_specs=[tok, _tile_spec(tm, d), pl.BlockSpec((1, N_MOD, d), lambda bi, i: (bi, 0, 0)),
                  pl.BlockSpec((1, d), lambda bi, i: (0, 0))],
        out_specs=tok,
        out_shape=jax.ShapeDtypeStruct((b, n, d), F32),
        compiler_params=_params("arbitrary", "arbitrary"),
        name="final_norm",
    )(h, moe, mods, g)


def _rope_tables(n):
    rows = n // GRID_W
    row = np.repeat(np.arange(rows, dtype=np.float64), GRID_W)
    col = np.tile(np.arange(GRID_W, dtype=np.float64), rows)
    axis_dim = HEAD_DIM // 2
    inv = ROPE_THETA ** (-np.arange(0, axis_dim, 2, dtype=np.float64) / axis_dim)
    ang = np.concatenate([row[:, None] * inv, col[:, None] * inv], axis=-1)
    cos = np.cos(ang)
    sin = np.sin(ang)
    cos64 = np.concatenate([cos, cos], axis=-1)
    sin64 = np.concatenate([-sin, sin], axis=-1)
    reps = LANES // HEAD_DIM
    return jnp.asarray(np.tile(cos64, (1, reps)), F32), jnp.asarray(np.tile(sin64, (1, reps)), F32)


def _router_tables(n):
    r = np.arange(ROUTER_CHUNK)
    utri = r[:, None] <= r[None, :]
    rows = np.arange((n // ROUTER_CHUNK) * N_EXPERTS)
    same_e = (rows[:, None] % N_EXPERTS) == (rows[None, :] % N_EXPERTS)
    earlier = (rows[None, :] // N_EXPERTS) < (rows[:, None] // N_EXPERTS)
    chunk_of_token = (np.arange(n)[:, None] // ROUTER_CHUNK) == np.arange(LANES)[None, :]
    return tuple(jnp.asarray(t.astype(np.float32), BF16) for t in (utri, same_e & earlier, chunk_of_token))


def _head_ones():
    r = np.arange(MXU_DIM) // HEAD_DIM
    return jnp.asarray((r[:, None] == r[None, :]).astype(np.float32), BF16)


def kernel(x, c, ctx, c_ctx, ada_w, ada_b, norm1_g, norm2_g, ev_w_in, ev_q_g, ev_k_g, ev_conv_w, ev_conv_b,
           ev_ln_g, ev_ln_b, ev_w_out, sc_w_in, sc_conv_w, sc_w_out, moe_w_r, moe_w_gate, moe_w_up,
           moe_w_down, final_g):
    b, n, d = x.shape
    depth = ada_w.shape[0]
    assert depth == 2 and b < SUBLANES and n % ROUTER_CHUNK == 0

    cos2, sin2 = _rope_tables(n)
    tables = _router_tables(n)
    ones_bd = _head_ones()

    cvecs = jnp.zeros((SUBLANES, d), F32).at[:b].set(c).at[b].set(c_ctx)
    mods = _ada_mod(cvecs, ada_w, ada_b).reshape(depth, SUBLANES, N_MOD, d)
    w_r = jnp.pad(moe_w_r, ((0, 0), (0, 0), (0, LANES - N_EXPERTS))).astype(BF16)

    qg = jnp.tile(ev_q_g[0], N_Q_HEADS)[None, :]
    kg = jnp.tile(ev_k_g[0], N_KV_HEADS)[None, :]
    qt, k, vt, glu = _inproj0(x, mods[0], norm1_g[0:1], ev_w_in, qg, kg, cos2, sin2, ones_bd)
    k_ctx, vt_ctx = _ctxkv(ctx, mods[0], b, norm1_g[0:1], ev_w_in, kg, ones_bd)
    attn = _attention(qt, k_ctx, k, vt_ctx, vt)
    h, hm, logits = _outproj0(attn, glu, ev_conv_w[0], ev_conv_b[0:1], ev_ln_g[0:1], ev_ln_b[0:1],
                              ev_w_out[0].astype(BF16), x, mods[0], norm2_g[0:1], w_r[0])
    moe0 = _ec_moe(hm, logits, moe_w_gate, moe_w_up, moe_w_down, 0, tables)

    h, hm, logits = _mixer1(h, moe0, mods[0], mods[1], norm1_g[1:2], sc_w_in, sc_conv_w[0], sc_w_out,
                            norm2_g[1:2], w_r[1])
    moe1 = _ec_moe(hm, logits, moe_w_gate, moe_w_up, moe_w_down, 1, tables)

    return _final(h, moe1, mods[1], final_g[None, :])
```

```python
import functools

import jax
import jax.numpy as jnp
import numpy as np
from jax import lax
from jax.experimental import pallas as pl
from jax.experimental.pallas import tpu as pltpu

F32 = jnp.float32
BF16 = jnp.bfloat16

HEAD_DIM = 64
N_Q_HEADS = 8
N_KV_HEADS = 2
GRID_W = 64
ROPE_THETA = 10000.0
CONF_K = 31
SC_K = 3
N_EXPERTS = 16
EC_CAPACITY = 2
N_MOD = 6
EPS = 1e-6

ATTN_W = N_Q_HEADS * HEAD_DIM
KV_W = N_KV_HEADS * HEAD_DIM
Q_PER_KV = N_Q_HEADS // N_KV_HEADS
QK_SCALE = HEAD_DIM ** -0.5 * 1.4426950408889634

LANES = 128
SUBLANES = 8
MXU_DIM = 256
VMEM_LIMIT_BYTES = 60000 * 1024

CONV_HALO = 16
ROUTER_CHUNK = LANES
REFINE_STEPS = 24
INPROJ_PARTS = 2
MIXER_PARTS = 1


def _params(*sem):
    return pltpu.CompilerParams(dimension_semantics=sem, vmem_limit_bytes=VMEM_LIMIT_BYTES)


def _norm_mod(x, g, shift, scale):
    ms = jnp.mean(x * x, axis=-1, keepdims=True)
    y = x * lax.rsqrt(ms + EPS) * g
    return y * (1.0 + scale) + shift


def _head_sumsq(x, ones_blockdiag):
    return jnp.dot((x * x).astype(BF16), ones_blockdiag, preferred_element_type=F32)


def _swap_half(x):
    w = x.shape[-1]
    lane = lax.broadcasted_iota(jnp.int32, x.shape, 1)
    first = (lane % HEAD_DIM) < (HEAD_DIM // 2)
    return jnp.where(first, pltpu.roll(x, w - HEAD_DIM // 2, axis=1), pltpu.roll(x, HEAD_DIM // 2, axis=1))


def _silu(x):
    return x * jax.nn.sigmoid(x)


def _tile_transpose8(vs):
    sub = lax.broadcasted_iota(jnp.int32, vs[0].shape, 1)
    for d in (4, 2, 1):
        keep = (sub & d) == 0
        out = list(vs)
        for i in range(SUBLANES):
            if i & d == 0:
                a, b = vs[i], vs[i + d]
                out[i] = jnp.where(keep, a, pltpu.roll(b, d, axis=1))
                out[i + d] = jnp.where(keep, pltpu.roll(a, SUBLANES - d, axis=1), b)
        vs = out
    return vs


def _rows_to_tiles(x):
    r = x.shape[0]
    vs = [x[:, a * LANES:(a + 1) * LANES].reshape(r // SUBLANES, SUBLANES, LANES) for a in range(SUBLANES)]
    return jnp.stack(_tile_transpose8(vs), axis=1).reshape(r, SUBLANES, LANES)


def _tiles_to_rows(x3):
    r = x3.shape[0]
    x4 = x3.reshape(r // SUBLANES, SUBLANES, SUBLANES, LANES)
    vs = _tile_transpose8([x4[:, j] for j in range(SUBLANES)])
    return jnp.concatenate([v.reshape(r, LANES) for v in vs], axis=1)


def _store_vt_with_ones(vt_ref, cols, v):
    vt = v.T
    ones = jnp.ones((HEAD_DIM, v.shape[0]), F32)
    for j in range(N_KV_HEADS):
        vt_ref[0, j, :, cols] = jnp.concatenate([vt[j * HEAD_DIM:(j + 1) * HEAD_DIM, :], ones], axis=0).astype(BF16)


def _ada_body(c_ref, w_ref, b_ref, o_ref):
    s = _silu(c_ref[...]).astype(BF16)
    o_ref[0] = jnp.dot(s, w_ref[0].astype(BF16), preferred_element_type=F32) + b_ref[0]


def _ada_mod(cvecs, ada_w, ada_b):
    n_layers, d, n_out = ada_w.shape
    tn = n_out // 4
    return pl.pallas_call(
        _ada_body,
        grid=(n_layers, n_out // tn),
        in_specs=[
            pl.BlockSpec((SUBLANES, d), lambda l, j: (0, 0)),
            pl.BlockSpec((1, d, tn), lambda l, j: (l, 0, j)),
            pl.BlockSpec((1, 1, tn), lambda l, j: (l, 0, j)),
        ],
        out_specs=pl.BlockSpec((1, SUBLANES, tn), lambda l, j: (l, 0, j)),
        out_shape=jax.ShapeDtypeStruct((n_layers, SUBLANES, n_out), F32),
        compiler_params=_params("arbitrary", "arbitrary"),
        name="ada_mod",
    )(cvecs, ada_w, ada_b.reshape(n_layers, 1, n_out))


def _inproj0_body(h_ref, mod_ref, g1_ref, w_ref, qg_ref, kg_ref, cos_ref, sin_ref, ones_ref,
                  qt_ref, k_ref, vt_ref, glu_ref):
    ones = ones_ref[...]
    tm = h_ref.shape[1]
    part = tm // INPROJ_PARTS
    for p in range(INPROJ_PARTS):
        rows = slice(p * part, (p + 1) * part)
        hn = _norm_mod(h_ref[0, rows, :], g1_ref[...], mod_ref[0, 0:1, :], mod_ref[0, 1:2, :])
        proj = jnp.dot(hn.astype(BF16), w_ref[0].astype(BF16), preferred_element_type=F32)
        cos2 = cos_ref[rows, :]
        sin2 = sin_ref[rows, :]

        q = proj[:, :ATTN_W]
        ssq = jnp.concatenate([_head_sumsq(q[:, :MXU_DIM], ones), _head_sumsq(q[:, MXU_DIM:], ones)], axis=1)
        qn = q * lax.rsqrt(ssq * (1.0 / HEAD_DIM) + EPS) * qg_ref[...]
        cos = jnp.concatenate([cos2] * (ATTN_W // LANES), axis=1)
        sin = jnp.concatenate([sin2] * (ATTN_W // LANES), axis=1)
        qr = (qn * cos + _swap_half(qn) * sin) * QK_SCALE
        qt_ref[0, :, rows] = qr.T.astype(BF16)

        k = proj[:, ATTN_W:ATTN_W + KV_W]
        kn = k * lax.rsqrt(_head_sumsq(k, ones[:KV_W, :KV_W]) * (1.0 / HEAD_DIM) + EPS) * kg_ref[...]
        k_ref[0, rows, :] = (kn * cos2 + _swap_half(kn) * sin2).astype(BF16)
        _store_vt_with_ones(vt_ref, rows, proj[:, ATTN_W + KV_W:ATTN_W + 2 * KV_W])

        c0 = ATTN_W + 2 * KV_W
        cc = (proj.shape[1] - c0) // 2
        glu_ref[0, rows, :] = proj[:, c0:c0 + cc] * jax.nn.sigmoid(proj[:, c0 + cc:])


def _inproj0(h, mods, g1, w_in, qg, kg, cos2, sin2, ones_bd, tm=512):
    b, n, d = h.shape
    n_in = w_in.shape[-1]
    conv_ch = (n_in - ATTN_W - 2 * KV_W) // 2
    return pl.pallas_call(
        _inproj0_body,
        grid=(b, n // tm),
        in_specs=[
            pl.BlockSpec((1, tm, d), lambda bi, i: (bi, i, 0)),
            pl.BlockSpec((1, N_MOD, d), lambda bi, i: (bi, 0, 0)),
            pl.BlockSpec((1, d), lambda bi, i: (0, 0)),
            pl.BlockSpec((1, d, n_in), lambda bi, i: (0, 0, 0), pipeline_mode=pl.Buffered(1)),
            pl.BlockSpec((1, ATTN_W), lambda bi, i: (0, 0)),
            pl.BlockSpec((1, KV_W), lambda bi, i: (0, 0)),
            pl.BlockSpec((tm, LANES), lambda bi, i: (i, 0)),
            pl.BlockSpec((tm, LANES), lambda bi, i: (i, 0)),
            pl.BlockSpec((MXU_DIM, MXU_DIM), lambda bi, i: (0, 0)),
        ],
        out_specs=[
            pl.BlockSpec((1, ATTN_W, tm), lambda bi, i: (bi, 0, i)),
            pl.BlockSpec((1, tm, KV_W), lambda bi, i: (bi, i, 0)),
            pl.BlockSpec((1, N_KV_HEADS, 2 * HEAD_DIM, tm), lambda bi, i: (bi, 0, 0, i)),
            pl.BlockSpec((1, tm, conv_ch), lambda bi, i: (bi, i, 0)),
        ],
        out_shape=[
            jax.ShapeDtypeStruct((b, ATTN_W, n), BF16),
            jax.ShapeDtypeStruct((b, n, KV_W), BF16),
            jax.ShapeDtypeStruct((b, N_KV_HEADS, 2 * HEAD_DIM, n), BF16),
            jax.ShapeDtypeStruct((b, n, conv_ch), F32),
        ],
        compiler_params=_params("arbitrary", "arbitrary"),
        name="inproj0",
    )(h, mods, g1, w_in, qg, kg, cos2, sin2, ones_bd)


def _ctxkv_body(x_ref, mod_ref, g1_ref, w_ref, kg_ref, ones_ref, k_ref, vt_ref):
    hn = _norm_mod(x_ref[0], g1_ref[...], mod_ref[0, 0:1, :], mod_ref[0, 1:2, :])
    proj = jnp.dot(hn.astype(BF16), w_ref[0].astype(BF16), preferred_element_type=F32)
    k = proj[:, :KV_W]
    kn = k * lax.rsqrt(_head_sumsq(k, ones_ref[...][:KV_W, :KV_W]) * (1.0 / HEAD_DIM) + EPS) * kg_ref[...]
    k_ref[0] = kn.astype(BF16)
    _store_vt_with_ones(vt_ref, slice(None), proj[:, KV_W:])


def _ctxkv(ctx, mods, ctx_row, g1, w_in, kg, ones_bd):
    b, t, d = ctx.shape
    assert ATTN_W % (2 * KV_W) == 0
    kv_block = ATTN_W // (2 * KV_W)
    return pl.pallas_call(
        _ctxkv_body,
        grid=(b,),
        in_specs=[
            pl.BlockSpec((1, t, d), lambda bi: (bi, 0, 0)),
            pl.BlockSpec((1, N_MOD, d), lambda bi: (ctx_row, 0, 0)),
            pl.BlockSpec((1, d), lambda bi: (0, 0)),
            pl.BlockSpec((1, d, 2 * KV_W), lambda bi: (0, 0, kv_block)),
            pl.BlockSpec((1, KV_W), lambda bi: (0, 0)),
            pl.BlockSpec((MXU_DIM, MXU_DIM), lambda bi: (0, 0)),
        ],
        out_specs=[
            pl.BlockSpec((1, t, KV_W), lambda bi: (bi, 0, 0)),
            pl.BlockSpec((1, N_KV_HEADS, 2 * HEAD_DIM, t), lambda bi: (bi, 0, 0, 0)),
        ],
        out_shape=[
            jax.ShapeDtypeStruct((b, t, KV_W), BF16),
            jax.ShapeDtypeStruct((b, N_KV_HEADS, 2 * HEAD_DIM, t), BF16),
        ],
        compiler_params=_params("arbitrary"),
        name="ctx_kv",
    )(ctx, mods, g1, w_in, kg, ones_bd)


ATTN_KEY_CHUNK = 512


def _attn_body(qt_ref, kc_ref, k_ref, vtc_ref, vt_ref, o_ref, s0, s1, p0, p1, ot):
    s_bufs = (s0, s1)
    p_bufs = (p0, p1)
    t_ctx = kc_ref.shape[1]
    n = k_ref.shape[1]
    tq = qt_ref.shape[2]

    def column_max(run, sc):
        part = jnp.max(sc.reshape(sc.shape[0] // SUBLANES, SUBLANES, tq), axis=0)
        return part if run is None else jnp.maximum(run, part)

    assert t_ctx % SUBLANES == 0 and n % ATTN_KEY_CHUNK == 0
    chunks = [(0, t_ctx, None)] + [(t_ctx + lo, ATTN_KEY_CHUNK, lo) for lo in range(0, n, ATTN_KEY_CHUNK)]

    def padded_query(h):
        qt = qt_ref[0, h * HEAD_DIM:(h + 1) * HEAD_DIM, :]
        zeros = jnp.zeros_like(qt)
        return jnp.concatenate([qt, zeros] if h // Q_PER_KV == 0 else [zeros, qt], axis=0)

    def score_chunk(h, qpad, run, chunk):
        row, size, lo = chunk
        keys = kc_ref[0] if lo is None else k_ref[0, lo:lo + size, :]
        sc = jnp.dot(keys, qpad, preferred_element_type=F32)
        s_bufs[h % 2][row:row + size, :] = sc
        return column_max(run, sc)

    def exp_chunk(h, m, chunk):
        row, size, _ = chunk
        p_bufs[h % 2][row:row + size, :] = jnp.exp2(s_bufs[h % 2][row:row + size, :] - m).astype(BF16)

    def value_chunk(h, acc, chunk):
        row, size, lo = chunk
        j = h // Q_PER_KV
        vt = vtc_ref[0, j] if lo is None else vt_ref[0, j, :, lo:lo + size]
        part = jnp.dot(vt, p_bufs[h % 2][row:row + size, :], preferred_element_type=F32)
        return part if acc is None else acc + part

    m = None
    for h in range(-1, N_Q_HEADS + 1):
        qpad = padded_query(h + 1) if h + 1 < N_Q_HEADS else None
        run = acc = None
        for chunk in chunks:
            if qpad is not None:
                run = score_chunk(h + 1, qpad, run, chunk)
            if 0 <= h - 1:
                acc = value_chunk(h - 1, acc, chunk)
            if 0 <= h < N_Q_HEADS:
                exp_chunk(h, m, chunk)
        if acc is not None:
            ot[(h - 1) * HEAD_DIM:h * HEAD_DIM, :] = acc[:HEAD_DIM, :] / acc[HEAD_DIM:HEAD_DIM + 1, :]
        m = jnp.max(run, axis=0, keepdims=True) if run is not None else None
    o_ref[0] = ot[...].T.astype(BF16)


def _attention(qt, k_ctx, k, vt_ctx, vt, tq=512):
    b, width, n = qt.shape
    t_ctx = k_ctx.shape[1]
    t_all = t_ctx + n
    return pl.pallas_call(
        _attn_body,
        grid=(b, n // tq),
        scratch_shapes=[pltpu.VMEM((t_all, tq), F32), pltpu.VMEM((t_all, tq), F32),
                        pltpu.VMEM((t_all, tq), BF16), pltpu.VMEM((t_all, tq), BF16),
                        pltpu.VMEM((width, tq), F32)],
        in_specs=[
            pl.BlockSpec((1, width, tq), lambda bi, i: (bi, 0, i)),
            pl.BlockSpec((1, t_ctx, KV_W), lambda bi, i: (bi, 0, 0)),
            pl.BlockSpec((1, n, KV_W), lambda bi, i: (bi, 0, 0)),
            pl.BlockSpec((1, N_KV_HEADS, 2 * HEAD_DIM, t_ctx), lambda bi, i: (bi, 0, 0, 0)),
            pl.BlockSpec((1, N_KV_HEADS, 2 * HEAD_DIM, n), lambda bi, i: (bi, 0, 0, 0)),
        ],
        out_specs=pl.BlockSpec((1, tq, width), lambda bi, i: (bi, i, 0)),
        out_shape=jax.ShapeDtypeStruct((b, n, width), BF16),
        compiler_params=_params("arbitrary", "arbitrary"),
        name="attention",
    )(qt, k_ctx, k, vt_ctx, vt)


CONV_ROWS = 64


def _conformer_rows(window, w_ref, cb_ref, lg_ref, lb_ref):
    tm = window.shape[0] - 2 * CONV_HALO
    ch = window.shape[1]
    win = CONV_ROWS + 2 * CONV_HALO
    off = CONV_HALO - CONF_K // 2
    cols = []
    for t in range(ch // LANES):
        lanes = slice(t * LANES, (t + 1) * LANES)
        blocks = []
        for r in range(tm // CONV_ROWS):
            x = window[r * CONV_ROWS:r * CONV_ROWS + win, lanes]
            acc = jnp.zeros((CONV_ROWS, LANES), F32)
            for sub in range(SUBLANES):
                xs = x if sub == 0 else pltpu.roll(x, win - sub, axis=0)
                for a in range(2 * CONV_HALO // SUBLANES):
                    k = a * SUBLANES + sub - off
                    if 0 <= k < CONF_K:
                        acc = acc + xs[a * SUBLANES:a * SUBLANES + CONV_ROWS, :] * w_ref[k:k + 1, lanes]
            blocks.append(acc)
        cols.append(jnp.concatenate(blocks, axis=0))
    y = jnp.concatenate(cols, axis=1) + cb_ref[...]
    mu = jnp.mean(y, axis=-1, keepdims=True)
    yc = y - mu
    var = jnp.mean(yc * yc, axis=-1, keepdims=True)
    return _silu(yc * lax.rsqrt(var + EPS) * lg_ref[...] + lb_ref[...])


def _residual_router(mix, h, mod_ref, g2_ref, wr_ref, h_out, hm_out, lg_out, rows=slice(None)):
    h1 = h + mod_ref[0, 2:3, :] * mix
    h_out[0, rows] = h1
    hm = _norm_mod(h1, g2_ref[...], mod_ref[0, 3:4, :], mod_ref[0, 4:5, :])
    hm_out[0, rows] = _rows_to_tiles(hm)
    lg_out[0, rows] = jnp.dot(hm.astype(BF16), wr_ref[...], preferred_element_type=F32)


def _outproj0_body(attn_ref, glu_ref, glup_ref, glun_ref, cw_ref, cb_ref, lg_ref, lb_ref, w_ref,
                   h_ref, mod_ref, g2_ref, wr_ref, h_out, hm_out, lg_out):
    i = pl.program_id(1)
    last = pl.num_programs(1) - 1
    before = jnp.where(i > 0, glup_ref[0], 0.0)
    after = jnp.where(i < last, glun_ref[0], 0.0)
    conf = _conformer_rows(jnp.concatenate([before, glu_ref[0], after], axis=0), cw_ref, cb_ref, lg_ref, lb_ref)
    a = jnp.concatenate([attn_ref[0], conf.astype(BF16)], axis=1)
    mix = jnp.dot(a, w_ref[...], preferred_element_type=F32)
    _residual_router(mix, h_ref[0], mod_ref, g2_ref, wr_ref, h_out, hm_out, lg_out)


def _tile_spec(tm, d):
    assert d == SUBLANES * LANES
    return pl.BlockSpec((1, tm, SUBLANES, LANES), lambda bi, i: (bi, i, 0, 0))


def _token_out_specs(b, n, d, tm):
    specs = [
        pl.BlockSpec((1, tm, d), lambda bi, i: (bi, i, 0)),
        _tile_spec(tm, d),
        pl.BlockSpec((1, tm, LANES), lambda bi, i: (bi, i, 0)),
    ]
    shapes = [
        jax.ShapeDtypeStruct((b, n, d), F32),
        jax.ShapeDtypeStruct((b, n, SUBLANES, d // SUBLANES), F32),
        jax.ShapeDtypeStruct((b, n, LANES), F32),
    ]
    return specs, shapes


def _outproj0(attn, glu, conv_w, conv_b, ln_g, ln_b, w_out, h, mods, g2, w_r, tm=512):
    b, n, d = h.shape
    ch = glu.shape[-1]
    per = tm // CONV_HALO
    n_halo = n // CONV_HALO
    vec = pl.BlockSpec((1, ch), lambda bi, i: (0, 0))
    out_specs, out_shape = _token_out_specs(b, n, d, tm)
    return pl.pallas_call(
        _outproj0_body,
        grid=(b, n // tm),
        in_specs=[
            pl.BlockSpec((1, tm, ATTN_W), lambda bi, i: (bi, i, 0)),
            pl.BlockSpec((1, tm, ch), lambda bi, i: (bi, i, 0)),
            pl.BlockSpec((1, CONV_HALO, ch), lambda bi, i: (bi, jnp.maximum(i * per - 1, 0), 0)),
            pl.BlockSpec((1, CONV_HALO, ch), lambda bi, i: (bi, jnp.minimum((i + 1) * per, n_halo - 1), 0)),
            pl.BlockSpec((CONF_K, ch), lambda bi, i: (0, 0)),
            vec, vec, vec,
            pl.BlockSpec((d, d), lambda bi, i: (0, 0)),
            pl.BlockSpec((1, tm, d), lambda bi, i: (bi, i, 0)),
            pl.BlockSpec((1, N_MOD, d), lambda bi, i: (bi, 0, 0)),
            pl.BlockSpec((1, d), lambda bi, i: (0, 0)),
            pl.BlockSpec((d, LANES), lambda bi, i: (0, 0)),
        ],
        out_specs=out_specs,
        out_shape=out_shape,
        compiler_params=_params("arbitrary", "arbitrary"),
        name="outproj0",
    )(attn, glu, glu, glu, conv_w, conv_b, ln_g, ln_b, w_out, h, mods, g2, w_r)


def _mixer1_body(h_ref, hp_ref, hn_ref, moe_ref, moep_ref, moen_ref, modp_ref, mod_ref, g1_ref, w_ref,
                 cw_ref, wo_ref, g2_ref, wr_ref, h_out, hm_out, lg_out):
    i = pl.program_id(1)
    last = pl.num_programs(1) - 1
    tm, d = h_ref.shape[1:]
    gate_prev = modp_ref[0, 5:6, :]

    def stream(h_blk, moe_blk, rows=slice(None)):
        return h_blk[0, rows] + gate_prev * _tiles_to_rows(moe_blk[0, rows])

    part = tm // MIXER_PARTS
    win = part + 2 * SUBLANES
    for p in range(MIXER_PARTS):
        rows = slice(p * part, (p + 1) * part)
        h = stream(h_ref, moe_ref, rows)
        above = (stream(hp_ref, moep_ref) if p == 0
                 else stream(h_ref, moe_ref, slice(p * part - SUBLANES, p * part)))
        below = (stream(hn_ref, moen_ref) if p == MIXER_PARTS - 1
                 else stream(h_ref, moe_ref, slice((p + 1) * part, (p + 1) * part + SUBLANES)))
        hn = _norm_mod(jnp.concatenate([above, h, below], axis=0), g1_ref[...], mod_ref[0, 0:1, :], mod_ref[0, 1:2, :])
        proj = jnp.dot(hn.astype(BF16), w_ref[0].astype(BF16), preferred_element_type=F32)
        z = proj[:, d:2 * d] * proj[:, 2 * d:]
        r = lax.broadcasted_iota(jnp.int32, z.shape, 0)
        if p == 0:
            z = jnp.where((r < SUBLANES) & (i == 0), 0.0, z)
        if p == MIXER_PARTS - 1:
            z = jnp.where((r >= part + SUBLANES) & (i == last), 0.0, z)
        y = (pltpu.roll(z, 1, axis=0) * cw_ref[0:1, :] + z * cw_ref[1:2, :]
             + pltpu.roll(z, win - 1, axis=0) * cw_ref[2:3, :])[SUBLANES:SUBLANES + part]
        gated = proj[SUBLANES:SUBLANES + part, :d] * y
        mix = jnp.dot(gated.astype(BF16), wo_ref[0].astype(BF16), preferred_element_type=F32)
        _residual_router(mix, h, mod_ref, g2_ref, wr_ref, h_out, hm_out, lg_out, rows)


def _mixer1(h, moe, mods_prev, mods, g1, w_in, conv_w, w_out, g2, w_r, tm=512):
    b, n, d = h.shape
    per = tm // SUBLANES
    n8 = n // SUBLANES
    before = lambda bi, i: (bi, jnp.maximum(i * per - 1, 0), 0)
    after = lambda bi, i: (bi, jnp.minimum((i + 1) * per, n8 - 1), 0)
    tok = pl.BlockSpec((1, tm, d), lambda bi, i: (bi, i, 0))
    modspec = pl.BlockSpec((1, N_MOD, d), lambda bi, i: (bi, 0, 0))
    vec = pl.BlockSpec((1, d), lambda bi, i: (0, 0))
    out_specs, out_shape = _token_out_specs(b, n, d, tm)
    return pl.pallas_call(
        _mixer1_body,
        grid=(b, n // tm),
        in_specs=[
            tok,
            pl.BlockSpec((1, SUBLANES, d), before),
            pl.BlockSpec((1, SUBLANES, d), after),
            _tile_spec(tm, d),
            pl.BlockSpec((1, SUBLANES, SUBLANES, LANES), lambda bi, i: before(bi, i) + (0,)),
            pl.BlockSpec((1, SUBLANES, SUBLANES, LANES), lambda bi, i: after(bi, i) + (0,)),
            modspec, modspec, vec,
            pl.BlockSpec((1, d, 3 * d), lambda bi, i: (0, 0, 0), pipeline_mode=pl.Buffered(1)),
            pl.BlockSpec((SC_K, d), lambda bi, i: (0, 0)),
            pl.BlockSpec((1, d, d), lambda bi, i: (0, 0, 0), pipeline_mode=pl.Buffered(1)),
            vec,
            pl.BlockSpec((d, LANES), lambda bi, i: (0, 0)),
        ],
        out_specs=out_specs,
        out_shape=out_shape,
        compiler_params=_params("arbitrary", "arbitrary"),
        name="mixer1",
    )(h, h, h, moe, moe, moe, mods_prev, mods, g1, w_in, conv_w, w_out, g2, w_r)


def _stack_chunks(x):
    n = x.shape[1]
    return jnp.concatenate([x[:, c * ROUTER_CHUNK:(c + 1) * ROUTER_CHUNK] for c in range(n // ROUTER_CHUNK)], axis=0)


def _exclusive_rank(flags, utri, chunk_lt):
    incl = jnp.dot(flags.astype(BF16), utri, preferred_element_type=F32)
    tot = jnp.broadcast_to(incl[:, ROUTER_CHUNK - 1:ROUTER_CHUNK], incl.shape)
    base = jnp.dot(chunk_lt, tot.astype(BF16), preferred_element_type=F32)
    return incl - flags + base


def _router_body(lg_ref, utri_ref, lt_ref, blk_ref, idx_ref, aff_ref, loc_scr, tot_scr, end_scr, *, cap):
    n = lg_ref.shape[1]
    n_chunks = n // ROUTER_CHUNK
    lg = lg_ref[0]
    lane = lax.broadcasted_iota(jnp.int32, lg.shape, 1)
    valid = lane < N_EXPERTS
    x = jnp.where(valid, lg, -jnp.inf)
    ex = jnp.where(valid, jnp.exp(x - jnp.max(x, axis=-1, keepdims=True)), 0.0)
    aff = ex / jnp.sum(ex, axis=-1, keepdims=True)

    aff_t = aff.T[:N_EXPERTS, :]
    aff_ref[0] = aff_t

    def count_ge(t):
        return jnp.sum(jnp.where(aff_t >= t, 1.0, 0.0), axis=-1, keepdims=True)

    def bit_step(i, bits):
        cand = bits | jnp.left_shift(jnp.int32(1), 30 - i)
        return jnp.where(count_ge(lax.bitcast_convert_type(cand, F32)) >= cap, cand, bits)

    bits = lax.fori_loop(0, 31, bit_step, jnp.zeros((N_EXPERTS, 1), jnp.int32))

    def refine(i, lo_hi):
        lo, hi = lo_hi
        mid = (lo + hi) * 0.5
        ok = count_ge(mid) >= cap
        return jnp.where(ok, mid, lo), jnp.where(ok, hi, mid)

    thr, _ = lax.fori_loop(0, REFINE_STEPS, refine,
                           (lax.bitcast_convert_type(bits, F32), lax.bitcast_convert_type(bits + 1, F32)))
    gt = jnp.where(aff_t > thr, 1.0, 0.0)
    eq = jnp.where(aff_t == thr, 1.0, 0.0)
    need = cap - jnp.sum(gt, axis=-1, keepdims=True)
    utri = utri_ref[...]
    chunk_lt = lt_ref[...]
    gt_s = _stack_chunks(gt)
    eq_s = _stack_chunks(eq)
    need_s = jnp.concatenate([need] * n_chunks, axis=0)
    sel = jnp.where((gt_s > 0) | ((eq_s > 0) & (_exclusive_rank(eq_s, utri, chunk_lt) < need_s)), 1.0, 0.0)

    loc_scr[...] = jnp.dot(sel.astype(BF16), utri, preferred_element_type=F32).astype(BF16)
    sel_t = jnp.concatenate([sel[c * N_EXPERTS:(c + 1) * N_EXPERTS, :] for c in range(n_chunks)], axis=1)
    tot = jnp.dot(sel_t.astype(BF16), blk_ref[...], preferred_element_type=F32)
    tot_scr[...] = tot
    end_scr[...] = jnp.dot(tot.astype(BF16), utri, preferred_element_type=F32)

    slot = lax.broadcasted_iota(jnp.int32, (cap, 1), 0).astype(F32)
    row_id = lax.broadcasted_iota(jnp.int32, (cap, n_chunks * N_EXPERTS), 1)

    def one_expert(e, carry):
        ends = end_scr[pl.ds(e, 1), :]
        before = ends <= slot
        chunk = jnp.sum(jnp.where(before, 1.0, 0.0), axis=-1, keepdims=True)
        base = jnp.sum(jnp.where(before, tot_scr[pl.ds(e, 1), :], 0.0), axis=-1, keepdims=True)
        pick = jnp.where(row_id == chunk.astype(jnp.int32) * N_EXPERTS + e, 1.0, 0.0).astype(BF16)
        counts = jnp.dot(pick, loc_scr[...], preferred_element_type=F32)
        inside = jnp.sum(jnp.where(counts <= slot - base, 1.0, 0.0), axis=-1, keepdims=True)
        idx_ref[0, e] = (chunk * ROUTER_CHUNK + inside).astype(jnp.int32)
        return carry

    lax.fori_loop(0, N_EXPERTS, one_expert, 0)


def _router(logits, utri, chunk_lt, chunk_of_token, cap):
    b, n, _ = logits.shape
    rows = (n // ROUTER_CHUNK) * N_EXPERTS
    assert n // ROUTER_CHUNK <= LANES
    return pl.pallas_call(
        functools.partial(_router_body, cap=cap),
        grid=(b,),
        in_specs=[
            pl.BlockSpec((1, n, LANES), lambda bi: (bi, 0, 0)),
            pl.BlockSpec((ROUTER_CHUNK, ROUTER_CHUNK), lambda bi: (0, 0)),
            pl.BlockSpec((rows, rows), lambda bi: (0, 0)),
            pl.BlockSpec((n, LANES), lambda bi: (0, 0)),
        ],
        out_specs=[
            pl.BlockSpec((1, N_EXPERTS, cap, 1), lambda bi: (bi, 0, 0, 0)),
            pl.BlockSpec((1, N_EXPERTS, n), lambda bi: (bi, 0, 0)),
        ],
        out_shape=[
            jax.ShapeDtypeStruct((b, N_EXPERTS, cap, 1), jnp.int32),
            jax.ShapeDtypeStruct((b, N_EXPERTS, n), F32),
        ],
        scratch_shapes=[pltpu.VMEM((rows, ROUTER_CHUNK), BF16), pltpu.VMEM((N_EXPERTS, LANES), F32),
                        pltpu.VMEM((N_EXPERTS, LANES), F32)],
        compiler_params=_params("arbitrary"),
        name="router",
    )(logits, utri, chunk_lt, chunk_of_token)


GATHER_UNROLL = 16
MOE_EXPERTS_PER_STEP = 1
MOE_SLOT_SPLIT = 2


def _gather_body(idx_ref, hm_ref, o_ref):
    cap = o_ref.shape[2]

    for k in range(o_ref.shape[1]):
        def group(g, carry, k=k):
            base = pl.multiple_of(g * GATHER_UNROLL, GATHER_UNROLL)
            halves = []
            for half in range(GATHER_UNROLL // SUBLANES):
                tiles = [hm_ref[0, idx_ref[k, 0, base + half * SUBLANES + u]][None] for u in range(SUBLANES)]
                halves.append(_tile_transpose8(tiles))
            rows = jnp.concatenate([jnp.concatenate([h[a][0] for h in halves], axis=0) for a in range(SUBLANES)],
                                   axis=1)
            o_ref[0, k, pl.ds(base, GATHER_UNROLL), :] = rows.astype(BF16)
            return carry

        lax.fori_loop(0, cap // GATHER_UNROLL, group, 0)


def _gather(idx, hm):
    b, n = hm.shape[:2]
    groups, cap = idx.shape[0] // b, idx.shape[-1]
    d = SUBLANES * LANES
    eps = MOE_EXPERTS_PER_STEP
    steps = groups // eps
    return pl.pallas_call(
        _gather_body,
        grid=(b, steps),
        in_specs=[
            pl.BlockSpec((eps, 1, cap), lambda bi, e: (bi * steps + e, 0, 0), memory_space=pltpu.SMEM),
            pl.BlockSpec((1, n, SUBLANES, LANES), lambda bi, e: (bi, 0, 0, 0)),
        ],
        out_specs=pl.BlockSpec((1, eps, cap, d), lambda bi, e: (bi, e, 0, 0)),
        out_shape=jax.ShapeDtypeStruct((b, groups, cap, d), BF16),
        compiler_params=_params("arbitrary", "arbitrary"),
        name="moe_gather",
    )(idx, hm)


FFN_SAMPLES_PER_STEP = 2


def _ffn_body(x_ref, wg_ref, wu_ref, wd_ref, o_ref):
    cap = x_ref.shape[2]
    x = jnp.concatenate([x_ref[s, 0] for s in range(x_ref.shape[0])], axis=0)
    hg = jnp.dot(x, wg_ref[0, 0].astype(BF16), preferred_element_type=F32)
    hu = jnp.dot(x, wu_ref[0, 0].astype(BF16), preferred_element_type=F32)
    y = jnp.dot((_silu(hg) * hu).astype(BF16), wd_ref[0, 0].astype(BF16), preferred_element_type=F32)
    for s in range(x_ref.shape[0]):
        o_ref[s, 0] = y[s * cap:(s + 1) * cap]


def _expert_ffn(xs, w_gate, w_up, w_down, layer):
    b, n_e, cap, d = xs.shape
    f = w_gate.shape[-1]
    sps = FFN_SAMPLES_PER_STEP if b % FFN_SAMPLES_PER_STEP == 0 else 1
    rows = pl.BlockSpec((sps, 1, cap, d), lambda e, bi: (bi, e, 0, 0))
    return pl.pallas_call(
        _ffn_body,
        grid=(n_e, b // sps),
        in_specs=[
            rows,
            pl.BlockSpec((1, 1, d, f), lambda e, bi: (layer, e, 0, 0)),
            pl.BlockSpec((1, 1, d, f), lambda e, bi: (layer, e, 0, 0)),
            pl.BlockSpec((1, 1, f, d), lambda e, bi: (layer, e, 0, 0)),
        ],
        out_specs=rows,
        out_shape=jax.ShapeDtypeStruct(xs.shape, F32),
        compiler_params=_params("arbitrary", "arbitrary"),
        name="moe_ffn",
    )(xs, w_gate, w_up, w_down)


SCATTER_UNROLL = 8


def _scatter_body(idx_ref, aff_ref, ys_ref, o_ref):
    cap = ys_ref.shape[2]

    @pl.when(pl.program_id(1) == 0)
    def _():
        o_ref[...] = jnp.zeros(o_ref.shape, F32)

    for k in range(ys_ref.shape[1]):
        def group(g, carry, k=k):
            base = pl.multiple_of(g * SCATTER_UNROLL, SCATTER_UNROLL)
            tiles = []
            for part in range(SCATTER_UNROLL // SUBLANES):
                y = ys_ref[0, k, pl.ds(base + part * SUBLANES, SUBLANES), :]
                tiles += _tile_transpose8([y[:, a * LANES:(a + 1) * LANES][None] for a in range(SUBLANES)])
            rows = [idx_ref[k, 0, base + u] for u in range(SCATTER_UNROLL)]
            sums = [o_ref[0, rows[u]] + tiles[u][0] * aff_ref[k, 0, rows[u]] for u in range(SCATTER_UNROLL)]
            for u in range(SCATTER_UNROLL):
                o_ref[0, rows[u]] = sums[u]
            return carry

        lax.fori_loop(0, cap // SCATTER_UNROLL, group, 0)


def _scatter(idx, aff, ys, n):
    b, groups, cap, d = ys.shape
    assert SCATTER_UNROLL % SUBLANES == 0 and cap % SCATTER_UNROLL == 0 and d == SUBLANES * LANES
    eps = MOE_EXPERTS_PER_STEP
    steps = groups // eps
    split = groups // N_EXPERTS
    assert split == 1 or eps == 1
    return pl.pallas_call(
        _scatter_body,
        grid=(b, steps),
        in_specs=[
            pl.BlockSpec((eps, 1, cap), lambda bi, e: (bi * steps + e, 0, 0), memory_space=pltpu.SMEM),
            pl.BlockSpec((eps, 1, n), lambda bi, e: ((bi * steps + e) // split, 0, 0), memory_space=pltpu.SMEM),
            pl.BlockSpec((1, eps, cap, d), lambda bi, e: (bi, e, 0, 0)),
        ],
        out_specs=pl.BlockSpec((1, n, SUBLANES, LANES), lambda bi, e: (bi, 0, 0, 0)),
        out_shape=jax.ShapeDtypeStruct((b, n, SUBLANES, LANES), F32),
        compiler_params=_params("arbitrary", "arbitrary"),
        name="moe_scatter",
    )(idx, aff, ys)


def _ec_moe(hm, logits, w_gate, w_up, w_down, layer, tables):
    b, n = hm.shape[:2]
    cap = max(1, EC_CAPACITY * n // N_EXPERTS)
    idx, aff = _router(logits, *tables, cap)
    split = MOE_SLOT_SPLIT
    idx = idx.reshape(b * N_EXPERTS * split, 1, cap // split)
    xs = _gather(idx, hm).reshape(b, N_EXPERTS, cap, -1)
    ys = _expert_ffn(xs, w_gate, w_up, w_down, layer)
    return _scatter(idx, aff.reshape(b * N_EXPERTS, 1, n), ys.reshape(b, N_EXPERTS * split, cap // split, -1), n)


def _final_body(h_ref, moe_ref, mod_ref, g_ref, o_ref):
    h = h_ref[0] + mod_ref[0, 5:6, :] * _tiles_to_rows(moe_ref[0])
    o_ref[0] = h * lax.rsqrt(jnp.mean(h * h, axis=-1, keepdims=True) + EPS) * g_ref[...]


def _final(h, moe, mods, g, tm=512):
    b, n, d = h.shape
    tok = pl.BlockSpec((1, tm, d), lambda bi, i: (bi, i, 0))
    return pl.pallas_call(
        _final_body,
        grid=(b, n // tm),
        in_specs=[tok, _tile_spec(tm, d), pl.BlockSpec((1, N_MOD, d), lambda bi, i: (bi, 0, 0)),
                  pl.BlockSpec((1, d), lambda bi, i: (0, 0))],
        out_specs=tok,
        out_shape=jax.ShapeDtypeStruct((b, n, d), F32),
        compiler_params=_params("arbitrary", "arbitrary"),
        name="final_norm",
    )(h, moe, mods, g)


def _rope_tables(n):
    rows = n // GRID_W
    row = np.repeat(np.arange(rows, dtype=np.float64), GRID_W)
    col = np.tile(np.arange(GRID_W, dtype=np.float64), rows)
    axis_dim = HEAD_DIM // 2
    inv = ROPE_THETA ** (-np.arange(0, axis_dim, 2, dtype=np.float64) / axis_dim)
    ang = np.concatenate([row[:, None] * inv, col[:, None] * inv], axis=-1)
    cos = np.cos(ang)
    sin = np.sin(ang)
    cos64 = np.concatenate([cos, cos], axis=-1)
    sin64 = np.concatenate([-sin, sin], axis=-1)
    reps = LANES // HEAD_DIM
    return jnp.asarray(np.tile(cos64, (1, reps)), F32), jnp.asarray(np.tile(sin64, (1, reps)), F32)


def _router_tables(n):
    r = np.arange(ROUTER_CHUNK)
    utri = r[:, None] <= r[None, :]
    rows = np.arange((n // ROUTER_CHUNK) * N_EXPERTS)
    same_e = (rows[:, None] % N_EXPERTS) == (rows[None, :] % N_EXPERTS)
    earlier = (rows[None, :] // N_EXPERTS) < (rows[:, None] // N_EXPERTS)
    chunk_of_token = (np.arange(n)[:, None] // ROUTER_CHUNK) == np.arange(LANES)[None, :]
    return tuple(jnp.asarray(t.astype(np.float32), BF16) for t in (utri, same_e & earlier, chunk_of_token))


def _head_ones():
    r = np.arange(MXU_DIM) // HEAD_DIM
    return jnp.asarray((r[:, None] == r[None, :]).astype(np.float32), BF16)


def kernel(x, c, ctx, c_ctx, ada_w, ada_b, norm1_g, norm2_g, ev_w_in, ev_q_g, ev_k_g, ev_conv_w, ev_conv_b,
           ev_ln_g, ev_ln_b, ev_w_out, sc_w_in, sc_conv_w, sc_w_out, moe_w_r, moe_w_gate, moe_w_up,
           moe_w_down, final_g):
    b, n, d = x.shape
    depth = ada_w.shape[0]
    assert depth == 2 and b < SUBLANES and n % ROUTER_CHUNK == 0

    cos2, sin2 = _rope_tables(n)
    tables = _router_tables(n)
    ones_bd = _head_ones()

    cvecs = jnp.zeros((SUBLANES, d), F32).at[:b].set(c).at[b].set(c_ctx)
    mods = _ada_mod(cvecs, ada_w, ada_b).reshape(depth, SUBLANES, N_MOD, d)
    w_r = jnp.pad(moe_w_r, ((0, 0), (0, 0), (0, LANES - N_EXPERTS))).astype(BF16)

    qg = jnp.tile(ev_q_g[0], N_Q_HEADS)[None, :]
    kg = jnp.tile(ev_k_g[0], N_KV_HEADS)[None, :]
    qt, k, vt, glu = _inproj0(x, mods[0], norm1_g[0:1], ev_w_in, qg, kg, cos2, sin2, ones_bd)
    k_ctx, vt_ctx = _ctxkv(ctx, mods[0], b, norm1_g[0:1], ev_w_in, kg, ones_bd)
    attn = _attention(qt, k_ctx, k, vt_ctx, vt)
    h, hm, logits = _outproj0(attn, glu, ev_conv_w[0], ev_conv_b[0:1], ev_ln_g[0:1], ev_ln_b[0:1],
                              ev_w_out[0].astype(BF16), x, mods[0], norm2_g[0:1], w_r[0])
    moe0 = _ec_moe(hm, logits, moe_w_gate, moe_w_up, moe_w_down, 0, tables)

    h, hm, logits = _mixer1(h, moe0, mods[0], mods[1], norm1_g[1:2], sc_w_in, sc_conv_w[0], sc_w_out,
                            norm2_g[1:2], w_r[1])
    moe1 = _ec_moe(hm, logits, moe_w_gate, moe_w_up, moe_w_down, 1, tables)

    return _final(h, moe1, mods[1], final_g[None, :])
```

```python
import functools

import jax
import jax.numpy as jnp
import numpy as np
from jax import lax
from jax.experimental import pallas as pl
from jax.experimental.pallas import tpu as pltpu

F32 = jnp.float32
BF16 = jnp.bfloat16

HEAD_DIM = 64
N_Q_HEADS = 8
N_KV_HEADS = 2
GRID_W = 64
ROPE_THETA = 10000.0
CONF_K = 31
SC_K = 3
N_EXPERTS = 16
EC_CAPACITY = 2
N_MOD = 6
EPS = 1e-6

ATTN_W = N_Q_HEADS * HEAD_DIM
KV_W = N_KV_HEADS * HEAD_DIM
Q_PER_KV = N_Q_HEADS // N_KV_HEADS
QK_SCALE = HEAD_DIM ** -0.5 * 1.4426950408889634

LANES = 128
SUBLANES = 8
MXU_DIM = 256
VMEM_LIMIT_BYTES = 60000 * 1024

CONV_HALO = 16
ROUTER_CHUNK = LANES
REFINE_STEPS = 24
INPROJ_PARTS = 2
MIXER_PARTS = 1


def _params(*sem):
    return pltpu.CompilerParams(dimension_semantics=sem, vmem_limit_bytes=VMEM_LIMIT_BYTES)


def _norm_mod(x, g, shift, scale):
    ms = jnp.mean(x * x, axis=-1, keepdims=True)
    y = x * lax.rsqrt(ms + EPS) * g
    return y * (1.0 + scale) + shift


def _head_sumsq(x, ones_blockdiag):
    return jnp.dot((x * x).astype(BF16), ones_blockdiag, preferred_element_type=F32)


def _swap_half(x):
    w = x.shape[-1]
    lane = lax.broadcasted_iota(jnp.int32, x.shape, 1)
    first = (lane % HEAD_DIM) < (HEAD_DIM // 2)
    return jnp.where(first, pltpu.roll(x, w - HEAD_DIM // 2, axis=1), pltpu.roll(x, HEAD_DIM // 2, axis=1))


def _silu(x):
    return x * jax.nn.sigmoid(x)


def _tile_transpose8(vs):
    sub = lax.broadcasted_iota(jnp.int32, vs[0].shape, 1)
    for d in (4, 2, 1):
        keep = (sub & d) == 0
        out = list(vs)
        for i in range(SUBLANES):
            if i & d == 0:
                a, b = vs[i], vs[i + d]
                out[i] = jnp.where(keep, a, pltpu.roll(b, d, axis=1))
                out[i + d] = jnp.where(keep, pltpu.roll(a, SUBLANES - d, axis=1), b)
        vs = out
    return vs


def _rows_to_tiles(x):
    r = x.shape[0]
    vs = [x[:, a * LANES:(a + 1) * LANES].reshape(r // SUBLANES, SUBLANES, LANES) for a in range(SUBLANES)]
    return jnp.stack(_tile_transpose8(vs), axis=1).reshape(r, SUBLANES, LANES)


def _tiles_to_rows(x3):
    r = x3.shape[0]
    x4 = x3.reshape(r // SUBLANES, SUBLANES, SUBLANES, LANES)
    vs = _tile_transpose8([x4[:, j] for j in range(SUBLANES)])
    return jnp.concatenate([v.reshape(r, LANES) for v in vs], axis=1)


def _store_vt_with_ones(vt_ref, cols, v):
    vt = v.T
    ones = jnp.ones((HEAD_DIM, v.shape[0]), F32)
    for j in range(N_KV_HEADS):
        vt_ref[0, j, :, cols] = jnp.concatenate([vt[j * HEAD_DIM:(j + 1) * HEAD_DIM, :], ones], axis=0).astype(BF16)


def _ada_body(c_ref, w_ref, b_ref, o_ref):
    s = _silu(c_ref[...]).astype(BF16)
    o_ref[0] = jnp.dot(s, w_ref[0].astype(BF16), preferred_element_type=F32) + b_ref[0]


def _ada_mod(cvecs, ada_w, ada_b):
    n_layers, d, n_out = ada_w.shape
    tn = n_out // 4
    return pl.pallas_call(
        _ada_body,
        grid=(n_layers, n_out // tn),
        in_specs=[
            pl.BlockSpec((SUBLANES, d), lambda l, j: (0, 0)),
            pl.BlockSpec((1, d, tn), lambda l, j: (l, 0, j)),
            pl.BlockSpec((1, 1, tn), lambda l, j: (l, 0, j)),
        ],
        out_specs=pl.BlockSpec((1, SUBLANES, tn), lambda l, j: (l, 0, j)),
        out_shape=jax.ShapeDtypeStruct((n_layers, SUBLANES, n_out), F32),
        compiler_params=_params("arbitrary", "arbitrary"),
        name="ada_mod",
    )(cvecs, ada_w, ada_b.reshape(n_layers, 1, n_out))


def _inproj0_body(h_ref, mod_ref, g1_ref, w_ref, qg_ref, kg_ref, cos_ref, sin_ref, ones_ref,
                  qt_ref, k_ref, vt_ref, glu_ref):
    ones = ones_ref[...]
    tm = h_ref.shape[1]
    part = tm // INPROJ_PARTS
    for p in range(INPROJ_PARTS):
        rows = slice(p * part, (p + 1) * part)
        hn = _norm_mod(h_ref[0, rows, :], g1_ref[...], mod_ref[0, 0:1, :], mod_ref[0, 1:2, :])
        proj = jnp.dot(hn.astype(BF16), w_ref[0].astype(BF16), preferred_element_type=F32)
        cos2 = cos_ref[rows, :]
        sin2 = sin_ref[rows, :]

        q = proj[:, :ATTN_W]
        ssq = jnp.concatenate([_head_sumsq(q[:, :MXU_DIM], ones), _head_sumsq(q[:, MXU_DIM:], ones)], axis=1)
        qn = q * lax.rsqrt(ssq * (1.0 / HEAD_DIM) + EPS) * qg_ref[...]
        cos = jnp.concatenate([cos2] * (ATTN_W // LANES), axis=1)
        sin = jnp.concatenate([sin2] * (ATTN_W // LANES), axis=1)
        qr = (qn * cos + _swap_half(qn) * sin) * QK_SCALE
        qt_ref[0, :, rows] = qr.T.astype(BF16)

        k = proj[:, ATTN_W:ATTN_W + KV_W]
        kn = k * lax.rsqrt(_head_sumsq(k, ones[:KV_W, :KV_W]) * (1.0 / HEAD_DIM) + EPS) * kg_ref[...]
        k_ref[0, rows, :] = (kn * cos2 + _swap_half(kn) * sin2).astype(BF16)
        _store_vt_with_ones(vt_ref, rows, proj[:, ATTN_W + KV_W:ATTN_W + 2 * KV_W])

        c0 = ATTN_W + 2 * KV_W
        cc = (proj.shape[1] - c0) // 2
        glu_ref[0, rows, :] = proj[:, c0:c0 + cc] * jax.nn.sigmoid(proj[:, c0 + cc:])


def _inproj0(h, mods, g1, w_in, qg, kg, cos2, sin2, ones_bd, tm=512):
    b, n, d = h.shape
    n_in = w_in.shape[-1]
    conv_ch = (n_in - ATTN_W - 2 * KV_W) // 2
    return pl.pallas_call(
        _inproj0_body,
        grid=(b, n // tm),
        in_specs=[
            pl.BlockSpec((1, tm, d), lambda bi, i: (bi, i, 0)),
            pl.BlockSpec((1, N_MOD, d), lambda bi, i: (bi, 0, 0)),
            pl.BlockSpec((1, d), lambda bi, i: (0, 0)),
            pl.BlockSpec((1, d, n_in), lambda bi, i: (0, 0, 0), pipeline_mode=pl.Buffered(1)),
            pl.BlockSpec((1, ATTN_W), lambda bi, i: (0, 0)),
            pl.BlockSpec((1, KV_W), lambda bi, i: (0, 0)),
            pl.BlockSpec((tm, LANES), lambda bi, i: (i, 0)),
            pl.BlockSpec((tm, LANES), lambda bi, i: (i, 0)),
            pl.BlockSpec((MXU_DIM, MXU_DIM), lambda bi, i: (0, 0)),
        ],
        out_specs=[
            pl.BlockSpec((1, ATTN_W, tm), lambda bi, i: (bi, 0, i)),
            pl.BlockSpec((1, tm, KV_W), lambda bi, i: (bi, i, 0)),
            pl.BlockSpec((1, N_KV_HEADS, 2 * HEAD_DIM, tm), lambda bi, i: (bi, 0, 0, i)),
            pl.BlockSpec((1, tm, conv_ch), lambda bi, i: (bi, i, 0)),
        ],
        out_shape=[
            jax.ShapeDtypeStruct((b, ATTN_W, n), BF16),
            jax.ShapeDtypeStruct((b, n, KV_W), BF16),
            jax.ShapeDtypeStruct((b, N_KV_HEADS, 2 * HEAD_DIM, n), BF16),
            jax.ShapeDtypeStruct((b, n, conv_ch), F32),
        ],
        compiler_params=_params("arbitrary", "arbitrary"),
        name="inproj0",
    )(h, mods, g1, w_in, qg, kg, cos2, sin2, ones_bd)


def _ctxkv_body(x_ref, mod_ref, g1_ref, w_ref, kg_ref, ones_ref, k_ref, vt_ref):
    hn = _norm_mod(x_ref[0], g1_ref[...], mod_ref[0, 0:1, :], mod_ref[0, 1:2, :])
    proj = jnp.dot(hn.astype(BF16), w_ref[0].astype(BF16), preferred_element_type=F32)
    k = proj[:, :KV_W]
    kn = k * lax.rsqrt(_head_sumsq(k, ones_ref[...][:KV_W, :KV_W]) * (1.0 / HEAD_DIM) + EPS) * kg_ref[...]
    k_ref[0] = kn.astype(BF16)
    _store_vt_with_ones(vt_ref, slice(None), proj[:, KV_W:])


def _ctxkv(ctx, mods, ctx_row, g1, w_in, kg, ones_bd):
    b, t, d = ctx.shape
    assert ATTN_W % (2 * KV_W) == 0
    kv_block = ATTN_W // (2 * KV_W)
    return pl.pallas_call(
        _ctxkv_body,
        grid=(b,),
        in_specs=[
            pl.BlockSpec((1, t, d), lambda bi: (bi, 0, 0)),
            pl.BlockSpec((1, N_MOD, d), lambda bi: (ctx_row, 0, 0)),
            pl.BlockSpec((1, d), lambda bi: (0, 0)),
            pl.BlockSpec((1, d, 2 * KV_W), lambda bi: (0, 0, kv_block)),
            pl.BlockSpec((1, KV_W), lambda bi: (0, 0)),
            pl.BlockSpec((MXU_DIM, MXU_DIM), lambda bi: (0, 0)),
        ],
        out_specs=[
            pl.BlockSpec((1, t, KV_W), lambda bi: (bi, 0, 0)),
            pl.BlockSpec((1, N_KV_HEADS, 2 * HEAD_DIM, t), lambda bi: (bi, 0, 0, 0)),
        ],
        out_shape=[
            jax.ShapeDtypeStruct((b, t, KV_W), BF16),
            jax.ShapeDtypeStruct((b, N_KV_HEADS, 2 * HEAD_DIM, t), BF16),
        ],
        compiler_params=_params("arbitrary"),
        name="ctx_kv",
    )(ctx, mods, g1, w_in, kg, ones_bd)


ATTN_KEY_CHUNK = 512


def _attn_body(qt_ref, kc_ref, k_ref, vtc_ref, vt_ref, o_ref, s0, s1, p0, p1, ot):
    s_bufs = (s0, s1)
    p_bufs = (p0, p1)
    t_ctx = kc_ref.shape[1]
    n = k_ref.shape[1]
    tq = qt_ref.shape[2]

    def column_max(run, sc):
        part = jnp.max(sc.reshape(sc.shape[0] // SUBLANES, SUBLANES, tq), axis=0)
        return part if run is None else jnp.maximum(run, part)

    assert t_ctx % SUBLANES == 0 and n % ATTN_KEY_CHUNK == 0
    chunks = [(0, t_ctx, None)] + [(t_ctx + lo, ATTN_KEY_CHUNK, lo) for lo in range(0, n, ATTN_KEY_CHUNK)]

    def padded_query(h):
        qt = qt_ref[0, h * HEAD_DIM:(h + 1) * HEAD_DIM, :]
        zeros = jnp.zeros_like(qt)
        return jnp.concatenate([qt, zeros] if h // Q_PER_KV == 0 else [zeros, qt], axis=0)

    def score_chunk(h, qpad, run, chunk):
        row, size, lo = chunk
        keys = kc_ref[0] if lo is None else k_ref[0, lo:lo + size, :]
        sc = jnp.dot(keys, qpad, preferred_element_type=F32)
        s_bufs[h % 2][row:row + size, :] = sc
        return column_max(run, sc)

    def exp_chunk(h, m, chunk):
        row, size, _ = chunk
        p_bufs[h % 2][row:row + size, :] = jnp.exp2(s_bufs[h % 2][row:row + size, :] - m).astype(BF16)

    def value_chunk(h, acc, chunk):
        row, size, lo = chunk
        j = h // Q_PER_KV
        vt = vtc_ref[0, j] if lo is None else vt_ref[0, j, :, lo:lo + size]
        part = jnp.dot(vt, p_bufs[h % 2][row:row + size, :], preferred_element_type=F32)
        return part if acc is None else acc + part

    m = None
    for h in range(-1, N_Q_HEADS + 1):
        qpad = padded_query(h + 1) if h + 1 < N_Q_HEADS else None
        run = acc = None
        for chunk in chunks:
            if qpad is not None:
                run = score_chunk(h + 1, qpad, run, chunk)
            if 0 <= h - 1:
                acc = value_chunk(h - 1, acc, chunk)
            if 0 <= h < N_Q_HEADS:
                exp_chunk(h, m, chunk)
        if acc is not None:
            ot[(h - 1) * HEAD_DIM:h * HEAD_DIM, :] = acc[:HEAD_DIM, :] / acc[HEAD_DIM:HEAD_DIM + 1, :]
        m = jnp.max(run, axis=0, keepdims=True) if run is not None else None
    o_ref[0] = ot[...].T.astype(BF16)


def _attention(qt, k_ctx, k, vt_ctx, vt, tq=512):
    b, width, n = qt.shape
    t_ctx = k_ctx.shape[1]
    t_all = t_ctx + n
    return pl.pallas_call(
        _attn_body,
        grid=(b, n // tq),
        scratch_shapes=[pltpu.VMEM((t_all, tq), F32), pltpu.VMEM((t_all, tq), F32),
                        pltpu.VMEM((t_all, tq), BF16), pltpu.VMEM((t_all, tq), BF16),
                        pltpu.VMEM((width, tq), F32)],
        in_specs=[
            pl.BlockSpec((1, width, tq), lambda bi, i: (bi, 0, i)),
            pl.BlockSpec((1, t_ctx, KV_W), lambda bi, i: (bi, 0, 0)),
            pl.BlockSpec((1, n, KV_W), lambda bi, i: (bi, 0, 0)),
            pl.BlockSpec((1, N_KV_HEADS, 2 * HEAD_DIM, t_ctx), lambda bi, i: (bi, 0, 0, 0)),
            pl.BlockSpec((1, N_KV_HEADS, 2 * HEAD_DIM, n), lambda bi, i: (bi, 0, 0, 0)),
        ],
        out_specs=pl.BlockSpec((1, tq, width), lambda bi, i: (bi, i, 0)),
        out_shape=jax.ShapeDtypeStruct((b, n, width), BF16),
        compiler_params=_params("arbitrary", "arbitrary"),
        name="attention",
    )(qt, k_ctx, k, vt_ctx, vt)


CONV_ROWS = 64


def _conformer_rows(window, w_ref, cb_ref, lg_ref, lb_ref):
    tm = window.shape[0] - 2 * CONV_HALO
    ch = window.shape[1]
    win = CONV_ROWS + 2 * CONV_HALO
    off = CONV_HALO - CONF_K // 2
    cols = []
    for t in range(ch // LANES):
        lanes = slice(t * LANES, (t + 1) * LANES)
        blocks = []
        for r in range(tm // CONV_ROWS):
            x = window[r * CONV_ROWS:r * CONV_ROWS + win, lanes]
            acc = jnp.zeros((CONV_ROWS, LANES), F32)
            for sub in range(SUBLANES):
                xs = x if sub == 0 else pltpu.roll(x, win - sub, axis=0)
                for a in range(2 * CONV_HALO // SUBLANES):
                    k = a * SUBLANES + sub - off
                    if 0 <= k < CONF_K:
                        acc = acc + xs[a * SUBLANES:a * SUBLANES + CONV_ROWS, :] * w_ref[k:k + 1, lanes]
            blocks.append(acc)
        cols.append(jnp.concatenate(blocks, axis=0))
    y = jnp.concatenate(cols, axis=1) + cb_ref[...]
    mu = jnp.mean(y, axis=-1, keepdims=True)
    yc = y - mu
    var = jnp.mean(yc * yc, axis=-1, keepdims=True)
    return _silu(yc * lax.rsqrt(var + EPS) * lg_ref[...] + lb_ref[...])


def _residual_router(mix, h, mod_ref, g2_ref, wr_ref, h_out, hm_out, lg_out, rows=slice(None)):
    h1 = h + mod_ref[0, 2:3, :] * mix
    h_out[0, rows] = h1
    hm = _norm_mod(h1, g2_ref[...], mod_ref[0, 3:4, :], mod_ref[0, 4:5, :])
    hm_out[0, rows] = _rows_to_tiles(hm)
    lg_out[0, rows] = jnp.dot(hm.astype(BF16), wr_ref[...], preferred_element_type=F32)


def _outproj0_body(attn_ref, glu_ref, glup_ref, glun_ref, cw_ref, cb_ref, lg_ref, lb_ref, w_ref,
                   h_ref, mod_ref, g2_ref, wr_ref, h_out, hm_out, lg_out):
    i = pl.program_id(1)
    last = pl.num_programs(1) - 1
    before = jnp.where(i > 0, glup_ref[0], 0.0)
    after = jnp.where(i < last, glun_ref[0], 0.0)
    conf = _conformer_rows(jnp.concatenate([before, glu_ref[0], after], axis=0), cw_ref, cb_ref, lg_ref, lb_ref)
    a = jnp.concatenate([attn_ref[0], conf.astype(BF16)], axis=1)
    mix = jnp.dot(a, w_ref[...], preferred_element_type=F32)
    _residual_router(mix, h_ref[0], mod_ref, g2_ref, wr_ref, h_out, hm_out, lg_out)


def _tile_spec(tm, d):
    assert d == SUBLANES * LANES
    return pl.BlockSpec((1, tm, SUBLANES, LANES), lambda bi, i: (bi, i, 0, 0))


def _token_out_specs(b, n, d, tm):
    specs = [
        pl.BlockSpec((1, tm, d), lambda bi, i: (bi, i, 0)),
        _tile_spec(tm, d),
        pl.BlockSpec((1, tm, LANES), lambda bi, i: (bi, i, 0)),
    ]
    shapes = [
        jax.ShapeDtypeStruct((b, n, d), F32),
        jax.ShapeDtypeStruct((b, n, SUBLANES, d // SUBLANES), F32),
        jax.ShapeDtypeStruct((b, n, LANES), F32),
    ]
    return specs, shapes


def _outproj0(attn, glu, conv_w, conv_b, ln_g, ln_b, w_out, h, mods, g2, w_r, tm=512):
    b, n, d = h.shape
    ch = glu.shape[-1]
    per = tm // CONV_HALO
    n_halo = n // CONV_HALO
    vec = pl.BlockSpec((1, ch), lambda bi, i: (0, 0))
    out_specs, out_shape = _token_out_specs(b, n, d, tm)
    return pl.pallas_call(
        _outproj0_body,
        grid=(b, n // tm),
        in_specs=[
            pl.BlockSpec((1, tm, ATTN_W), lambda bi, i: (bi, i, 0)),
            pl.BlockSpec((1, tm, ch), lambda bi, i: (bi, i, 0)),
            pl.BlockSpec((1, CONV_HALO, ch), lambda bi, i: (bi, jnp.maximum(i * per - 1, 0), 0)),
            pl.BlockSpec((1, CONV_HALO, ch), lambda bi, i: (bi, jnp.minimum((i + 1) * per, n_halo - 1), 0)),
            pl.BlockSpec((CONF_K, ch), lambda bi, i: (0, 0)),
            vec, vec, vec,
            pl.BlockSpec((d, d), lambda bi, i: (0, 0)),
            pl.BlockSpec((1, tm, d), lambda bi, i: (bi, i, 0)),
            pl.BlockSpec((1, N_MOD, d), lambda bi, i: (bi, 0, 0)),
            pl.BlockSpec((1, d), lambda bi, i: (0, 0)),
            pl.BlockSpec((d, LANES), lambda bi, i: (0, 0)),
        ],
        out_specs=out_specs,
        out_shape=out_shape,
        compiler_params=_params("arbitrary", "arbitrary"),
        name="outproj0",
    )(attn, glu, glu, glu, conv_w, conv_b, ln_g, ln_b, w_out, h, mods, g2, w_r)


def _mixer1_body(h_ref, hp_ref, hn_ref, moe_ref, moep_ref, moen_ref, modp_ref, mod_ref, g1_ref, w_ref,
                 cw_ref, wo_ref, g2_ref, wr_ref, h_out, hm_out, lg_out):
    i = pl.program_id(1)
    last = pl.num_programs(1) - 1
    tm, d = h_ref.shape[1:]
    gate_prev = modp_ref[0, 5:6, :]

    def stream(h_blk, moe_blk, rows=slice(None)):
        return h_blk[0, rows] + gate_prev * _tiles_to_rows(moe_blk[0, rows])

    part = tm // MIXER_PARTS
    win = part + 2 * SUBLANES
    for p in range(MIXER_PARTS):
        rows = slice(p * part, (p + 1) * part)
        h = stream(h_ref, moe_ref, rows)
        above = (stream(hp_ref, moep_ref) if p == 0
                 else stream(h_ref, moe_ref, slice(p * part - SUBLANES, p * part)))
        below = (stream(hn_ref, moen_ref) if p == MIXER_PARTS - 1
                 else stream(h_ref, moe_ref, slice((p + 1) * part, (p + 1) * part + SUBLANES)))
        hn = _norm_mod(jnp.concatenate([above, h, below], axis=0), g1_ref[...], mod_ref[0, 0:1, :], mod_ref[0, 1:2, :])
        proj = jnp.dot(hn.astype(BF16), w_ref[0].astype(BF16), preferred_element_type=F32)
        z = proj[:, d:2 * d] * proj[:, 2 * d:]
        r = lax.broadcasted_iota(jnp.int32, z.shape, 0)
        if p == 0:
            z = jnp.where((r < SUBLANES) & (i == 0), 0.0, z)
        if p == MIXER_PARTS - 1:
            z = jnp.where((r >= part + SUBLANES) & (i == last), 0.0, z)
        y = (pltpu.roll(z, 1, axis=0) * cw_ref[0:1, :] + z * cw_ref[1:2, :]
             + pltpu.roll(z, win - 1, axis=0) * cw_ref[2:3, :])[SUBLANES:SUBLANES + part]
        gated = proj[SUBLANES:SUBLANES + part, :d] * y
        mix = jnp.dot(gated.astype(BF16), wo_ref[0].astype(BF16), preferred_element_type=F32)
        _residual_router(mix, h, mod_ref, g2_ref, wr_ref, h_out, hm_out, lg_out, rows)


def _mixer1(h, moe, mods_prev, mods, g1, w_in, conv_w, w_out, g2, w_r, tm=512):
    b, n, d = h.shape
    per = tm // SUBLANES
    n8 = n // SUBLANES
    before = lambda bi, i: (bi, jnp.maximum(i * per - 1, 0), 0)
    after = lambda bi, i: (bi, jnp.minimum((i + 1) * per, n8 - 1), 0)
    tok = pl.BlockSpec((1, tm, d), lambda bi, i: (bi, i, 0))
    modspec = pl.BlockSpec((1, N_MOD, d), lambda bi, i: (bi, 0, 0))
    vec = pl.BlockSpec((1, d), lambda bi, i: (0, 0))
    out_specs, out_shape = _token_out_specs(b, n, d, tm)
    return pl.pallas_call(
        _mixer1_body,
        grid=(b, n // tm),
        in_specs=[
            tok,
            pl.BlockSpec((1, SUBLANES, d), before),
            pl.BlockSpec((1, SUBLANES, d), after),
            _tile_spec(tm, d),
            pl.BlockSpec((1, SUBLANES, SUBLANES, LANES), lambda bi, i: before(bi, i) + (0,)),
            pl.BlockSpec((1, SUBLANES, SUBLANES, LANES), lambda bi, i: after(bi, i) + (0,)),
            modspec, modspec, vec,
            pl.BlockSpec((1, d, 3 * d), lambda bi, i: (0, 0, 0), pipeline_mode=pl.Buffered(1)),
            pl.BlockSpec((SC_K, d), lambda bi, i: (0, 0)),
            pl.BlockSpec((1, d, d), lambda bi, i: (0, 0, 0), pipeline_mode=pl.Buffered(1)),
            vec,
            pl.BlockSpec((d, LANES), lambda bi, i: (0, 0)),
        ],
        out_specs=out_specs,
        out_shape=out_shape,
        compiler_params=_params("arbitrary", "arbitrary"),
        name="mixer1",
    )(h, h, h, moe, moe, moe, mods_prev, mods, g1, w_in, conv_w, w_out, g2, w_r)


def _stack_chunks(x):
    n = x.shape[1]
    return jnp.concatenate([x[:, c * ROUTER_CHUNK:(c + 1) * ROUTER_CHUNK] for c in range(n // ROUTER_CHUNK)], axis=0)


def _exclusive_rank(flags, utri, chunk_lt):
    incl = jnp.dot(flags.astype(BF16), utri, preferred_element_type=F32)
    tot = jnp.broadcast_to(incl[:, ROUTER_CHUNK - 1:ROUTER_CHUNK], incl.shape)
    base = jnp.dot(chunk_lt, tot.astype(BF16), preferred_element_type=F32)
    return incl - flags + base


def _router_body(lg_ref, utri_ref, lt_ref, blk_ref, idx_ref, aff_ref, loc_scr, tot_scr, end_scr, *, cap):
    n = lg_ref.shape[1]
    n_chunks = n // ROUTER_CHUNK
    lg = lg_ref[0]
    lane = lax.broadcasted_iota(jnp.int32, lg.shape, 1)
    valid = lane < N_EXPERTS
    x = jnp.where(valid, lg, -jnp.inf)
    ex = jnp.where(valid, jnp.exp(x - jnp.max(x, axis=-1, keepdims=True)), 0.0)
    aff = ex / jnp.sum(ex, axis=-1, keepdims=True)

    aff_t = aff.T[:N_EXPERTS, :]
    aff_ref[0] = aff_t

    def count_ge(t):
        return jnp.sum(jnp.where(aff_t >= t, 1.0, 0.0), axis=-1, keepdims=True)

    def bit_step(i, bits):
        cand = bits | jnp.left_shift(jnp.int32(1), 30 - i)
        return jnp.where(count_ge(lax.bitcast_convert_type(cand, F32)) >= cap, cand, bits)

    bits = lax.fori_loop(0, 31, bit_step, jnp.zeros((N_EXPERTS, 1), jnp.int32))

    def refine(i, lo_hi):
        lo, hi = lo_hi
        mid = (lo + hi) * 0.5
        ok = count_ge(mid) >= cap
        return jnp.where(ok, mid, lo), jnp.where(ok, hi, mid)

    thr, _ = lax.fori_loop(0, REFINE_STEPS, refine,
                           (lax.bitcast_convert_type(bits, F32), lax.bitcast_convert_type(bits + 1, F32)))
    gt = jnp.where(aff_t > thr, 1.0, 0.0)
    eq = jnp.where(aff_t == thr, 1.0, 0.0)
    need = cap - jnp.sum(gt, axis=-1, keepdims=True)
    utri = utri_ref[...]
    chunk_lt = lt_ref[...]
    gt_s = _stack_chunks(gt)
    eq_s = _stack_chunks(eq)
    need_s = jnp.concatenate([need] * n_chunks, axis=0)
    sel = jnp.where((gt_s > 0) | ((eq_s > 0) & (_exclusive_rank(eq_s, utri, chunk_lt) < need_s)), 1.0, 0.0)

    loc_scr[...] = jnp.dot(sel.astype(BF16), utri, preferred_element_type=F32).astype(BF16)
    sel_t = jnp.concatenate([sel[c * N_EXPERTS:(c + 1) * N_EXPERTS, :] for c in range(n_chunks)], axis=1)
    tot = jnp.dot(sel_t.astype(BF16), blk_ref[...], preferred_element_type=F32)
    tot_scr[...] = tot
    end_scr[...] = jnp.dot(tot.astype(BF16), utri, preferred_element_type=F32)

    slot = lax.broadcasted_iota(jnp.int32, (cap, 1), 0).astype(F32)
    row_id = lax.broadcasted_iota(jnp.int32, (cap, n_chunks * N_EXPERTS), 1)

    def one_expert(e, carry):
        ends = end_scr[pl.ds(e, 1), :]
        before = ends <= slot
        chunk = jnp.sum(jnp.where(before, 1.0, 0.0), axis=-1, keepdims=True)
        base = jnp.sum(jnp.where(before, tot_scr[pl.ds(e, 1), :], 0.0), axis=-1, keepdims=True)
        pick = jnp.where(row_id == chunk.astype(jnp.int32) * N_EXPERTS + e, 1.0, 0.0).astype(BF16)
        counts = jnp.dot(pick, loc_scr[...], preferred_element_type=F32)
        inside = jnp.sum(jnp.where(counts <= slot - base, 1.0, 0.0), axis=-1, keepdims=True)
        idx_ref[0, e] = (chunk * ROUTER_CHUNK + inside).astype(jnp.int32)
        return carry

    lax.fori_loop(0, N_EXPERTS, one_expert, 0)


def _router(logits, utri, chunk_lt, chunk_of_token, cap):
    b, n, _ = logits.shape
    rows = (n // ROUTER_CHUNK) * N_EXPERTS
    assert n // ROUTER_CHUNK <= LANES
    return pl.pallas_call(
        functools.partial(_router_body, cap=cap),
        grid=(b,),
        in_specs=[
            pl.BlockSpec((1, n, LANES), lambda bi: (bi, 0, 0)),
            pl.BlockSpec((ROUTER_CHUNK, ROUTER_CHUNK), lambda bi: (0, 0)),
            pl.BlockSpec((rows, rows), lambda bi: (0, 0)),
            pl.BlockSpec((n, LANES), lambda bi: (0, 0)),
        ],
        out_specs=[
            pl.BlockSpec((1, N_EXPERTS, cap, 1), lambda bi: (bi, 0, 0, 0)),
            pl.BlockSpec((1, N_EXPERTS, n), lambda bi: (bi, 0, 0)),
        ],
        out_shape=[
            jax.ShapeDtypeStruct((b, N_EXPERTS, cap, 1), jnp.int32),
            jax.ShapeDtypeStruct((b, N_EXPERTS, n), F32),
        ],
        scratch_shapes=[pltpu.VMEM((rows, ROUTER_CHUNK), BF16), pltpu.VMEM((N_EXPERTS, LANES), F32),
                        pltpu.VMEM((N_EXPERTS, LANES), F32)],
        compiler_params=_params("arbitrary"),
        name="router",
    )(logits, utri, chunk_lt, chunk_of_token)


GATHER_UNROLL = 16
MOE_EXPERTS_PER_STEP = 1
MOE_SLOT_SPLIT = 1


def _gather_body(idx_ref, hm_ref, o_ref):
    cap = o_ref.shape[2]

    for k in range(o_ref.shape[1]):
        def group(g, carry, k=k):
            base = pl.multiple_of(g * GATHER_UNROLL, GATHER_UNROLL)
            halves = []
            for half in range(GATHER_UNROLL // SUBLANES):
                tiles = [hm_ref[0, idx_ref[k, 0, base + half * SUBLANES + u]][None] for u in range(SUBLANES)]
                halves.append(_tile_transpose8(tiles))
            rows = jnp.concatenate([jnp.concatenate([h[a][0] for h in halves], axis=0) for a in range(SUBLANES)],
                                   axis=1)
            o_ref[0, k, pl.ds(base, GATHER_UNROLL), :] = rows.astype(BF16)
            return carry

        lax.fori_loop(0, cap // GATHER_UNROLL, group, 0)


def _gather(idx, hm):
    b, n = hm.shape[:2]
    groups, cap = idx.shape[0] // b, idx.shape[-1]
    d = SUBLANES * LANES
    eps = MOE_EXPERTS_PER_STEP
    steps = groups // eps
    return pl.pallas_call(
        _gather_body,
        grid=(b, steps),
        in_specs=[
            pl.BlockSpec((eps, 1, cap), lambda bi, e: (bi * steps + e, 0, 0), memory_space=pltpu.SMEM),
            pl.BlockSpec((1, n, SUBLANES, LANES), lambda bi, e: (bi, 0, 0, 0)),
        ],
        out_specs=pl.BlockSpec((1, eps, cap, d), lambda bi, e: (bi, e, 0, 0)),
        out_shape=jax.ShapeDtypeStruct((b, groups, cap, d), BF16),
        compiler_params=_params("arbitrary", "arbitrary"),
        name="moe_gather",
    )(idx, hm)


FFN_SAMPLES_PER_STEP = 2


def _ffn_body(x_ref, wg_ref, wu_ref, wd_ref, o_ref):
    cap = x_ref.shape[2]
    x = jnp.concatenate([x_ref[s, 0] for s in range(x_ref.shape[0])], axis=0)
    hg = jnp.dot(x, wg_ref[0, 0].astype(BF16), preferred_element_type=F32)
    hu = jnp.dot(x, wu_ref[0, 0].astype(BF16), preferred_element_type=F32)
    y = jnp.dot((_silu(hg) * hu).astype(BF16), wd_ref[0, 0].astype(BF16), preferred_element_type=F32)
    for s in range(x_ref.shape[0]):
        o_ref[s, 0] = y[s * cap:(s + 1) * cap]


def _expert_ffn(xs, w_gate, w_up, w_down, layer):
    b, n_e, cap, d = xs.shape
    f = w_gate.shape[-1]
    sps = FFN_SAMPLES_PER_STEP if b % FFN_SAMPLES_PER_STEP == 0 else 1
    rows = pl.BlockSpec((sps, 1, cap, d), lambda e, bi: (bi, e, 0, 0))
    return pl.pallas_call(
        _ffn_body,
        grid=(n_e, b // sps),
        in_specs=[
            rows,
            pl.BlockSpec((1, 1, d, f), lambda e, bi: (layer, e, 0, 0)),
            pl.BlockSpec((1, 1, d, f), lambda e, bi: (layer, e, 0, 0)),
            pl.BlockSpec((1, 1, f, d), lambda e, bi: (layer, e, 0, 0)),
        ],
        out_specs=rows,
        out_shape=jax.ShapeDtypeStruct(xs.shape, F32),
        compiler_params=_params("arbitrary", "arbitrary"),
        name="moe_ffn",
    )(xs, w_gate, w_up, w_down)


SCATTER_UNROLL = 8


def _scatter_body(idx_ref, aff_ref, ys_ref, o_ref):
    cap = ys_ref.shape[2]

    @pl.when(pl.program_id(1) == 0)
    def _():
        o_ref[...] = jnp.zeros(o_ref.shape, F32)

    for k in range(ys_ref.shape[1]):
        def group(g, carry, k=k):
            base = pl.multiple_of(g * SCATTER_UNROLL, SCATTER_UNROLL)
            tiles = []
            for part in range(SCATTER_UNROLL // SUBLANES):
                y = ys_ref[0, k, pl.ds(base + part * SUBLANES, SUBLANES), :]
                tiles += _tile_transpose8([y[:, a * LANES:(a + 1) * LANES][None] for a in range(SUBLANES)])
            rows = [idx_ref[k, 0, base + u] for u in range(SCATTER_UNROLL)]
            sums = [o_ref[0, rows[u]] + tiles[u][0] * aff_ref[k, 0, rows[u]] for u in range(SCATTER_UNROLL)]
            for u in range(SCATTER_UNROLL):
                o_ref[0, rows[u]] = sums[u]
            return carry

        lax.fori_loop(0, cap // SCATTER_UNROLL, group, 0)


def _scatter(idx, aff, ys, n):
    b, groups, cap, d = ys.shape
    assert SCATTER_UNROLL % SUBLANES == 0 and cap % SCATTER_UNROLL == 0 and d == SUBLANES * LANES
    eps = MOE_EXPERTS_PER_STEP
    steps = groups // eps
    split = groups // N_EXPERTS
    assert split == 1 or eps == 1
    return pl.pallas_call(
        _scatter_body,
        grid=(b, steps),
        in_specs=[
            pl.BlockSpec((eps, 1, cap), lambda bi, e: (bi * steps + e, 0, 0), memory_space=pltpu.SMEM),
            pl.BlockSpec((eps, 1, n), lambda bi, e: ((bi * steps + e) // split, 0, 0), memory_space=pltpu.SMEM),
            pl.BlockSpec((1, eps, cap, d), lambda bi, e: (bi, e, 0, 0)),
        ],
        out_specs=pl.BlockSpec((1, n, SUBLANES, LANES), lambda bi, e: (bi, 0, 0, 0)),
        out_shape=jax.ShapeDtypeStruct((b, n, SUBLANES, LANES), F32),
        compiler_params=_params("arbitrary", "arbitrary"),
        name="moe_scatter",
    )(idx, aff, ys)


def _ec_moe(hm, logits, w_gate, w_up, w_down, layer, tables):
    b, n = hm.shape[:2]
    cap = max(1, EC_CAPACITY * n // N_EXPERTS)
    idx, aff = _router(logits, *tables, cap)
    split = MOE_SLOT_SPLIT
    idx = idx.reshape(b * N_EXPERTS * split, 1, cap // split)
    xs = _gather(idx, hm).reshape(b, N_EXPERTS, cap, -1)
    ys = _expert_ffn(xs, w_gate, w_up, w_down, layer)
    return _scatter(idx, aff.reshape(b * N_EXPERTS, 1, n), ys.reshape(b, N_EXPERTS * split, cap // split, -1), n)


def _final_body(h_ref, moe_ref, mod_ref, g_ref, o_ref):
    h = h_ref[0] + mod_ref[0, 5:6, :] * _tiles_to_rows(moe_ref[0])
    o_ref[0] = h * lax.rsqrt(jnp.mean(h * h, axis=-1, keepdims=True) + EPS) * g_ref[...]


def _final(h, moe, mods, g, tm=512):
    b, n, d = h.shape
    tok = pl.BlockSpec((1, tm, d), lambda bi, i: (bi, i, 0))
    return pl.pallas_call(
        _final_body,
        grid=(b, n // tm),
        in_specs=[tok, _tile_spec(tm, d), pl.BlockSpec((1, N_MOD, d), lambda bi, i: (bi, 0, 0)),
                  pl.BlockSpec((1, d), lambda bi, i: (0, 0))],
        out_specs=tok,
        out_shape=jax.ShapeDtypeStruct((b, n, d), F32),
        compiler_params=_params("arbitrary", "arbitrary"),
        name="final_norm",
    )(h, moe, mods, g)


def _rope_tables(n):
    rows = n // GRID_W
    row = np.repeat(np.arange(rows, dtype=np.float64), GRID_W)
    col = np.tile(np.arange(GRID_W, dtype=np.float64), rows)
    axis_dim = HEAD_DIM // 2
    inv = ROPE_THETA ** (-np.arange(0, axis_dim, 2, dtype=np.float64) / axis_dim)
    ang = np.concatenate([row[:, None] * inv, col[:, None] * inv], axis=-1)
    cos = np.cos(ang)
    sin = np.sin(ang)
    cos64 = np.concatenate([cos, cos], axis=-1)
    sin64 = np.concatenate([-sin, sin], axis=-1)
    reps = LANES // HEAD_DIM
    return jnp.asarray(np.tile(cos64, (1, reps)), F32), jnp.asarray(np.tile(sin64, (1, reps)), F32)


def _router_tables(n):
    r = np.arange(ROUTER_CHUNK)
    utri = r[:, None] <= r[None, :]
    rows = np.arange((n // ROUTER_CHUNK) * N_EXPERTS)
    same_e = (rows[:, None] % N_EXPERTS) == (rows[None, :] % N_EXPERTS)
    earlier = (rows[None, :] // N_EXPERTS) < (rows[:, None] // N_EXPERTS)
    chunk_of_token = (np.arange(n)[:, None] // ROUTER_CHUNK) == np.arange(LANES)[None, :]
    return tuple(jnp.asarray(t.astype(np.float32), BF16) for t in (utri, same_e & earlier, chunk_of_token))


def _head_ones():
    r = np.arange(MXU_DIM) // HEAD_DIM
    return jnp.asarray((r[:, None] == r[None, :]).astype(np.float32), BF16)


def kernel(x, c, ctx, c_ctx, ada_w, ada_b, norm1_g, norm2_g, ev_w_in, ev_q_g, ev_k_g, ev_conv_w, ev_conv_b,
           ev_ln_g, ev_ln_b, ev_w_out, sc_w_in, sc_conv_w, sc_w_out, moe_w_r, moe_w_gate, moe_w_up,
           moe_w_down, final_g):
    b, n, d = x.shape
    depth = ada_w.shape[0]
    assert depth == 2 and b < SUBLANES and n % ROUTER_CHUNK == 0

    cos2, sin2 = _rope_tables(n)
    tables = _router_tables(n)
    ones_bd = _head_ones()

    cvecs = jnp.zeros((SUBLANES, d), F32).at[:b].set(c).at[b].set(c_ctx)
    mods = _ada_mod(cvecs, ada_w, ada_b).reshape(depth, SUBLANES, N_MOD, d)
    w_r = jnp.pad(moe_w_r, ((0, 0), (0, 0), (0, LANES - N_EXPERTS))).astype(BF16)

    qg = jnp.tile(ev_q_g[0], N_Q_HEADS)[None, :]
    kg = jnp.tile(ev_k_g[0], N_KV_HEADS)[None, :]
    qt, k, vt, glu = _inproj0(x, mods[0], norm1_g[0:1], ev_w_in, qg, kg, cos2, sin2, ones_bd)
    k_ctx, vt_ctx = _ctxkv(ctx, mods[0], b, norm1_g[0:1], ev_w_in, kg, ones_bd)
    attn = _attention(qt, k_ctx, k, vt_ctx, vt)
    h, hm, logits = _outproj0(attn, glu, ev_conv_w[0], ev_conv_b[0:1], ev_ln_g[0:1], ev_ln_b[0:1],
                              ev_w_out[0].astype(BF16), x, mods[0], norm2_g[0:1], w_r[0])
    moe0 = _ec_moe(hm, logits, moe_w_gate, moe_w_up, moe_w_down, 0, tables)

    h, hm, logits = _mixer1(h, moe0, mods[0], mods[1], norm1_g[1:2], sc_w_in, sc_conv_w[0], sc_w_out,
                            norm2_g[1:2], w_r[1])
    moe1 = _ec_moe(hm, logits, moe_w_gate, moe_w_up, moe_w_down, 1, tables)

    return _final(h, moe1, mods[1], final_g[None, :])
```

```python
import functools

import jax
import jax.numpy as jnp
import numpy as np
from jax import lax
from jax.experimental import pallas as pl
from jax.experimental.pallas import tpu as pltpu

F32 = jnp.float32
BF16 = jnp.bfloat16

HEAD_DIM = 64
N_Q_HEADS = 8
N_KV_HEADS = 2
GRID_W = 64
ROPE_THETA = 10000.0
CONF_K = 31
SC_K = 3
N_EXPERTS = 16
EC_CAPACITY = 2
N_MOD = 6
EPS = 1e-6

ATTN_W = N_Q_HEADS * HEAD_DIM
KV_W = N_KV_HEADS * HEAD_DIM
Q_PER_KV = N_Q_HEADS // N_KV_HEADS
QK_SCALE = HEAD_DIM ** -0.5 * 1.4426950408889634

LANES = 128
SUBLANES = 8
MXU_DIM = 256
VMEM_LIMIT_BYTES = 60000 * 1024

CONV_HALO = 16
ROUTER_CHUNK = LANES
REFINE_STEPS = 16
INPROJ_PARTS = 2
MIXER_PARTS = 1


def _params(*sem):
    return pltpu.CompilerParams(dimension_semantics=sem, vmem_limit_bytes=VMEM_LIMIT_BYTES)


def _norm_mod(x, g, shift, scale):
    ms = jnp.mean(x * x, axis=-1, keepdims=True)
    y = x * lax.rsqrt(ms + EPS) * g
    return y * (1.0 + scale) + shift


def _head_sumsq(x, ones_blockdiag):
    return jnp.dot((x * x).astype(BF16), ones_blockdiag, preferred_element_type=F32)


def _swap_half(x):
    w = x.shape[-1]
    lane = lax.broadcasted_iota(jnp.int32, x.shape, 1)
    first = (lane % HEAD_DIM) < (HEAD_DIM // 2)
    return jnp.where(first, pltpu.roll(x, w - HEAD_DIM // 2, axis=1), pltpu.roll(x, HEAD_DIM // 2, axis=1))


def _silu(x):
    return x * jax.nn.sigmoid(x)


def _tile_transpose8(vs):
    sub = lax.broadcasted_iota(jnp.int32, vs[0].shape, 1)
    for d in (4, 2, 1):
        keep = (sub & d) == 0
        out = list(vs)
        for i in range(SUBLANES):
            if i & d == 0:
                a, b = vs[i], vs[i + d]
                out[i] = jnp.where(keep, a, pltpu.roll(b, d, axis=1))
                out[i + d] = jnp.where(keep, pltpu.roll(a, SUBLANES - d, axis=1), b)
        vs = out
    return vs


def _rows_to_tiles(x):
    r = x.shape[0]
    vs = [x[:, a * LANES:(a + 1) * LANES].reshape(r // SUBLANES, SUBLANES, LANES) for a in range(SUBLANES)]
    return jnp.stack(_tile_transpose8(vs), axis=1).reshape(r, SUBLANES, LANES)


def _tiles_to_rows(x3):
    r = x3.shape[0]
    x4 = x3.reshape(r // SUBLANES, SUBLANES, SUBLANES, LANES)
    vs = _tile_transpose8([x4[:, j] for j in range(SUBLANES)])
    return jnp.concatenate([v.reshape(r, LANES) for v in vs], axis=1)


def _store_vt_with_ones(vt_ref, cols, v):
    vt = v.T
    ones = jnp.ones((HEAD_DIM, v.shape[0]), F32)
    for j in range(N_KV_HEADS):
        vt_ref[0, j, :, cols] = jnp.concatenate([vt[j * HEAD_DIM:(j + 1) * HEAD_DIM, :], ones], axis=0).astype(BF16)


def _ada_body(c_ref, w_ref, b_ref, o_ref):
    s = _silu(c_ref[...]).astype(BF16)
    o_ref[0] = jnp.dot(s, w_ref[0].astype(BF16), preferred_element_type=F32) + b_ref[0]


def _ada_mod(cvecs, ada_w, ada_b):
    n_layers, d, n_out = ada_w.shape
    tn = n_out // 2
    return pl.pallas_call(
        _ada_body,
        grid=(n_layers, n_out // tn),
        in_specs=[
            pl.BlockSpec((SUBLANES, d), lambda l, j: (0, 0)),
            pl.BlockSpec((1, d, tn), lambda l, j: (l, 0, j)),
            pl.BlockSpec((1, 1, tn), lambda l, j: (l, 0, j)),
        ],
        out_specs=pl.BlockSpec((1, SUBLANES, tn), lambda l, j: (l, 0, j)),
        out_shape=jax.ShapeDtypeStruct((n_layers, SUBLANES, n_out), F32),
        compiler_params=_params("arbitrary", "arbitrary"),
        name="ada_mod",
    )(cvecs, ada_w, ada_b.reshape(n_layers, 1, n_out))


def _inproj0_body(h_ref, mod_ref, g1_ref, w_ref, qg_ref, kg_ref, cos_ref, sin_ref, ones_ref,
                  qt_ref, k_ref, vt_ref, glu_ref):
    ones = ones_ref[...]
    tm = h_ref.shape[1]
    part = tm // INPROJ_PARTS
    for p in range(INPROJ_PARTS):
        rows = slice(p * part, (p + 1) * part)
        hn = _norm_mod(h_ref[0, rows, :], g1_ref[...], mod_ref[0, 0:1, :], mod_ref[0, 1:2, :])
        proj = jnp.dot(hn.astype(BF16), w_ref[0].astype(BF16), preferred_element_type=F32)
        cos2 = cos_ref[rows, :]
        sin2 = sin_ref[rows, :]

        q = proj[:, :ATTN_W]
        ssq = jnp.concatenate([_head_sumsq(q[:, :MXU_DIM], ones), _head_sumsq(q[:, MXU_DIM:], ones)], axis=1)
        qn = q * lax.rsqrt(ssq * (1.0 / HEAD_DIM) + EPS) * qg_ref[...]
        cos = jnp.concatenate([cos2] * (ATTN_W // LANES), axis=1)
        sin = jnp.concatenate([sin2] * (ATTN_W // LANES), axis=1)
        qr = (qn * cos + _swap_half(qn) * sin) * QK_SCALE
        qt_ref[0, :, rows] = qr.T.astype(BF16)

        k = proj[:, ATTN_W:ATTN_W + KV_W]
        kn = k * lax.rsqrt(_head_sumsq(k, ones[:KV_W, :KV_W]) * (1.0 / HEAD_DIM) + EPS) * kg_ref[...]
        k_ref[0, rows, :] = (kn * cos2 + _swap_half(kn) * sin2).astype(BF16)
        _store_vt_with_ones(vt_ref, rows, proj[:, ATTN_W + KV_W:ATTN_W + 2 * KV_W])

        c0 = ATTN_W + 2 * KV_W
        cc = (proj.shape[1] - c0) // 2
        glu_ref[0, rows, :] = proj[:, c0:c0 + cc] * jax.nn.sigmoid(proj[:, c0 + cc:])


def _inproj0(h, mods, g1, w_in, qg, kg, cos2, sin2, ones_bd, tm=512):
    b, n, d = h.shape
    n_in = w_in.shape[-1]
    conv_ch = (n_in - ATTN_W - 2 * KV_W) // 2
    return pl.pallas_call(
        _inproj0_body,
        grid=(b, n // tm),
        in_specs=[
            pl.BlockSpec((1, tm, d), lambda bi, i: (bi, i, 0)),
            pl.BlockSpec((1, N_MOD, d), lambda bi, i: (bi, 0, 0)),
            pl.BlockSpec((1, d), lambda bi, i: (0, 0)),
            pl.BlockSpec((1, d, n_in), lambda bi, i: (0, 0, 0), pipeline_mode=pl.Buffered(1)),
            pl.BlockSpec((1, ATTN_W), lambda bi, i: (0, 0)),
            pl.BlockSpec((1, KV_W), lambda bi, i: (0, 0)),
            pl.BlockSpec((tm, LANES), lambda bi, i: (i, 0)),
            pl.BlockSpec((tm, LANES), lambda bi, i: (i, 0)),
            pl.BlockSpec((MXU_DIM, MXU_DIM), lambda bi, i: (0, 0)),
        ],
        out_specs=[
            pl.BlockSpec((1, ATTN_W, tm), lambda bi, i: (bi, 0, i)),
            pl.BlockSpec((1, tm, KV_W), lambda bi, i: (bi, i, 0)),
            pl.BlockSpec((1, N_KV_HEADS, 2 * HEAD_DIM, tm), lambda bi, i: (bi, 0, 0, i)),
            pl.BlockSpec((1, tm, conv_ch), lambda bi, i: (bi, i, 0)),
        ],
        out_shape=[
            jax.ShapeDtypeStruct((b, ATTN_W, n), BF16),
            jax.ShapeDtypeStruct((b, n, KV_W), BF16),
            jax.ShapeDtypeStruct((b, N_KV_HEADS, 2 * HEAD_DIM, n), BF16),
            jax.ShapeDtypeStruct((b, n, conv_ch), F32),
        ],
        compiler_params=_params("arbitrary", "arbitrary"),
        name="inproj0",
    )(h, mods, g1, w_in, qg, kg, cos2, sin2, ones_bd)


def _ctxkv_body(x_ref, mod_ref, g1_ref, w_ref, kg_ref, ones_ref, k_ref, vt_ref):
    hn = _norm_mod(x_ref[0], g1_ref[...], mod_ref[0, 0:1, :], mod_ref[0, 1:2, :])
    proj = jnp.dot(hn.astype(BF16), w_ref[0].astype(BF16), preferred_element_type=F32)
    k = proj[:, :KV_W]
    kn = k * lax.rsqrt(_head_sumsq(k, ones_ref[...][:KV_W, :KV_W]) * (1.0 / HEAD_DIM) + EPS) * kg_ref[...]
    k_ref[0] = kn.astype(BF16)
    _store_vt_with_ones(vt_ref, slice(None), proj[:, KV_W:])


def _ctxkv(ctx, mods, ctx_row, g1, w_in, kg, ones_bd):
    b, t, d = ctx.shape
    assert ATTN_W % (2 * KV_W) == 0
    kv_block = ATTN_W // (2 * KV_W)
    return pl.pallas_call(
        _ctxkv_body,
        grid=(b,),
        in_specs=[
            pl.BlockSpec((1, t, d), lambda bi: (bi, 0, 0)),
            pl.BlockSpec((1, N_MOD, d), lambda bi: (ctx_row, 0, 0)),
            pl.BlockSpec((1, d), lambda bi: (0, 0)),
            pl.BlockSpec((1, d, 2 * KV_W), lambda bi: (0, 0, kv_block)),
            pl.BlockSpec((1, KV_W), lambda bi: (0, 0)),
            pl.BlockSpec((MXU_DIM, MXU_DIM), lambda bi: (0, 0)),
        ],
        out_specs=[
            pl.BlockSpec((1, t, KV_W), lambda bi: (bi, 0, 0)),
            pl.BlockSpec((1, N_KV_HEADS, 2 * HEAD_DIM, t), lambda bi: (bi, 0, 0, 0)),
        ],
        out_shape=[
            jax.ShapeDtypeStruct((b, t, KV_W), BF16),
            jax.ShapeDtypeStruct((b, N_KV_HEADS, 2 * HEAD_DIM, t), BF16),
        ],
        compiler_params=_params("arbitrary"),
        name="ctx_kv",
    )(ctx, mods, g1, w_in, kg, ones_bd)


ATTN_KEY_CHUNK = 512


def _attn_body(qt_ref, kc_ref, k_ref, vtc_ref, vt_ref, o_ref, s0, s1, p0, p1, ot):
    s_bufs = (s0, s1)
    p_bufs = (p0, p1)
    t_ctx = kc_ref.shape[1]
    n = k_ref.shape[1]
    tq = qt_ref.shape[2]

    def column_max(run, sc):
        part = jnp.max(sc.reshape(sc.shape[0] // SUBLANES, SUBLANES, tq), axis=0)
        return part if run is None else jnp.maximum(run, part)

    assert t_ctx % SUBLANES == 0 and n % ATTN_KEY_CHUNK == 0
    chunks = [(0, t_ctx, None)] + [(t_ctx + lo, ATTN_KEY_CHUNK, lo) for lo in range(0, n, ATTN_KEY_CHUNK)]

    def padded_query(h):
        qt = qt_ref[0, h * HEAD_DIM:(h + 1) * HEAD_DIM, :]
        zeros = jnp.zeros_like(qt)
        return jnp.concatenate([qt, zeros] if h // Q_PER_KV == 0 else [zeros, qt], axis=0)

    def score_chunk(h, qpad, run, chunk):
        row, size, lo = chunk
        keys = kc_ref[0] if lo is None else k_ref[0, lo:lo + size, :]
        sc = jnp.dot(keys, qpad, preferred_element_type=F32)
        s_bufs[h % 2][row:row + size, :] = sc
        return column_max(run, sc)

    def exp_chunk(h, m, chunk):
        row, size, _ = chunk
        p_bufs[h % 2][row:row + size, :] = jnp.exp2(s_bufs[h % 2][row:row + size, :] - m).astype(BF16)

    def value_chunk(h, acc, chunk):
        row, size, lo = chunk
        j = h // Q_PER_KV
        vt = vtc_ref[0, j] if lo is None else vt_ref[0, j, :, lo:lo + size]
        part = jnp.dot(vt, p_bufs[h % 2][row:row + size, :], preferred_element_type=F32)
        return part if acc is None else acc + part

    m = None
    for h in range(-1, N_Q_HEADS + 1):
        qpad = padded_query(h + 1) if h + 1 < N_Q_HEADS else None
        run = acc = None
        for chunk in chunks:
            if qpad is not None:
                run = score_chunk(h + 1, qpad, run, chunk)
            if 0 <= h - 1:
                acc = value_chunk(h - 1, acc, chunk)
            if 0 <= h < N_Q_HEADS:
                exp_chunk(h, m, chunk)
        if acc is not None:
            ot[(h - 1) * HEAD_DIM:h * HEAD_DIM, :] = acc[:HEAD_DIM, :] / acc[HEAD_DIM:HEAD_DIM + 1, :]
        m = jnp.max(run, axis=0, keepdims=True) if run is not None else None
    o_ref[0] = ot[...].T.astype(BF16)


def _attention(qt, k_ctx, k, vt_ctx, vt, tq=512):
    b, width, n = qt.shape
    t_ctx = k_ctx.shape[1]
    t_all = t_ctx + n
    return pl.pallas_call(
        _attn_body,
        grid=(b, n // tq),
        scratch_shapes=[pltpu.VMEM((t_all, tq), F32), pltpu.VMEM((t_all, tq), F32),
                        pltpu.VMEM((t_all, tq), BF16), pltpu.VMEM((t_all, tq), BF16),
                        pltpu.VMEM((width, tq), F32)],
        in_specs=[
            pl.BlockSpec((1, width, tq), lambda bi, i: (bi, 0, i)),
            pl.BlockSpec((1, t_ctx, KV_W), lambda bi, i: (bi, 0, 0)),
            pl.BlockSpec((1, n, KV_W), lambda bi, i: (bi, 0, 0)),
            pl.BlockSpec((1, N_KV_HEADS, 2 * HEAD_DIM, t_ctx), lambda bi, i: (bi, 0, 0, 0)),
            pl.BlockSpec((1, N_KV_HEADS, 2 * HEAD_DIM, n), lambda bi, i: (bi, 0, 0, 0)),
        ],
        out_specs=pl.BlockSpec((1, tq, width), lambda bi, i: (bi, i, 0)),
        out_shape=jax.ShapeDtypeStruct((b, n, width), BF16),
        compiler_params=_params("arbitrary", "arbitrary"),
        name="attention",
    )(qt, k_ctx, k, vt_ctx, vt)


CONV_ROWS = 64


def _conformer_rows(window, w_ref, cb_ref, lg_ref, lb_ref):
    tm = window.shape[0] - 2 * CONV_HALO
    ch = window.shape[1]
    win = CONV_ROWS + 2 * CONV_HALO
    off = CONV_HALO - CONF_K // 2
    cols = []
    for t in range(ch // LANES):
        lanes = slice(t * LANES, (t + 1) * LANES)
        blocks = []
        for r in range(tm // CONV_ROWS):
            x = window[r * CONV_ROWS:r * CONV_ROWS + win, lanes]
            acc = jnp.zeros((CONV_ROWS, LANES), F32)
            for sub in range(SUBLANES):
                xs = x if sub == 0 else pltpu.roll(x, win - sub, axis=0)
                for a in range(2 * CONV_HALO // SUBLANES):
                    k = a * SUBLANES + sub - off
                    if 0 <= k < CONF_K:
                        acc = acc + xs[a * SUBLANES:a * SUBLANES + CONV_ROWS, :] * w_ref[k:k + 1, lanes]
            blocks.append(acc)
        cols.append(jnp.concatenate(blocks, axis=0))
    y = jnp.concatenate(cols, axis=1) + cb_ref[...]
    mu = jnp.mean(y, axis=-1, keepdims=True)
    yc = y - mu
    var = jnp.mean(yc * yc, axis=-1, keepdims=True)
    return _silu(yc * lax.rsqrt(var + EPS) * lg_ref[...] + lb_ref[...])


def _residual_router(mix, h, mod_ref, g2_ref, wr_ref, h_out, hm_out, lg_out, rows=slice(None)):
    h1 = h + mod_ref[0, 2:3, :] * mix
    h_out[0, rows] = h1
    hm = _norm_mod(h1, g2_ref[...], mod_ref[0, 3:4, :], mod_ref[0, 4:5, :])
    hm_out[0, rows] = _rows_to_tiles(hm)
    lg_out[0, rows] = jnp.dot(hm.astype(BF16), wr_ref[...], preferred_element_type=F32)


def _outproj0_body(attn_ref, glu_ref, glup_ref, glun_ref, cw_ref, cb_ref, lg_ref, lb_ref, w_ref,
                   h_ref, mod_ref, g2_ref, wr_ref, h_out, hm_out, lg_out):
    i = pl.program_id(1)
    last = pl.num_programs(1) - 1
    before = jnp.where(i > 0, glup_ref[0], 0.0)
    after = jnp.where(i < last, glun_ref[0], 0.0)
    conf = _conformer_rows(jnp.concatenate([before, glu_ref[0], after], axis=0), cw_ref, cb_ref, lg_ref, lb_ref)
    a = jnp.concatenate([attn_ref[0], conf.astype(BF16)], axis=1)
    mix = jnp.dot(a, w_ref[...], preferred_element_type=F32)
    _residual_router(mix, h_ref[0], mod_ref, g2_ref, wr_ref, h_out, hm_out, lg_out)


def _tile_spec(tm, d):
    assert d == SUBLANES * LANES
    return pl.BlockSpec((1, tm, SUBLANES, LANES), lambda bi, i: (bi, i, 0, 0))


def _token_out_specs(b, n, d, tm):
    specs = [
        pl.BlockSpec((1, tm, d), lambda bi, i: (bi, i, 0)),
        _tile_spec(tm, d),
        pl.BlockSpec((1, tm, LANES), lambda bi, i: (bi, i, 0)),
    ]
    shapes = [
        jax.ShapeDtypeStruct((b, n, d), F32),
        jax.ShapeDtypeStruct((b, n, SUBLANES, d // SUBLANES), F32),
        jax.ShapeDtypeStruct((b, n, LANES), F32),
    ]
    return specs, shapes


def _outproj0(attn, glu, conv_w, conv_b, ln_g, ln_b, w_out, h, mods, g2, w_r, tm=512):
    b, n, d = h.shape
    ch = glu.shape[-1]
    per = tm // CONV_HALO
    n_halo = n // CONV_HALO
    vec = pl.BlockSpec((1, ch), lambda bi, i: (0, 0))
    out_specs, out_shape = _token_out_specs(b, n, d, tm)
    return pl.pallas_call(
        _outproj0_body,
        grid=(b, n // tm),
        in_specs=[
            pl.BlockSpec((1, tm, ATTN_W), lambda bi, i: (bi, i, 0)),
            pl.BlockSpec((1, tm, ch), lambda bi, i: (bi, i, 0)),
            pl.BlockSpec((1, CONV_HALO, ch), lambda bi, i: (bi, jnp.maximum(i * per - 1, 0), 0)),
            pl.BlockSpec((1, CONV_HALO, ch), lambda bi, i: (bi, jnp.minimum((i + 1) * per, n_halo - 1), 0)),
            pl.BlockSpec((CONF_K, ch), lambda bi, i: (0, 0)),
            vec, vec, vec,
            pl.BlockSpec((d, d), lambda bi, i: (0, 0)),
            pl.BlockSpec((1, tm, d), lambda bi, i: (bi, i, 0)),
            pl.BlockSpec((1, N_MOD, d), lambda bi, i: (bi, 0, 0)),
            pl.BlockSpec((1, d), lambda bi, i: (0, 0)),
            pl.BlockSpec((d, LANES), lambda bi, i: (0, 0)),
        ],
        out_specs=out_specs,
        out_shape=out_shape,
        compiler_params=_params("arbitrary", "arbitrary"),
        name="outproj0",
    )(attn, glu, glu, glu, conv_w, conv_b, ln_g, ln_b, w_out, h, mods, g2, w_r)


def _mixer1_body(h_ref, hp_ref, hn_ref, moe_ref, moep_ref, moen_ref, modp_ref, mod_ref, g1_ref, w_ref,
                 cw_ref, wo_ref, g2_ref, wr_ref, h_out, hm_out, lg_out):
    i = pl.program_id(1)
    last = pl.num_programs(1) - 1
    tm, d = h_ref.shape[1:]
    gate_prev = modp_ref[0, 5:6, :]

    def stream(h_blk, moe_blk, rows=slice(None)):
        return h_blk[0, rows] + gate_prev * _tiles_to_rows(moe_blk[0, rows])

    part = tm // MIXER_PARTS
    win = part + 2 * SUBLANES
    for p in range(MIXER_PARTS):
        rows = slice(p * part, (p + 1) * part)
        h = stream(h_ref, moe_ref, rows)
        above = (stream(hp_ref, moep_ref) if p == 0
                 else stream(h_ref, moe_ref, slice(p * part - SUBLANES, p * part)))
        below = (stream(hn_ref, moen_ref) if p == MIXER_PARTS - 1
                 else stream(h_ref, moe_ref, slice((p + 1) * part, (p + 1) * part + SUBLANES)))
        hn = _norm_mod(jnp.concatenate([above, h, below], axis=0), g1_ref[...], mod_ref[0, 0:1, :], mod_ref[0, 1:2, :])
        proj = jnp.dot(hn.astype(BF16), w_ref[0].astype(BF16), preferred_element_type=F32)
        z = proj[:, d:2 * d] * proj[:, 2 * d:]
        r = lax.broadcasted_iota(jnp.int32, z.shape, 0)
        if p == 0:
            z = jnp.where((r < SUBLANES) & (i == 0), 0.0, z)
        if p == MIXER_PARTS - 1:
            z = jnp.where((r >= part + SUBLANES) & (i == last), 0.0, z)
        y = (pltpu.roll(z, 1, axis=0) * cw_ref[0:1, :] + z * cw_ref[1:2, :]
             + pltpu.roll(z, win - 1, axis=0) * cw_ref[2:3, :])[SUBLANES:SUBLANES + part]
        gated = proj[SUBLANES:SUBLANES + part, :d] * y
        mix = jnp.dot(gated.astype(BF16), wo_ref[0].astype(BF16), preferred_element_type=F32)
        _residual_router(mix, h, mod_ref, g2_ref, wr_ref, h_out, hm_out, lg_out, rows)


def _mixer1(h, moe, mods_prev, mods, g1, w_in, conv_w, w_out, g2, w_r, tm=512):
    b, n, d = h.shape
    per = tm // SUBLANES
    n8 = n // SUBLANES
    before = lambda bi, i: (bi, jnp.maximum(i * per - 1, 0), 0)
    after = lambda bi, i: (bi, jnp.minimum((i + 1) * per, n8 - 1), 0)
    tok = pl.BlockSpec((1, tm, d), lambda bi, i: (bi, i, 0))
    modspec = pl.BlockSpec((1, N_MOD, d), lambda bi, i: (bi, 0, 0))
    vec = pl.BlockSpec((1, d), lambda bi, i: (0, 0))
    out_specs, out_shape = _token_out_specs(b, n, d, tm)
    return pl.pallas_call(
        _mixer1_body,
        grid=(b, n // tm),
        in_specs=[
            tok,
            pl.BlockSpec((1, SUBLANES, d), before),
            pl.BlockSpec((1, SUBLANES, d), after),
            _tile_spec(tm, d),
            pl.BlockSpec((1, SUBLANES, SUBLANES, LANES), lambda bi, i: before(bi, i) + (0,)),
            pl.BlockSpec((1, SUBLANES, SUBLANES, LANES), lambda bi, i: after(bi, i) + (0,)),
            modspec, modspec, vec,
            pl.BlockSpec((1, d, 3 * d), lambda bi, i: (0, 0, 0), pipeline_mode=pl.Buffered(1)),
            pl.BlockSpec((SC_K, d), lambda bi, i: (0, 0)),
            pl.BlockSpec((1, d, d), lambda bi, i: (0, 0, 0), pipeline_mode=pl.Buffered(1)),
            vec,
            pl.BlockSpec((d, LANES), lambda bi, i: (0, 0)),
        ],
        out_specs=out_specs,
        out_shape=out_shape,
        compiler_params=_params("arbitrary", "arbitrary"),
        name="mixer1",
    )(h, h, h, moe, moe, moe, mods_prev, mods, g1, w_in, conv_w, w_out, g2, w_r)


def _stack_chunks(x):
    n = x.shape[1]
    return jnp.concatenate([x[:, c * ROUTER_CHUNK:(c + 1) * ROUTER_CHUNK] for c in range(n // ROUTER_CHUNK)], axis=0)


def _exclusive_rank(flags, utri, chunk_lt):
    incl = jnp.dot(flags.astype(BF16), utri, preferred_element_type=F32)
    tot = jnp.broadcast_to(incl[:, ROUTER_CHUNK - 1:ROUTER_CHUNK], incl.shape)
    base = jnp.dot(chunk_lt, tot.astype(BF16), preferred_element_type=F32)
    return incl - flags + base


def _router_body(lg_ref, utri_ref, lt_ref, blk_ref, idx_ref, aff_ref, loc_scr, tot_scr, end_scr, *, cap):
    n = lg_ref.shape[1]
    n_chunks = n // ROUTER_CHUNK
    lg = lg_ref[0]
    lane = lax.broadcasted_iota(jnp.int32, lg.shape, 1)
    valid = lane < N_EXPERTS
    x = jnp.where(valid, lg, -jnp.inf)
    ex = jnp.where(valid, jnp.exp(x - jnp.max(x, axis=-1, keepdims=True)), 0.0)
    aff = ex / jnp.sum(ex, axis=-1, keepdims=True)

    aff_t = aff.T[:N_EXPERTS, :]
    aff_ref[0] = aff_t

    def count_ge(t):
        return jnp.sum(jnp.where(aff_t >= t, 1.0, 0.0), axis=-1, keepdims=True)

    def bit_step(i, bits):
        cand = bits | jnp.left_shift(jnp.int32(1), 30 - i)
        return jnp.where(count_ge(lax.bitcast_convert_type(cand, F32)) >= cap, cand, bits)

    bits = lax.fori_loop(0, 31, bit_step, jnp.zeros((N_EXPERTS, 1), jnp.int32))

    def refine(i, lo_hi):
        lo, hi = lo_hi
        mid = (lo + hi) * 0.5
        ok = count_ge(mid) >= cap
        return jnp.where(ok, mid, lo), jnp.where(ok, hi, mid)

    thr, _ = lax.fori_loop(0, REFINE_STEPS, refine,
                           (lax.bitcast_convert_type(bits, F32), lax.bitcast_convert_type(bits + 1, F32)))
    gt = jnp.where(aff_t > thr, 1.0, 0.0)
    eq = jnp.where(aff_t == thr, 1.0, 0.0)
    need = cap - jnp.sum(gt, axis=-1, keepdims=True)
    utri = utri_ref[...]
    chunk_lt = lt_ref[...]
    gt_s = _stack_chunks(gt)
    eq_s = _stack_chunks(eq)
    need_s = jnp.concatenate([need] * n_chunks, axis=0)
    sel = jnp.where((gt_s > 0) | ((eq_s > 0) & (_exclusive_rank(eq_s, utri, chunk_lt) < need_s)), 1.0, 0.0)

    loc_scr[...] = jnp.dot(sel.astype(BF16), utri, preferred_element_type=F32).astype(BF16)
    sel_t = jnp.concatenate([sel[c * N_EXPERTS:(c + 1) * N_EXPERTS, :] for c in range(n_chunks)], axis=1)
    tot = jnp.dot(sel_t.astype(BF16), blk_ref[...], preferred_element_type=F32)
    tot_scr[...] = tot
    end_scr[...] = jnp.dot(tot.astype(BF16), utri, preferred_element_type=F32)

    slot = lax.broadcasted_iota(jnp.int32, (cap, 1), 0).astype(F32)
    row_id = lax.broadcasted_iota(jnp.int32, (cap, n_chunks * N_EXPERTS), 1)

    def one_expert(e, carry):
        ends = end_scr[pl.ds(e, 1), :]
        before = ends <= slot
        chunk = jnp.sum(jnp.where(before, 1.0, 0.0), axis=-1, keepdims=True)
        base = jnp.sum(jnp.where(before, tot_scr[pl.ds(e, 1), :], 0.0), axis=-1, keepdims=True)
        pick = jnp.where(row_id == chunk.astype(jnp.int32) * N_EXPERTS + e, 1.0, 0.0).astype(BF16)
        counts = jnp.dot(pick, loc_scr[...], preferred_element_type=F32)
        inside = jnp.sum(jnp.where(counts <= slot - base, 1.0, 0.0), axis=-1, keepdims=True)
        idx_ref[0, e] = (chunk * ROUTER_CHUNK + inside).astype(jnp.int32)
        return carry

    lax.fori_loop(0, N_EXPERTS, one_expert, 0)


def _router(logits, utri, chunk_lt, chunk_of_token, cap):
    b, n, _ = logits.shape
    rows = (n // ROUTER_CHUNK) * N_EXPERTS
    assert n // ROUTER_CHUNK <= LANES
    return pl.pallas_call(
        functools.partial(_router_body, cap=cap),
        grid=(b,),
        in_specs=[
            pl.BlockSpec((1, n, LANES), lambda bi: (bi, 0, 0)),
            pl.BlockSpec((ROUTER_CHUNK, ROUTER_CHUNK), lambda bi: (0, 0)),
            pl.BlockSpec((rows, rows), lambda bi: (0, 0)),
            pl.BlockSpec((n, LANES), lambda bi: (0, 0)),
        ],
        out_specs=[
            pl.BlockSpec((1, N_EXPERTS, cap, 1), lambda bi: (bi, 0, 0, 0)),
            pl.BlockSpec((1, N_EXPERTS, n), lambda bi: (bi, 0, 0)),
        ],
        out_shape=[
            jax.ShapeDtypeStruct((b, N_EXPERTS, cap, 1), jnp.int32),
            jax.ShapeDtypeStruct((b, N_EXPERTS, n), F32),
        ],
        scratch_shapes=[pltpu.VMEM((rows, ROUTER_CHUNK), BF16), pltpu.VMEM((N_EXPERTS, LANES), F32),
                        pltpu.VMEM((N_EXPERTS, LANES), F32)],
        compiler_params=_params("arbitrary"),
        name="router",
    )(logits, utri, chunk_lt, chunk_of_token)


GATHER_UNROLL = 16
MOE_EXPERTS_PER_STEP = 1
MOE_SLOT_SPLIT = 1


def _gather_body(idx_ref, hm_ref, o_ref):
    cap = o_ref.shape[2]

    for k in range(o_ref.shape[1]):
        def group(g, carry, k=k):
            base = pl.multiple_of(g * GATHER_UNROLL, GATHER_UNROLL)
            halves = []
            for half in range(GATHER_UNROLL // SUBLANES):
                tiles = [hm_ref[0, idx_ref[k, 0, base + half * SUBLANES + u]][None] for u in range(SUBLANES)]
                halves.append(_tile_transpose8(tiles))
            rows = jnp.concatenate([jnp.concatenate([h[a][0] for h in halves], axis=0) for a in range(SUBLANES)],
                                   axis=1)
            o_ref[0, k, pl.ds(base, GATHER_UNROLL), :] = rows.astype(BF16)
            return carry

        lax.fori_loop(0, cap // GATHER_UNROLL, group, 0)


def _gather(idx, hm):
    b, n = hm.shape[:2]
    groups, cap = idx.shape[0] // b, idx.shape[-1]
    d = SUBLANES * LANES
    eps = MOE_EXPERTS_PER_STEP
    steps = groups // eps
    return pl.pallas_call(
        _gather_body,
        grid=(b, steps),
        in_specs=[
            pl.BlockSpec((eps, 1, cap), lambda bi, e: (bi * steps + e, 0, 0), memory_space=pltpu.SMEM),
            pl.BlockSpec((1, n, SUBLANES, LANES), lambda bi, e: (bi, 0, 0, 0)),
        ],
        out_specs=pl.BlockSpec((1, eps, cap, d), lambda bi, e: (bi, e, 0, 0)),
        out_shape=jax.ShapeDtypeStruct((b, groups, cap, d), BF16),
        compiler_params=_params("arbitrary", "arbitrary"),
        name="moe_gather",
    )(idx, hm)


FFN_SAMPLES_PER_STEP = 2


def _ffn_body(x_ref, wg_ref, wu_ref, wd_ref, o_ref):
    cap = x_ref.shape[2]
    x = jnp.concatenate([x_ref[s, 0] for s in range(x_ref.shape[0])], axis=0)
    hg = jnp.dot(x, wg_ref[0, 0].astype(BF16), preferred_element_type=F32)
    hu = jnp.dot(x, wu_ref[0, 0].astype(BF16), preferred_element_type=F32)
    y = jnp.dot((_silu(hg) * hu).astype(BF16), wd_ref[0, 0].astype(BF16), preferred_element_type=F32)
    for s in range(x_ref.shape[0]):
        o_ref[s, 0] = y[s * cap:(s + 1) * cap]


def _expert_ffn(xs, w_gate, w_up, w_down, layer):
    b, n_e, cap, d = xs.shape
    f = w_gate.shape[-1]
    sps = FFN_SAMPLES_PER_STEP if b % FFN_SAMPLES_PER_STEP == 0 else 1
    rows = pl.BlockSpec((sps, 1, cap, d), lambda e, bi: (bi, e, 0, 0))
    return pl.pallas_call(
        _ffn_body,
        grid=(n_e, b // sps),
        in_specs=[
            rows,
            pl.BlockSpec((1, 1, d, f), lambda e, bi: (layer, e, 0, 0)),
            pl.BlockSpec((1, 1, d, f), lambda e, bi: (layer, e, 0, 0)),
            pl.BlockSpec((1, 1, f, d), lambda e, bi: (layer, e, 0, 0)),
        ],
        out_specs=rows,
        out_shape=jax.ShapeDtypeStruct(xs.shape, F32),
        compiler_params=_params("arbitrary", "arbitrary"),
        name="moe_ffn",
    )(xs, w_gate, w_up, w_down)


SCATTER_UNROLL = 8


def _scatter_body(idx_ref, aff_ref, ys_ref, o_ref):
    cap = ys_ref.shape[2]

    @pl.when(pl.program_id(1) == 0)
    def _():
        o_ref[...] = jnp.zeros(o_ref.shape, F32)

    for k in range(ys_ref.shape[1]):
        def group(g, carry, k=k):
            base = pl.multiple_of(g * SCATTER_UNROLL, SCATTER_UNROLL)
            tiles = []
            for part in range(SCATTER_UNROLL // SUBLANES):
                y = ys_ref[0, k, pl.ds(base + part * SUBLANES, SUBLANES), :]
                tiles += _tile_transpose8([y[:, a * LANES:(a + 1) * LANES][None] for a in range(SUBLANES)])
            rows = [idx_ref[k, 0, base + u] for u in range(SCATTER_UNROLL)]
            sums = [o_ref[0, rows[u]] + tiles[u][0] * aff_ref[k, 0, rows[u]] for u in range(SCATTER_UNROLL)]
            for u in range(SCATTER_UNROLL):
                o_ref[0, rows[u]] = sums[u]
            return carry

        lax.fori_loop(0, cap // SCATTER_UNROLL, group, 0)


def _scatter(idx, aff, ys, n):
    b, groups, cap, d = ys.shape
    assert SCATTER_UNROLL % SUBLANES == 0 and cap % SCATTER_UNROLL == 0 and d == SUBLANES * LANES
    eps = MOE_EXPERTS_PER_STEP
    steps = groups // eps
    split = groups // N_EXPERTS
    assert split == 1 or eps == 1
    return pl.pallas_call(
        _scatter_body,
        grid=(b, steps),
        in_specs=[
            pl.BlockSpec((eps, 1, cap), lambda bi, e: (bi * steps + e, 0, 0), memory_space=pltpu.SMEM),
            pl.BlockSpec((eps, 1, n), lambda bi, e: ((bi * steps + e) // split, 0, 0), memory_space=pltpu.SMEM),
            pl.BlockSpec((1, eps, cap, d), lambda bi, e: (bi, e, 0, 0)),
        ],
        out_specs=pl.BlockSpec((1, n, SUBLANES, LANES), lambda bi, e: (bi, 0, 0, 0)),
        out_shape=jax.ShapeDtypeStruct((b, n, SUBLANES, LANES), F32),
        compiler_params=_params("arbitrary", "arbitrary"),
        name="moe_scatter",
    )(idx, aff, ys)


def _ec_moe(hm, logits, w_gate, w_up, w_down, layer, tables):
    b, n = hm.shape[:2]
    cap = max(1, EC_CAPACITY * n // N_EXPERTS)
    idx, aff = _router(logits, *tables, cap)
    split = MOE_SLOT_SPLIT
    idx = idx.reshape(b * N_EXPERTS * split, 1, cap // split)
    xs = _gather(idx, hm).reshape(b, N_EXPERTS, cap, -1)
    ys = _expert_ffn(xs, w_gate, w_up, w_down, layer)
    return _scatter(idx, aff.reshape(b * N_EXPERTS, 1, n), ys.reshape(b, N_EXPERTS * split, cap // split, -1), n)


def _final_body(h_ref, moe_ref, mod_ref, g_ref, o_ref):
    h = h_ref[0] + mod_ref[0, 5:6, :] * _tiles_to_rows(moe_ref[0])
    o_ref[0] = h * lax.rsqrt(jnp.mean(h * h, axis=-1, keepdims=True) + EPS) * g_ref[...]


def _final(h, moe, mods, g, tm=1024):
    b, n, d = h.shape
    tok = pl.BlockSpec((1, tm, d), lambda bi, i: (bi, i, 0))
    return pl.pallas_call(
        _final_body,
        grid=(b, n // tm),
        in_specs=[tok, _tile_spec(tm, d), pl.BlockSpec((1, N_MOD, d), lambda bi, i: (bi, 0, 0)),
                  pl.BlockSpec((1, d), lambda bi, i: (0, 0))],
        out_specs=tok,
        out_shape=jax.ShapeDtypeStruct((b, n, d), F32),
        compiler_params=_params("arbitrary", "arbitrary"),
        name="final_norm",
    )(h, moe, mods, g)


def _rope_tables(n):
    rows = n // GRID_W
    row = np.repeat(np.arange(rows, dtype=np.float64), GRID_W)
    col = np.tile(np.arange(GRID_W, dtype=np.float64), rows)
    axis_dim = HEAD_DIM // 2
    inv = ROPE_THETA ** (-np.arange(0, axis_dim, 2, dtype=np.float64) / axis_dim)
    ang = np.concatenate([row[:, None] * inv, col[:, None] * inv], axis=-1)
    cos = np.cos(ang)
    sin = np.sin(ang)
    cos64 = np.concatenate([cos, cos], axis=-1)
    sin64 = np.concatenate([-sin, sin], axis=-1)
    reps = LANES // HEAD_DIM
    return jnp.asarray(np.tile(cos64, (1, reps)), F32), jnp.asarray(np.tile(sin64, (1, reps)), F32)


def _router_tables(n):
    r = np.arange(ROUTER_CHUNK)
    utri = r[:, None] <= r[None, :]
    rows = np.arange((n // ROUTER_CHUNK) * N_EXPERTS)
    same_e = (rows[:, None] % N_EXPERTS) == (rows[None, :] % N_EXPERTS)
    earlier = (rows[None, :] // N_EXPERTS) < (rows[:, None] // N_EXPERTS)
    chunk_of_token = (np.arange(n)[:, None] // ROUTER_CHUNK) == np.arange(LANES)[None, :]
    return tuple(jnp.asarray(t.astype(np.float32), BF16) for t in (utri, same_e & earlier, chunk_of_token))


def _head_ones():
    r = np.arange(MXU_DIM) // HEAD_DIM
    return jnp.asarray((r[:, None] == r[None, :]).astype(np.float32), BF16)


def kernel(x, c, ctx, c_ctx, ada_w, ada_b, norm1_g, norm2_g, ev_w_in, ev_q_g, ev_k_g, ev_conv_w, ev_conv_b,
           ev_ln_g, ev_ln_b, ev_w_out, sc_w_in, sc_conv_w, sc_w_out, moe_w_r, moe_w_gate, moe_w_up,
           moe_w_down, final_g):
    b, n, d = x.shape
    depth = ada_w.shape[0]
    assert depth == 2 and b < SUBLANES and n % ROUTER_CHUNK == 0

    cos2, sin2 = _rope_tables(n)
    tables = _router_tables(n)
    ones_bd = _head_ones()

    cvecs = jnp.zeros((SUBLANES, d), F32).at[:b].set(c).at[b].set(c_ctx)
    mods = _ada_mod(cvecs, ada_w, ada_b).reshape(depth, SUBLANES, N_MOD, d)
    w_r = jnp.pad(moe_w_r, ((0, 0), (0, 0), (0, LANES - N_EXPERTS))).astype(BF16)

    qg = jnp.tile(ev_q_g[0], N_Q_HEADS)[None, :]
    kg = jnp.tile(ev_k_g[0], N_KV_HEADS)[None, :]
    qt, k, vt, glu = _inproj0(x, mods[0], norm1_g[0:1], ev_w_in, qg, kg, cos2, sin2, ones_bd)
    k_ctx, vt_ctx = _ctxkv(ctx, mods[0], b, norm1_g[0:1], ev_w_in, kg, ones_bd)
    attn = _attention(qt, k_ctx, k, vt_ctx, vt)
    h, hm, logits = _outproj0(attn, glu, ev_conv_w[0], ev_conv_b[0:1], ev_ln_g[0:1], ev_ln_b[0:1],
                              ev_w_out[0].astype(BF16), x, mods[0], norm2_g[0:1], w_r[0])
    moe0 = _ec_moe(hm, logits, moe_w_gate, moe_w_up, moe_w_down, 0, tables)

    h, hm, logits = _mixer1(h, moe0, mods[0], mods[1], norm1_g[1:2], sc_w_in, sc_conv_w[0], sc_w_out,
                            norm2_g[1:2], w_r[1])
    moe1 = _ec_moe(hm, logits, moe_w_gate, moe_w_up, moe_w_down, 1, tables)

    return _final(h, moe1, mods[1], final_g[None, :])
```

```python
import functools

import jax
import jax.numpy as jnp
import numpy as np
from jax import lax
from jax.experimental import pallas as pl
from jax.experimental.pallas import tpu as pltpu

F32 = jnp.float32
BF16 = jnp.bfloat16

HEAD_DIM = 64
N_Q_HEADS = 8
N_KV_HEADS = 2
GRID_W = 64
ROPE_THETA = 10000.0
CONF_K = 31
SC_K = 3
N_EXPERTS = 16
EC_CAPACITY = 2
N_MOD = 6
EPS = 1e-6

ATTN_W = N_Q_HEADS * HEAD_DIM
KV_W = N_KV_HEADS * HEAD_DIM
Q_PER_KV = N_Q_HEADS // N_KV_HEADS
QK_SCALE = HEAD_DIM ** -0.5 * 1.4426950408889634

LANES = 128
SUBLANES = 8
MXU_DIM = 256
VMEM_LIMIT_BYTES = 60000 * 1024

CONV_HALO = 16
ROUTER_CHUNK = LANES
REFINE_STEPS = 16
INPROJ_PARTS = 2
MIXER_PARTS = 1


def _params(*sem):
    return pltpu.CompilerParams(dimension_semantics=sem, vmem_limit_bytes=VMEM_LIMIT_BYTES)


def _norm_mod(x, g, shift, scale):
    ms = jnp.mean(x * x, axis=-1, keepdims=True)
    y = x * lax.rsqrt(ms + EPS) * g
    return y * (1.0 + scale) + shift


def _head_sumsq(x, ones_blockdiag):
    return jnp.dot((x * x).astype(BF16), ones_blockdiag, preferred_element_type=F32)


def _swap_half(x):
    w = x.shape[-1]
    lane = lax.broadcasted_iota(jnp.int32, x.shape, 1)
    first = (lane % HEAD_DIM) < (HEAD_DIM // 2)
    return jnp.where(first, pltpu.roll(x, w - HEAD_DIM // 2, axis=1), pltpu.roll(x, HEAD_DIM // 2, axis=1))


def _silu(x):
    return x * jax.nn.sigmoid(x)


def _tile_transpose8(vs):
    sub = lax.broadcasted_iota(jnp.int32, vs[0].shape, 1)
    for d in (4, 2, 1):
        keep = (sub & d) == 0
        out = list(vs)
        for i in range(SUBLANES):
            if i & d == 0:
                a, b = vs[i], vs[i + d]
                out[i] = jnp.where(keep, a, pltpu.roll(b, d, axis=1))
                out[i + d] = jnp.where(keep, pltpu.roll(a, SUBLANES - d, axis=1), b)
        vs = out
    return vs


def _rows_to_tiles(x):
    r = x.shape[0]
    vs = [x[:, a * LANES:(a + 1) * LANES].reshape(r // SUBLANES, SUBLANES, LANES) for a in range(SUBLANES)]
    return jnp.stack(_tile_transpose8(vs), axis=1).reshape(r, SUBLANES, LANES)


def _tiles_to_rows(x3):
    r = x3.shape[0]
    x4 = x3.reshape(r // SUBLANES, SUBLANES, SUBLANES, LANES)
    vs = _tile_transpose8([x4[:, j] for j in range(SUBLANES)])
    return jnp.concatenate([v.reshape(r, LANES) for v in vs], axis=1)


def _store_vt_with_ones(vt_ref, cols, v):
    vt = v.T
    ones = jnp.ones((HEAD_DIM, v.shape[0]), F32)
    for j in range(N_KV_HEADS):
        vt_ref[0, j, :, cols] = jnp.concatenate([vt[j * HEAD_DIM:(j + 1) * HEAD_DIM, :], ones], axis=0).astype(BF16)


def _ada_body(c_ref, w_ref, b_ref, o_ref):
    s = _silu(c_ref[...]).astype(BF16)
    o_ref[0] = jnp.dot(s, w_ref[0].astype(BF16), preferred_element_type=F32) + b_ref[0]


def _ada_mod(cvecs, ada_w, ada_b):
    n_layers, d, n_out = ada_w.shape
    tn = n_out // 2
    return pl.pallas_call(
        _ada_body,
        grid=(n_layers, n_out // tn),
        in_specs=[
            pl.BlockSpec((SUBLANES, d), lambda l, j: (0, 0)),
            pl.BlockSpec((1, d, tn), lambda l, j: (l, 0, j)),
            pl.BlockSpec((1, 1, tn), lambda l, j: (l, 0, j)),
        ],
        out_specs=pl.BlockSpec((1, SUBLANES, tn), lambda l, j: (l, 0, j)),
        out_shape=jax.ShapeDtypeStruct((n_layers, SUBLANES, n_out), F32),
        compiler_params=_params("arbitrary", "arbitrary"),
        name="ada_mod",
    )(cvecs, ada_w, ada_b.reshape(n_layers, 1, n_out))


def _inproj0_body(h_ref, mod_ref, g1_ref, w_ref, qg_ref, kg_ref, cos_ref, sin_ref, ones_ref,
                  qt_ref, k_ref, vt_ref, glu_ref):
    ones = ones_ref[...]
    tm = h_ref.shape[1]
    part = tm // INPROJ_PARTS
    for p in range(INPROJ_PARTS):
        rows = slice(p * part, (p + 1) * part)
        hn = _norm_mod(h_ref[0, rows, :], g1_ref[...], mod_ref[0, 0:1, :], mod_ref[0, 1:2, :])
        proj = jnp.dot(hn.astype(BF16), w_ref[0].astype(BF16), preferred_element_type=F32)
        cos2 = cos_ref[rows, :]
        sin2 = sin_ref[rows, :]

        q = proj[:, :ATTN_W]
        ssq = jnp.concatenate([_head_sumsq(q[:, :MXU_DIM], ones), _head_sumsq(q[:, MXU_DIM:], ones)], axis=1)
        qn = q * lax.rsqrt(ssq * (1.0 / HEAD_DIM) + EPS) * qg_ref[...]
        cos = jnp.concatenate([cos2] * (ATTN_W // LANES), axis=1)
        sin = jnp.concatenate([sin2] * (ATTN_W // LANES), axis=1)
        qr = (qn * cos + _swap_half(qn) * sin) * QK_SCALE
        qt_ref[0, :, rows] = qr.T.astype(BF16)

        k = proj[:, ATTN_W:ATTN_W + KV_W]
        kn = k * lax.rsqrt(_head_sumsq(k, ones[:KV_W, :KV_W]) * (1.0 / HEAD_DIM) + EPS) * kg_ref[...]
        k_ref[0, rows, :] = (kn * cos2 + _swap_half(kn) * sin2).astype(BF16)
        _store_vt_with_ones(vt_ref, rows, proj[:, ATTN_W + KV_W:ATTN_W + 2 * KV_W])

        c0 = ATTN_W + 2 * KV_W
        cc = (proj.shape[1] - c0) // 2
        glu_ref[0, rows, :] = proj[:, c0:c0 + cc] * jax.nn.sigmoid(proj[:, c0 + cc:])


def _inproj0(h, mods, g1, w_in, qg, kg, cos2, sin2, ones_bd, tm=512):
    b, n, d = h.shape
    n_in = w_in.shape[-1]
    conv_ch = (n_in - ATTN_W - 2 * KV_W) // 2
    return pl.pallas_call(
        _inproj0_body,
        grid=(b, n // tm),
        in_specs=[
            pl.BlockSpec((1, tm, d), lambda bi, i: (bi, i, 0)),
            pl.BlockSpec((1, N_MOD, d), lambda bi, i: (bi, 0, 0)),
            pl.BlockSpec((1, d), lambda bi, i: (0, 0)),
            pl.BlockSpec((1, d, n_in), lambda bi, i: (0, 0, 0), pipeline_mode=pl.Buffered(1)),
            pl.BlockSpec((1, ATTN_W), lambda bi, i: (0, 0)),
            pl.BlockSpec((1, KV_W), lambda bi, i: (0, 0)),
            pl.BlockSpec((tm, LANES), lambda bi, i: (i, 0)),
            pl.BlockSpec((tm, LANES), lambda bi, i: (i, 0)),
            pl.BlockSpec((MXU_DIM, MXU_DIM), lambda bi, i: (0, 0)),
        ],
        out_specs=[
            pl.BlockSpec((1, ATTN_W, tm), lambda bi, i: (bi, 0, i)),
            pl.BlockSpec((1, tm, KV_W), lambda bi, i: (bi, i, 0)),
            pl.BlockSpec((1, N_KV_HEADS, 2 * HEAD_DIM, tm), lambda bi, i: (bi, 0, 0, i)),
            pl.BlockSpec((1, tm, conv_ch), lambda bi, i: (bi, i, 0)),
        ],
        out_shape=[
            jax.ShapeDtypeStruct((b, ATTN_W, n), BF16),
            jax.ShapeDtypeStruct((b, n, KV_W), BF16),
            jax.ShapeDtypeStruct((b, N_KV_HEADS, 2 * HEAD_DIM, n), BF16),
            jax.ShapeDtypeStruct((b, n, conv_ch), F32),
        ],
        compiler_params=_params("arbitrary", "arbitrary"),
        name="inproj0",
    )(h, mods, g1, w_in, qg, kg, cos2, sin2, ones_bd)


def _ctxkv_body(x_ref, mod_ref, g1_ref, w_ref, kg_ref, ones_ref, k_ref, vt_ref):
    hn = _norm_mod(x_ref[0], g1_ref[...], mod_ref[0, 0:1, :], mod_ref[0, 1:2, :])
    proj = jnp.dot(hn.astype(BF16), w_ref[0].astype(BF16), preferred_element_type=F32)
    k = proj[:, :KV_W]
    kn = k * lax.rsqrt(_head_sumsq(k, ones_ref[...][:KV_W, :KV_W]) * (1.0 / HEAD_DIM) + EPS) * kg_ref[...]
    k_ref[0] = kn.astype(BF16)
    _store_vt_with_ones(vt_ref, slice(None), proj[:, KV_W:])


def _ctxkv(ctx, mods, ctx_row, g1, w_in, kg, ones_bd):
    b, t, d = ctx.shape
    assert ATTN_W % (2 * KV_W) == 0
    kv_block = ATTN_W // (2 * KV_W)
    return pl.pallas_call(
        _ctxkv_body,
        grid=(b,),
        in_specs=[
            pl.BlockSpec((1, t, d), lambda bi: (bi, 0, 0)),
            pl.BlockSpec((1, N_MOD, d), lambda bi: (ctx_row, 0, 0)),
            pl.BlockSpec((1, d), lambda bi: (0, 0)),
            pl.BlockSpec((1, d, 2 * KV_W), lambda bi: (0, 0, kv_block)),
            pl.BlockSpec((1, KV_W), lambda bi: (0, 0)),
            pl.BlockSpec((MXU_DIM, MXU_DIM), lambda bi: (0, 0)),
        ],
        out_specs=[
            pl.BlockSpec((1, t, KV_W), lambda bi: (bi, 0, 0)),
            pl.BlockSpec((1, N_KV_HEADS, 2 * HEAD_DIM, t), lambda bi: (bi, 0, 0, 0)),
        ],
        out_shape=[
            jax.ShapeDtypeStruct((b, t, KV_W), BF16),
            jax.ShapeDtypeStruct((b, N_KV_HEADS, 2 * HEAD_DIM, t), BF16),
        ],
        compiler_params=_params("arbitrary"),
        name="ctx_kv",
    )(ctx, mods, g1, w_in, kg, ones_bd)


ATTN_KEY_CHUNK = 512


def _attn_body(qt_ref, kc_ref, k_ref, vtc_ref, vt_ref, o_ref, s0, s1, p0, p1, ot):
    s_bufs = (s0, s1)
    p_bufs = (p0, p1)
    t_ctx = kc_ref.shape[1]
    n = k_ref.shape[1]
    tq = qt_ref.shape[2]

    def column_max(run, sc):
        part = jnp.max(sc.reshape(sc.shape[0] // SUBLANES, SUBLANES, tq), axis=0)
        return part if run is None else jnp.maximum(run, part)

    assert t_ctx % SUBLANES == 0 and n % ATTN_KEY_CHUNK == 0
    chunks = [(0, t_ctx, None)] + [(t_ctx + lo, ATTN_KEY_CHUNK, lo) for lo in range(0, n, ATTN_KEY_CHUNK)]

    def padded_query(h):
        qt = qt_ref[0, h * HEAD_DIM:(h + 1) * HEAD_DIM, :]
        zeros = jnp.zeros_like(qt)
        return jnp.concatenate([qt, zeros] if h // Q_PER_KV == 0 else [zeros, qt], axis=0)

    def score_chunk(h, qpad, run, chunk):
        row, size, lo = chunk
        keys = kc_ref[0] if lo is None else k_ref[0, lo:lo + size, :]
        sc = jnp.dot(keys, qpad, preferred_element_type=F32)
        s_bufs[h % 2][row:row + size, :] = sc
        return column_max(run, sc)

    def exp_chunk(h, m, chunk):
        row, size, _ = chunk
        p_bufs[h % 2][row:row + size, :] = jnp.exp2(s_bufs[h % 2][row:row + size, :] - m).astype(BF16)

    def value_chunk(h, acc, chunk):
        row, size, lo = chunk
        j = h // Q_PER_KV
        vt = vtc_ref[0, j] if lo is None else vt_ref[0, j, :, lo:lo + size]
        part = jnp.dot(vt, p_bufs[h % 2][row:row + size, :], preferred_element_type=F32)
        return part if acc is None else acc + part

    m = None
    for h in range(-1, N_Q_HEADS + 1):
        qpad = padded_query(h + 1) if h + 1 < N_Q_HEADS else None
        run = acc = None
        for chunk in chunks:
            if qpad is not None:
                run = score_chunk(h + 1, qpad, run, chunk)
            if 0 <= h - 1:
                acc = value_chunk(h - 1, acc, chunk)
            if 0 <= h < N_Q_HEADS:
                exp_chunk(h, m, chunk)
        if acc is not None:
            ot[(h - 1) * HEAD_DIM:h * HEAD_DIM, :] = acc[:HEAD_DIM, :] / acc[HEAD_DIM:HEAD_DIM + 1, :]
        m = jnp.max(run, axis=0, keepdims=True) if run is not None else None
    o_ref[0] = ot[...].T.astype(BF16)


def _attention(qt, k_ctx, k, vt_ctx, vt, tq=512):
    b, width, n = qt.shape
    t_ctx = k_ctx.shape[1]
    t_all = t_ctx + n
    return pl.pallas_call(
        _attn_body,
        grid=(b, n // tq),
        scratch_shapes=[pltpu.VMEM((t_all, tq), F32), pltpu.VMEM((t_all, tq), F32),
                        pltpu.VMEM((t_all, tq), BF16), pltpu.VMEM((t_all, tq), BF16),
                        pltpu.VMEM((width, tq), F32)],
        in_specs=[
            pl.BlockSpec((1, width, tq), lambda bi, i: (bi, 0, i)),
            pl.BlockSpec((1, t_ctx, KV_W), lambda bi, i: (bi, 0, 0)),
            pl.BlockSpec((1, n, KV_W), lambda bi, i: (bi, 0, 0)),
            pl.BlockSpec((1, N_KV_HEADS, 2 * HEAD_DIM, t_ctx), lambda bi, i: (bi, 0, 0, 0)),
            pl.BlockSpec((1, N_KV_HEADS, 2 * HEAD_DIM, n), lambda bi, i: (bi, 0, 0, 0)),
        ],
        out_specs=pl.BlockSpec((1, tq, width), lambda bi, i: (bi, i, 0)),
        out_shape=jax.ShapeDtypeStruct((b, n, width), BF16),
        compiler_params=_params("arbitrary", "arbitrary"),
        name="attention",
    )(qt, k_ctx, k, vt_ctx, vt)


CONV_ROWS = 64


def _conformer_rows(window, w_ref, cb_ref, lg_ref, lb_ref):
    tm = window.shape[0] - 2 * CONV_HALO
    ch = window.shape[1]
    win = CONV_ROWS + 2 * CONV_HALO
    off = CONV_HALO - CONF_K // 2
    cols = []
    for t in range(ch // LANES):
        lanes = slice(t * LANES, (t + 1) * LANES)
        blocks = []
        for r in range(tm // CONV_ROWS):
            x = window[r * CONV_ROWS:r * CONV_ROWS + win, lanes]
            acc = jnp.zeros((CONV_ROWS, LANES), F32)
            for sub in range(SUBLANES):
                xs = x if sub == 0 else pltpu.roll(x, win - sub, axis=0)
                for a in range(2 * CONV_HALO // SUBLANES):
                    k = a * SUBLANES + sub - off
                    if 0 <= k < CONF_K:
                        acc = acc + xs[a * SUBLANES:a * SUBLANES + CONV_ROWS, :] * w_ref[k:k + 1, lanes]
            blocks.append(acc)
        cols.append(jnp.concatenate(blocks, axis=0))
    y = jnp.concatenate(cols, axis=1) + cb_ref[...]
    mu = jnp.mean(y, axis=-1, keepdims=True)
    yc = y - mu
    var = jnp.mean(yc * yc, axis=-1, keepdims=True)
    return _silu(yc * lax.rsqrt(var + EPS) * lg_ref[...] + lb_ref[...])


def _residual_router(mix, h, mod_ref, g2_ref, wr_ref, h_out, hm_out, lg_out, rows=slice(None)):
    h1 = h + mod_ref[0, 2:3, :] * mix
    h_out[0, rows] = h1
    hm = _norm_mod(h1, g2_ref[...], mod_ref[0, 3:4, :], mod_ref[0, 4:5, :])
    hm_out[0, rows] = _rows_to_tiles(hm)
    lg_out[0, rows] = jnp.dot(hm.astype(BF16), wr_ref[...], preferred_element_type=F32)


def _outproj0_body(attn_ref, glu_ref, glup_ref, glun_ref, cw_ref, cb_ref, lg_ref, lb_ref, w_ref,
                   h_ref, mod_ref, g2_ref, wr_ref, h_out, hm_out, lg_out):
    i = pl.program_id(1)
    last = pl.num_programs(1) - 1
    before = jnp.where(i > 0, glup_ref[0], 0.0)
    after = jnp.where(i < last, glun_ref[0], 0.0)
    conf = _conformer_rows(jnp.concatenate([before, glu_ref[0], after], axis=0), cw_ref, cb_ref, lg_ref, lb_ref)
    a = jnp.concatenate([attn_ref[0], conf.astype(BF16)], axis=1)
    mix = jnp.dot(a, w_ref[...], preferred_element_type=F32)
    _residual_router(mix, h_ref[0], mod_ref, g2_ref, wr_ref, h_out, hm_out, lg_out)


def _tile_spec(tm, d):
    assert d == SUBLANES * LANES
    return pl.BlockSpec((1, tm, SUBLANES, LANES), lambda bi, i: (bi, i, 0, 0))


def _token_out_specs(b, n, d, tm):
    specs = [
        pl.BlockSpec((1, tm, d), lambda bi, i: (bi, i, 0)),
        _tile_spec(tm, d),
        pl.BlockSpec((1, tm, LANES), lambda bi, i: (bi, i, 0)),
    ]
    shapes = [
        jax.ShapeDtypeStruct((b, n, d), F32),
        jax.ShapeDtypeStruct((b, n, SUBLANES, d // SUBLANES), F32),
        jax.ShapeDtypeStruct((b, n, LANES), F32),
    ]
    return specs, shapes


def _outproj0(attn, glu, conv_w, conv_b, ln_g, ln_b, w_out, h, mods, g2, w_r, tm=512):
    b, n, d = h.shape
    ch = glu.shape[-1]
    per = tm // CONV_HALO
    n_halo = n // CONV_HALO
    vec = pl.BlockSpec((1, ch), lambda bi, i: (0, 0))
    out_specs, out_shape = _token_out_specs(b, n, d, tm)
    return pl.pallas_call(
        _outproj0_body,
        grid=(b, n // tm),
        in_specs=[
            pl.BlockSpec((1, tm, ATTN_W), lambda bi, i: (bi, i, 0)),
            pl.BlockSpec((1, tm, ch), lambda bi, i: (bi, i, 0)),
            pl.BlockSpec((1, CONV_HALO, ch), lambda bi, i: (bi, jnp.maximum(i * per - 1, 0), 0)),
            pl.BlockSpec((1, CONV_HALO, ch), lambda bi, i: (bi, jnp.minimum((i + 1) * per, n_halo - 1), 0)),
            pl.BlockSpec((CONF_K, ch), lambda bi, i: (0, 0)),
            vec, vec, vec,
            pl.BlockSpec((d, d), lambda bi, i: (0, 0)),
            pl.BlockSpec((1, tm, d), lambda bi, i: (bi, i, 0)),
            pl.BlockSpec((1, N_MOD, d), lambda bi, i: (bi, 0, 0)),
            pl.BlockSpec((1, d), lambda bi, i: (0, 0)),
            pl.BlockSpec((d, LANES), lambda bi, i: (0, 0)),
        ],
        out_specs=out_specs,
        out_shape=out_shape,
        compiler_params=_params("arbitrary", "arbitrary"),
        name="outproj0",
    )(attn, glu, glu, glu, conv_w, conv_b, ln_g, ln_b, w_out, h, mods, g2, w_r)


def _mixer1_body(h_ref, hp_ref, hn_ref, moe_ref, moep_ref, moen_ref, modp_ref, mod_ref, g1_ref, w_ref,
                 cw_ref, wo_ref, g2_ref, wr_ref, h_out, hm_out, lg_out):
    i = pl.program_id(1)
    last = pl.num_programs(1) - 1
    tm, d = h_ref.shape[1:]
    gate_prev = modp_ref[0, 5:6, :]

    def stream(h_blk, moe_blk, rows=slice(None)):
        return h_blk[0, rows] + gate_prev * _tiles_to_rows(moe_blk[0, rows])

    part = tm // MIXER_PARTS
    win = part + 2 * SUBLANES
    for p in range(MIXER_PARTS):
        rows = slice(p * part, (p + 1) * part)
        h = stream(h_ref, moe_ref, rows)
        above = (stream(hp_ref, moep_ref) if p == 0
                 else stream(h_ref, moe_ref, slice(p * part - SUBLANES, p * part)))
        below = (stream(hn_ref, moen_ref) if p == MIXER_PARTS - 1
                 else stream(h_ref, moe_ref, slice((p + 1) * part, (p + 1) * part + SUBLANES)))
        hn = _norm_mod(jnp.concatenate([above, h, below], axis=0), g1_ref[...], mod_ref[0, 0:1, :], mod_ref[0, 1:2, :])
        proj = jnp.dot(hn.astype(BF16), w_ref[0].astype(BF16), preferred_element_type=F32)
        z = proj[:, d:2 * d] * proj[:, 2 * d:]
        r = lax.broadcasted_iota(jnp.int32, z.shape, 0)
        if p == 0:
            z = jnp.where((r < SUBLANES) & (i == 0), 0.0, z)
        if p == MIXER_PARTS - 1:
            z = jnp.where((r >= part + SUBLANES) & (i == last), 0.0, z)
        y = (pltpu.roll(z, 1, axis=0) * cw_ref[0:1, :] + z * cw_ref[1:2, :]
             + pltpu.roll(z, win - 1, axis=0) * cw_ref[2:3, :])[SUBLANES:SUBLANES + part]
        gated = proj[SUBLANES:SUBLANES + part, :d] * y
        mix = jnp.dot(gated.astype(BF16), wo_ref[0].astype(BF16), preferred_element_type=F32)
        _residual_router(mix, h, mod_ref, g2_ref, wr_ref, h_out, hm_out, lg_out, rows)


def _mixer1(h, moe, mods_prev, mods, g1, w_in, conv_w, w_out, g2, w_r, tm=512):
    b, n, d = h.shape
    per = tm // SUBLANES
    n8 = n // SUBLANES
    before = lambda bi, i: (bi, jnp.maximum(i * per - 1, 0), 0)
    after = lambda bi, i: (bi, jnp.minimum((i + 1) * per, n8 - 1), 0)
    tok = pl.BlockSpec((1, tm, d), lambda bi, i: (bi, i, 0))
    modspec = pl.BlockSpec((1, N_MOD, d), lambda bi, i: (bi, 0, 0))
    vec = pl.BlockSpec((1, d), lambda bi, i: (0, 0))
    out_specs, out_shape = _token_out_specs(b, n, d, tm)
    return pl.pallas_call(
        _mixer1_body,
        grid=(b, n // tm),
        in_specs=[
            tok,
            pl.BlockSpec((1, SUBLANES, d), before),
            pl.BlockSpec((1, SUBLANES, d), after),
            _tile_spec(tm, d),
            pl.BlockSpec((1, SUBLANES, SUBLANES, LANES), lambda bi, i: before(bi, i) + (0,)),
            pl.BlockSpec((1, SUBLANES, SUBLANES, LANES), lambda bi, i: after(bi, i) + (0,)),
            modspec, modspec, vec,
            pl.BlockSpec((1, d, 3 * d), lambda bi, i: (0, 0, 0), pipeline_mode=pl.Buffered(1)),
            pl.BlockSpec((SC_K, d), lambda bi, i: (0, 0)),
            pl.BlockSpec((1, d, d), lambda bi, i: (0, 0, 0), pipeline_mode=pl.Buffered(1)),
            vec,
            pl.BlockSpec((d, LANES), lambda bi, i: (0, 0)),
        ],
        out_specs=out_specs,
        out_shape=out_shape,
        compiler_params=_params("arbitrary", "arbitrary"),
        name="mixer1",
    )(h, h, h, moe, moe, moe, mods_prev, mods, g1, w_in, conv_w, w_out, g2, w_r)


def _stack_chunks(x):
    n = x.shape[1]
    return jnp.concatenate([x[:, c * ROUTER_CHUNK:(c + 1) * ROUTER_CHUNK] for c in range(n // ROUTER_CHUNK)], axis=0)


def _exclusive_rank(flags, utri, chunk_lt):
    incl = jnp.dot(flags.astype(BF16), utri, preferred_element_type=F32)
    tot = jnp.broadcast_to(incl[:, ROUTER_CHUNK - 1:ROUTER_CHUNK], incl.shape)
    base = jnp.dot(chunk_lt, tot.astype(BF16), preferred_element_type=F32)
    return incl - flags + base


def _router_body(lg_ref, utri_ref, lt_ref, blk_ref, idx_ref, aff_ref, loc_scr, tot_scr, end_scr, *, cap):
    n = lg_ref.shape[1]
    n_chunks = n // ROUTER_CHUNK
    lg = lg_ref[0]
    lane = lax.broadcasted_iota(jnp.int32, lg.shape, 1)
    valid = lane < N_EXPERTS
    x = jnp.where(valid, lg, -jnp.inf)
    ex = jnp.where(valid, jnp.exp(x - jnp.max(x, axis=-1, keepdims=True)), 0.0)
    aff = ex / jnp.sum(ex, axis=-1, keepdims=True)

    aff_t = aff.T[:N_EXPERTS, :]
    aff_ref[0] = aff_t

    def count_ge(t):
        return jnp.sum(jnp.where(aff_t >= t, 1.0, 0.0), axis=-1, keepdims=True)

    def bit_step(i, bits):
        cand = bits | jnp.left_shift(jnp.int32(1), 30 - i)
        return jnp.where(count_ge(lax.bitcast_convert_type(cand, F32)) >= cap, cand, bits)

    bits = lax.fori_loop(0, 31, bit_step, jnp.zeros((N_EXPERTS, 1), jnp.int32))

    def refine(i, lo_hi):
        lo, hi = lo_hi
        mid = (lo + hi) * 0.5
        ok = count_ge(mid) >= cap
        return jnp.where(ok, mid, lo), jnp.where(ok, hi, mid)

    thr, _ = lax.fori_loop(0, REFINE_STEPS, refine,
                           (lax.bitcast_convert_type(bits, F32), lax.bitcast_convert_type(bits + 1, F32)))
    gt = jnp.where(aff_t > thr, 1.0, 0.0)
    eq = jnp.where(aff_t == thr, 1.0, 0.0)
    need = cap - jnp.sum(gt, axis=-1, keepdims=True)
    utri = utri_ref[...]
    chunk_lt = lt_ref[...]
    gt_s = _stack_chunks(gt)
    eq_s = _stack_chunks(eq)
    need_s = jnp.concatenate([need] * n_chunks, axis=0)
    sel = jnp.where((gt_s > 0) | ((eq_s > 0) & (_exclusive_rank(eq_s, utri, chunk_lt) < need_s)), 1.0, 0.0)

    loc_scr[...] = jnp.dot(sel.astype(BF16), utri, preferred_element_type=F32).astype(BF16)
    sel_t = jnp.concatenate([sel[c * N_EXPERTS:(c + 1) * N_EXPERTS, :] for c in range(n_chunks)], axis=1)
    tot = jnp.dot(sel_t.astype(BF16), blk_ref[...], preferred_element_type=F32)
    tot_scr[...] = tot
    end_scr[...] = jnp.dot(tot.astype(BF16), utri, preferred_element_type=F32)

    slot = lax.broadcasted_iota(jnp.int32, (cap, 1), 0).astype(F32)
    row_id = lax.broadcasted_iota(jnp.int32, (cap, n_chunks * N_EXPERTS), 1)

    def one_expert(e, carry):
        ends = end_scr[pl.ds(e, 1), :]
        before = ends <= slot
        chunk = jnp.sum(jnp.where(before, 1.0, 0.0), axis=-1, keepdims=True)
        base = jnp.sum(jnp.where(before, tot_scr[pl.ds(e, 1), :], 0.0), axis=-1, keepdims=True)
        pick = jnp.where(row_id == chunk.astype(jnp.int32) * N_EXPERTS + e, 1.0, 0.0).astype(BF16)
        counts = jnp.dot(pick, loc_scr[...], preferred_element_type=F32)
        inside = jnp.sum(jnp.where(counts <= slot - base, 1.0, 0.0), axis=-1, keepdims=True)
        idx_ref[0, e] = (chunk * ROUTER_CHUNK + inside).astype(jnp.int32)
        return carry

    lax.fori_loop(0, N_EXPERTS, one_expert, 0)


def _router(logits, utri, chunk_lt, chunk_of_token, cap):
    b, n, _ = logits.shape
    rows = (n // ROUTER_CHUNK) * N_EXPERTS
    assert n // ROUTER_CHUNK <= LANES
    return pl.pallas_call(
        functools.partial(_router_body, cap=cap),
        grid=(b,),
        in_specs=[
            pl.BlockSpec((1, n, LANES), lambda bi: (bi, 0, 0)),
            pl.BlockSpec((ROUTER_CHUNK, ROUTER_CHUNK), lambda bi: (0, 0)),
            pl.BlockSpec((rows, rows), lambda bi: (0, 0)),
            pl.BlockSpec((n, LANES), lambda bi: (0, 0)),
        ],
        out_specs=[
            pl.BlockSpec((1, N_EXPERTS, cap, 1), lambda bi: (bi, 0, 0, 0)),
            pl.BlockSpec((1, N_EXPERTS, n), lambda bi: (bi, 0, 0)),
        ],
        out_shape=[
            jax.ShapeDtypeStruct((b, N_EXPERTS, cap, 1), jnp.int32),
            jax.ShapeDtypeStruct((b, N_EXPERTS, n), F32),
        ],
        scratch_shapes=[pltpu.VMEM((rows, ROUTER_CHUNK), BF16), pltpu.VMEM((N_EXPERTS, LANES), F32),
                        pltpu.VMEM((N_EXPERTS, LANES), F32)],
        compiler_params=_params("arbitrary"),
        name="router",
    )(logits, utri, chunk_lt, chunk_of_token)


GATHER_UNROLL = 16
MOE_EXPERTS_PER_STEP = 1
MOE_SLOT_SPLIT = 1


def _gather_body(idx_ref, hm_ref, o_ref):
    cap = o_ref.shape[2]

    for k in range(o_ref.shape[1]):
        def group(g, carry, k=k):
            base = pl.multiple_of(g * GATHER_UNROLL, GATHER_UNROLL)
            halves = []
            for half in range(GATHER_UNROLL // SUBLANES):
                tiles = [hm_ref[0, idx_ref[k, 0, base + half * SUBLANES + u]][None] for u in range(SUBLANES)]
                halves.append(_tile_transpose8(tiles))
            rows = jnp.concatenate([jnp.concatenate([h[a][0] for h in halves], axis=0) for a in range(SUBLANES)],
                                   axis=1)
            o_ref[0, k, pl.ds(base, GATHER_UNROLL), :] = rows.astype(BF16)
            return carry

        lax.fori_loop(0, cap // GATHER_UNROLL, group, 0)


def _gather(idx, hm):
    b, n = hm.shape[:2]
    groups, cap = idx.shape[0] // b, idx.shape[-1]
    d = SUBLANES * LANES
    eps = MOE_EXPERTS_PER_STEP
    steps = groups // eps
    return pl.pallas_call(
        _gather_body,
        grid=(b, steps),
        in_specs=[
            pl.BlockSpec((eps, 1, cap), lambda bi, e: (bi * steps + e, 0, 0), memory_space=pltpu.SMEM),
            pl.BlockSpec((1, n, SUBLANES, LANES), lambda bi, e: (bi, 0, 0, 0)),
        ],
        out_specs=pl.BlockSpec((1, eps, cap, d), lambda bi, e: (bi, e, 0, 0)),
        out_shape=jax.ShapeDtypeStruct((b, groups, cap, d), BF16),
        compiler_params=_params("arbitrary", "arbitrary"),
        name="moe_gather",
    )(idx, hm)


FFN_SAMPLES_PER_STEP = 2


def _ffn_body(x_ref, wg_ref, wu_ref, wd_ref, o_ref):
    cap = x_ref.shape[2]
    x = jnp.concatenate([x_ref[s, 0] for s in range(x_ref.shape[0])], axis=0)
    hg = jnp.dot(x, wg_ref[0, 0].astype(BF16), preferred_element_type=F32)
    hu = jnp.dot(x, wu_ref[0, 0].astype(BF16), preferred_element_type=F32)
    y = jnp.dot((_silu(hg) * hu).astype(BF16), wd_ref[0, 0].astype(BF16), preferred_element_type=F32)
    for s in range(x_ref.shape[0]):
        o_ref[s, 0] = y[s * cap:(s + 1) * cap]


def _expert_ffn(xs, w_gate, w_up, w_down, layer):
    b, n_e, cap, d = xs.shape
    f = w_gate.shape[-1]
    sps = FFN_SAMPLES_PER_STEP if b % FFN_SAMPLES_PER_STEP == 0 else 1
    rows = pl.BlockSpec((sps, 1, cap, d), lambda e, bi: (bi, e, 0, 0))
    return pl.pallas_call(
        _ffn_body,
        grid=(n_e, b // sps),
        in_specs=[
            rows,
            pl.BlockSpec((1, 1, d, f), lambda e, bi: (layer, e, 0, 0)),
            pl.BlockSpec((1, 1, d, f), lambda e, bi: (layer, e, 0, 0)),
            pl.BlockSpec((1, 1, f, d), lambda e, bi: (layer, e, 0, 0)),
        ],
        out_specs=rows,
        out_shape=jax.ShapeDtypeStruct(xs.shape, F32),
        compiler_params=_params("arbitrary", "arbitrary"),
        name="moe_ffn",
    )(xs, w_gate, w_up, w_down)


SCATTER_UNROLL = 8


def _scatter_body(idx_ref, aff_ref, ys_ref, o_ref):
    cap = ys_ref.shape[2]

    @pl.when(pl.program_id(1) == 0)
    def _():
        o_ref[...] = jnp.zeros(o_ref.shape, F32)

    for k in range(ys_ref.shape[1]):
        def group(g, carry, k=k):
            base = pl.multiple_of(g * SCATTER_UNROLL, SCATTER_UNROLL)
            tiles = []
            for part in range(SCATTER_UNROLL // SUBLANES):
                y = ys_ref[0, k, pl.ds(base + part * SUBLANES, SUBLANES), :]
                tiles += _tile_transpose8([y[:, a * LANES:(a + 1) * LANES][None] for a in range(SUBLANES)])
            rows = [idx_ref[k, 0, base + u] for u in range(SCATTER_UNROLL)]
            sums = [o_ref[0, rows[u]] + tiles[u][0] * aff_ref[k, 0, rows[u]] for u in range(SCATTER_UNROLL)]
            for u in range(SCATTER_UNROLL):
                o_ref[0, rows[u]] = sums[u]
            return carry

        lax.fori_loop(0, cap // SCATTER_UNROLL, group, 0)


def _scatter(idx, aff, ys, n):
    b, groups, cap, d = ys.shape
    assert SCATTER_UNROLL % SUBLANES == 0 and cap % SCATTER_UNROLL == 0 and d == SUBLANES * LANES
    eps = MOE_EXPERTS_PER_STEP
    steps = groups // eps
    split = groups // N_EXPERTS
    assert split == 1 or eps == 1
    return pl.pallas_call(
        _scatter_body,
        grid=(b, steps),
        in_specs=[
            pl.BlockSpec((eps, 1, cap), lambda bi, e: (bi * steps + e, 0, 0), memory_space=pltpu.SMEM),
            pl.BlockSpec((eps, 1, n), lambda bi, e: ((bi * steps + e) // split, 0, 0), memory_space=pltpu.SMEM),
            pl.BlockSpec((1, eps, cap, d), lambda bi, e: (bi, e, 0, 0)),
        ],
        out_specs=pl.BlockSpec((1, n, SUBLANES, LANES), lambda bi, e: (bi, 0, 0, 0)),
        out_shape=jax.ShapeDtypeStruct((b, n, SUBLANES, LANES), F32),
        compiler_params=_params("arbitrary", "arbitrary"),
        name="moe_scatter",
    )(idx, aff, ys)


def _ec_moe(hm, logits, w_gate, w_up, w_down, layer, tables):
    b, n = hm.shape[:2]
    cap = max(1, EC_CAPACITY * n // N_EXPERTS)
    idx, aff = _router(logits, *tables, cap)
    split = MOE_SLOT_SPLIT
    idx = idx.reshape(b * N_EXPERTS * split, 1, cap // split)
    xs = _gather(idx, hm).reshape(b, N_EXPERTS, cap, -1)
    ys = _expert_ffn(xs, w_gate, w_up, w_down, layer)
    return _scatter(idx, aff.reshape(b * N_EXPERTS, 1, n), ys.reshape(b, N_EXPERTS * split, cap // split, -1), n)


def _final_body(h_ref, moe_ref, mod_ref, g_ref, o_ref):
    h = h_ref[0] + mod_ref[0, 5:6, :] * _tiles_to_rows(moe_ref[0])
    o_ref[0] = h * lax.rsqrt(jnp.mean(h * h, axis=-1, keepdims=True) + EPS) * g_ref[...]


def _final(h, moe, mods, g, tm=2048):
    b, n, d = h.shape
    tok = pl.BlockSpec((1, tm, d), lambda bi, i: (bi, i, 0))
    return pl.pallas_call(
        _final_body,
        grid=(b, n // tm),
        in_specs=[tok, _tile_spec(tm, d), pl.BlockSpec((1, N_MOD, d), lambda bi, i: (bi, 0, 0)),
                  pl.BlockSpec((1, d), lambda bi, i: (0, 0))],
        out_specs=tok,
        out_shape=jax.ShapeDtypeStruct((b, n, d), F32),
        compiler_params=_params("arbitrary", "arbitrary"),
        name="final_norm",
    )(h, moe, mods, g)


def _rope_tables(n):
    rows = n // GRID_W
    row = np.repeat(np.arange(rows, dtype=np.float64), GRID_W)
    col = np.tile(np.arange(GRID_W, dtype=np.float64), rows)
    axis_dim = HEAD_DIM // 2
    inv = ROPE_THETA ** (-np.arange(0, axis_dim, 2, dtype=np.float64) / axis_dim)
    ang = np.concatenate([row[:, None] * inv, col[:, None] * inv], axis=-1)
    cos = np.cos(ang)
    sin = np.sin(ang)
    cos64 = np.concatenate([cos, cos], axis=-1)
    sin64 = np.concatenate([-sin, sin], axis=-1)
    reps = LANES // HEAD_DIM
    return jnp.asarray(np.tile(cos64, (1, reps)), F32), jnp.asarray(np.tile(sin64, (1, reps)), F32)


def _router_tables(n):
    r = np.arange(ROUTER_CHUNK)
    utri = r[:, None] <= r[None, :]
    rows = np.arange((n // ROUTER_CHUNK) * N_EXPERTS)
    same_e = (rows[:, None] % N_EXPERTS) == (rows[None, :] % N_EXPERTS)
    earlier = (rows[None, :] // N_EXPERTS) < (rows[:, None] // N_EXPERTS)
    chunk_of_token = (np.arange(n)[:, None] // ROUTER_CHUNK) == np.arange(LANES)[None, :]
    return tuple(jnp.asarray(t.astype(np.float32), BF16) for t in (utri, same_e & earlier, chunk_of_token))


def _head_ones():
    r = np.arange(MXU_DIM) // HEAD_DIM
    return jnp.asarray((r[:, None] == r[None, :]).astype(np.float32), BF16)


def kernel(x, c, ctx, c_ctx, ada_w, ada_b, norm1_g, norm2_g, ev_w_in, ev_q_g, ev_k_g, ev_conv_w, ev_conv_b,
           ev_ln_g, ev_ln_b, ev_w_out, sc_w_in, sc_conv_w, sc_w_out, moe_w_r, moe_w_gate, moe_w_up,
           moe_w_down, final_g):
    b, n, d = x.shape
    depth = ada_w.shape[0]
    assert depth == 2 and b < SUBLANES and n % ROUTER_CHUNK == 0

    cos2, sin2 = _rope_tables(n)
    tables = _router_tables(n)
    ones_bd = _head_ones()

    cvecs = jnp.zeros((SUBLANES, d), F32).at[:b].set(c).at[b].set(c_ctx)
    mods = _ada_mod(cvecs, ada_w, ada_b).reshape(depth, SUBLANES, N_MOD, d)
    w_r = jnp.pad(moe_w_r, ((0, 0), (0, 0), (0, LANES - N_EXPERTS))).astype(BF16)

    qg = jnp.tile(ev_q_g[0], N_Q_HEADS)[None, :]
    kg = jnp.tile(ev_k_g[0], N_KV_HEADS)[None, :]
    qt, k, vt, glu = _inproj0(x, mods[0], norm1_g[0:1], ev_w_in, qg, kg, cos2, sin2, ones_bd)
    k_ctx, vt_ctx = _ctxkv(ctx, mods[0], b, norm1_g[0:1], ev_w_in, kg, ones_bd)
    attn = _attention(qt, k_ctx, k, vt_ctx, vt)
    h, hm, logits = _outproj0(attn, glu, ev_conv_w[0], ev_conv_b[0:1], ev_ln_g[0:1], ev_ln_b[0:1],
                              ev_w_out[0].astype(BF16), x, mods[0], norm2_g[0:1], w_r[0])
    moe0 = _ec_moe(hm, logits, moe_w_gate, moe_w_up, moe_w_down, 0, tables)

    h, hm, logits = _mixer1(h, moe0, mods[0], mods[1], norm1_g[1:2], sc_w_in, sc_conv_w[0], sc_w_out,
                            norm2_g[1:2], w_r[1])
    moe1 = _ec_moe(hm, logits, moe_w_gate, moe_w_up, moe_w_down, 1, tables)

    return _final(h, moe1, mods[1], final_g[None, :])
```
